```python
import math
import jax, jax.numpy as jnp
from jax import lax
import numpy as np

D_MODEL = 1024
BATCH = 8
SEQ = 2048
DEPTH = 2
DEC_BATCH = 32
DEC_SEQ = 4
PAST_LEN = 16384
PAGE_SIZE = 128

N_MIXERS = 2
N_SSM_LAYERS = (DEPTH + 1) // 2
N_NSA_LAYERS = DEPTH // 2
MIX_WIDTH = D_MODEL
N_MEM = 256
MEM_HEADS = 4
MEM_HEAD_DIM = 64
MEM_WIDTH = MEM_HEADS * MEM_HEAD_DIM
SSM_WIDTH = MIX_WIDTH - MEM_WIDTH
SSM_GROUP = 16
SSM_GROUPS = SSM_WIDTH // SSM_GROUP
SSM_STATE = 64
DT_MIN = 1e-3
DT_MAX = 1e-1
NSA_HEAD_DIM = 64
NSA_HEADS = (MIX_WIDTH - MEM_WIDTH) // NSA_HEAD_DIM
NSA_KV_HEADS = 3
NSA_GROUP = NSA_HEADS // NSA_KV_HEADS
NSA_WIDTH = NSA_HEADS * NSA_HEAD_DIM
NSA_KV_WIDTH = NSA_KV_HEADS * 2 * NSA_HEAD_DIM
CMP_BLOCK = 32
CMP_STRIDE = 16
CMP_HIDDEN = 2 * NSA_HEAD_DIM
SEL_BLOCK = 64
SEL_TOP = 16
WINDOW = 512
SLC_Q_BLOCK = 32
WIN_Q_BLOCK = 128
SSM_IN = SSM_WIDTH + MEM_WIDTH
NSA_IN = NSA_WIDTH + 3 * NSA_KV_WIDTH + 3 * NSA_HEADS + MEM_WIDTH
D_FF = -(-8 * D_MODEL // (3 * 256)) * 256
RMS_EPS = 1e-6
NEG = -1e30
BIG = 1e30
SCALE = NSA_HEAD_DIM ** -0.5

kernel_name = 'hybrid_s5_nsa_memory_decoder_step'

F32 = jnp.float32


def rmsnorm(x, g):
    xf = x.astype(F32)
    y = xf * lax.rsqrt(jnp.mean(xf * xf, axis=-1, keepdims=True) + RMS_EPS)
    return (y * g.astype(F32)).astype(x.dtype)


def masked_softmax(s, mask):
    s = jnp.where(mask, s, NEG)
    p = jnp.exp(s - jnp.max(s, axis=-1, keepdims=True)) * mask
    return p / jnp.maximum(jnp.sum(p, axis=-1, keepdims=True), 1e-30)


def alibi_slopes():
    h = jnp.arange(NSA_HEADS, dtype=F32) + 1.0
    return jnp.exp2(-8.0 * h / NSA_HEADS).reshape(NSA_KV_HEADS, NSA_GROUP)


def swiglu_ffn(x, w_in, w_out):
    g, u = jnp.split(x @ w_in, 2, axis=-1)
    return (jax.nn.silu(g) * u) @ w_out


def mem_kv_proj(mem, w):
    return (mem @ w).reshape(mem.shape[0], mem.shape[1], MEM_HEADS, 2, MEM_HEAD_DIM)


def mem_attention(q, mem_kv):
    s = jnp.einsum('bthd,bnhd->bthn', q, mem_kv[..., 0, :]).astype(F32) * MEM_HEAD_DIM ** -0.5
    p = jax.nn.softmax(s, axis=-1)
    o = jnp.einsum('bthn,bnhd->bthd', p.astype(mem_kv.dtype), mem_kv[..., 1, :])
    return o.reshape(q.shape[0], q.shape[1], MEM_WIDTH)


def _linear_combine(e1, e2):
    a1, b1 = e1
    a2, b2 = e2
    return a1 * a2, a2 * b1 + b2


def s5_ssm(u, h0, a_re, a_im, log_dt, b_re, b_im, c_re, c_im, d, w_glu, b_glu):
    B, L, _ = u.shape
    lam = lax.complex(a_re.astype(F32), a_im.astype(F32))
    dt = jnp.exp(log_dt.astype(F32))[:, None]
    abar = jnp.exp(lam * dt)
    bbar = ((abar - 1.0) / lam)[..., None] * lax.complex(b_re.astype(F32), b_im.astype(F32))
    ug = u.astype(F32).reshape(B, L, SSM_GROUPS, SSM_GROUP)
    bu = lax.complex(jnp.einsum('blgj,gpj->blgp', ug, jnp.real(bbar)),
                     jnp.einsum('blgj,gpj->blgp', ug, jnp.imag(bbar)))
    bu = bu.at[:, 0].add(abar * h0)
    a = jnp.broadcast_to(abar, (1, L) + abar.shape)
    _, h = lax.associative_scan(_linear_combine, (a, bu), axis=1)
    y = (jnp.einsum('blgp,gjp->blgj', jnp.real(h), c_re.astype(F32))
         - jnp.einsum('blgp,gjp->blgj', jnp.imag(h), c_im.astype(F32)))
    y = y.reshape(B, L, SSM_WIDTH) + d.astype(F32) * u.astype(F32)
    y = jax.nn.gelu(y)
    y = y * jax.nn.sigmoid(y @ w_glu.astype(F32) + b_glu.astype(F32))
    return y.astype(u.dtype), h[:, -1]


def s5_mixer(h, h0, mem_kv, w_in, a_re, a_im, log_dt, b_re, b_im, c_re, c_im, d, w_glu, b_glu):
    B, T = h.shape[:2]
    z = h @ w_in
    y, h_last = s5_ssm(z[..., :SSM_WIDTH], h0, a_re, a_im, log_dt, b_re, b_im, c_re, c_im, d, w_glu, b_glu)
    om = mem_attention(z[..., SSM_WIDTH:].reshape(B, T, MEM_HEADS, MEM_HEAD_DIM), mem_kv)
    return jnp.concatenate([y, om.astype(y.dtype)], axis=-1), h_last


def nsa_split(z):
    B, T = z.shape[:2]
    q = z[..., :NSA_WIDTH].reshape(B, T, NSA_KV_HEADS, NSA_GROUP, NSA_HEAD_DIM)
    o = NSA_WIDTH
    kvs = []
    for _ in range(3):
        kvs.append(z[..., o:o + NSA_KV_WIDTH].reshape(B, T, NSA_KV_HEADS, 2, NSA_HEAD_DIM))
        o += NSA_KV_WIDTH
    gates = jax.nn.sigmoid(z[..., o:o + 3 * NSA_HEADS].astype(F32)).reshape(B, T, 3, NSA_KV_HEADS, NSA_GROUP)
    o += 3 * NSA_HEADS
    qm = z[..., o:].reshape(B, T, MEM_HEADS, MEM_HEAD_DIM)
    return q, kvs[0], kvs[1], kvs[2], gates, qm


def cmp_subblock_proj(kv, w1):
    B, T = kv.shape[:2]
    sub = kv.reshape(B, T // CMP_STRIDE, CMP_STRIDE, NSA_KV_HEADS, 2, NSA_HEAD_DIM)
    w = w1.reshape(CMP_BLOCK // CMP_STRIDE, CMP_STRIDE, 2, NSA_HEAD_DIM, CMP_HIDDEN)
    return jnp.einsum('bnjhce,ajcef->bnhacf', sub, w)


def cmp_tokens(proj, pe, w1, b1, w2):
    bias = jnp.einsum('jce,jcef->cf', pe, w1) + b1
    hid = proj[:, :-1, :, 0] + proj[:, 1:, :, 1] + bias
    return jnp.einsum('bnhcf,cfe->bnhce', jax.nn.gelu(hid), w2)


def cmp_attend(q, q_pos, ckv, slopes):
    n = ckv.shape[1]
    e_pos = (jnp.arange(n) + CMP_BLOCK // CMP_STRIDE) * CMP_STRIDE - 1
    dist = q_pos[:, None] - e_pos[None, :]
    s = (jnp.einsum('bqhgd,bnhd->bqhgn', q, ckv[..., 0, :]).astype(F32) * SCALE
         - slopes[None, :, :, None] * dist.astype(F32)[:, None, None, :])
    p = masked_softmax(s, (dist >= 0)[:, None, None, :])
    o = jnp.einsum('bqhgn,bnhd->bqhgd', p.astype(ckv.dtype), ckv[..., 1, :])
    return o, p


def select_blocks(p, q_pos, total_len):
    n_cmp = p.shape[-1]
    n_sel = -(-total_len // SEL_BLOCK)
    ci = jnp.arange(n_cmp)[:, None] * CMP_STRIDE
    sj = jnp.arange(n_sel)[None, :] * SEL_BLOCK
    cover = ((ci < sj + SEL_BLOCK) & (ci + CMP_BLOCK > sj)).astype(F32)
    imp = jnp.einsum('bqhgn,nj->bqhj', p, cover)
    jj = jnp.arange(n_sel)[None, :]
    cur = (q_pos // SEL_BLOCK)[:, None]
    forced = (jj == 0) | (jj == cur) | (jj == cur - 1)
    valid = jj * SEL_BLOCK <= q_pos[:, None]
    imp = jnp.where(forced[None, :, None, :], BIG, jnp.where(valid[None, :, None, :], imp, NEG))
    _, idx = lax.top_k(imp, min(SEL_TOP, n_sel))
    return idx


def block_tokens(idx):
    B, Tq, H, k = idx.shape
    return (idx[..., None] * SEL_BLOCK + jnp.arange(SEL_BLOCK)).reshape(B, Tq, H, k * SEL_BLOCK)


def slc_attend(q, q_pos, tok, kv_sel, slopes):
    dist = q_pos[None, :, None, None] - tok
    s = (jnp.einsum('bqhgd,bqhnd->bqhgn', q, kv_sel[..., 0, :]).astype(F32) * SCALE
         - slopes[None, None, :, :, None] * dist.astype(F32)[:, :, :, None, :])
    p = masked_softmax(s, (dist >= 0)[:, :, :, None, :])
    return jnp.einsum('bqhgn,bqhnd->bqhgd', p.astype(kv_sel.dtype), kv_sel[..., 1, :])


def win_attend(q, q_pos, kv, k_pos, slopes):
    dist = q_pos[:, None] - k_pos[None, :]
    mask = (dist >= 0) & (dist < WINDOW) & (k_pos[None, :] >= 0)
    s = (jnp.einsum('bqhgd,bkhd->bqhgk', q, kv[..., 0, :]).astype(F32) * SCALE
         - slopes[None, :, :, None] * dist.astype(F32)[:, None, None, :])
    p = masked_softmax(s, mask[:, None, None, :])
    return jnp.einsum('bqhgk,bkhd->bqhgd', p.astype(kv.dtype), kv[..., 1, :])


def gate_merge(gates, o_cmp, o_slc, o_win, dtype):
    o = (gates[:, :, 0, :, :, None] * o_cmp + gates[:, :, 1, :, :, None] * o_slc
         + gates[:, :, 2, :, :, None] * o_win)
    return o.reshape(o.shape[0], o.shape[1], NSA_WIDTH).astype(dtype)


def nsa_prompt(h, mem_kv, w_in, pe, w1, b1, w2, slopes):
    B, T = h.shape[:2]
    q, kv_cmp, kv_slc, kv_win, gates, qm = nsa_split(h @ w_in)
    q_pos = jnp.arange(T)
    ckv = cmp_tokens(cmp_subblock_proj(kv_cmp, w1), pe, w1, b1, w2)
    o_cmp, p = cmp_attend(q, q_pos, ckv, slopes)
    idx = select_blocks(p, q_pos, T)
    k_sel = idx.shape[-1]
    bi = jnp.arange(B)[:, None, None, None]
    hi = jnp.arange(NSA_KV_HEADS)[None, None, :, None]

    def slc_block(args):
        qb, ib, pb = args
        tok = block_tokens(ib)
        kv_sel = kv_slc[bi, jnp.minimum(tok, T - 1), hi]
        return slc_attend(qb, pb, tok, kv_sel, slopes)

    nb = T // SLC_Q_BLOCK
    o_slc = lax.map(slc_block, (
        jnp.moveaxis(q.reshape(B, nb, SLC_Q_BLOCK, NSA_KV_HEADS, NSA_GROUP, NSA_HEAD_DIM), 1, 0),
        jnp.moveaxis(idx.reshape(B, nb, SLC_Q_BLOCK, NSA_KV_HEADS, k_sel), 1, 0),
        q_pos.reshape(nb, SLC_Q_BLOCK)))
    o_slc = jnp.moveaxis(o_slc, 0, 1).reshape(q.shape)

    kv_pad = jnp.pad(kv_win, ((0, 0), (WINDOW, 0), (0, 0), (0, 0), (0, 0)))

    def win_block(n):
        start = n * WIN_Q_BLOCK
        qb = lax.dynamic_slice_in_dim(q, start, WIN_Q_BLOCK, axis=1)
        kb = lax.dynamic_slice_in_dim(kv_pad, start, WIN_Q_BLOCK + WINDOW, axis=1)
        qp = start + jnp.arange(WIN_Q_BLOCK)
        kp = start - WINDOW + jnp.arange(WIN_Q_BLOCK + WINDOW)
        return win_attend(qb, qp, kb, kp, slopes)

    o_win = jnp.moveaxis(lax.map(win_block, jnp.arange(T // WIN_Q_BLOCK)), 0, 1).reshape(q.shape)
    o = gate_merge(gates, o_cmp, o_slc, o_win, h.dtype)
    om = mem_attention(qm, mem_kv)
    new_win = kv_win[:, T - min(WINDOW, T):]
    return jnp.concatenate([o, om.astype(o.dtype)], axis=-1), (kv_cmp, kv_slc, new_win)


def nsa_sample(h, mem_kv, pool_cmp, pool_slc, win_buf, page_table, w_in, pe, w1, b1, w2, slopes):
    B, T = h.shape[:2]
    past_len = page_table.shape[1] * PAGE_SIZE
    total = past_len + T
    q, kv_cmp, kv_slc, kv_win, gates, qm = nsa_split(h @ w_in)
    q_pos = past_len + jnp.arange(T)
    past_cmp = pool_cmp[page_table].reshape(B, past_len, NSA_KV_HEADS, 2, NSA_HEAD_DIM)
    proj = cmp_subblock_proj(past_cmp, w1)
    n_new_rows = (T // CMP_STRIDE) * CMP_STRIDE
    if n_new_rows > 0:
        proj = jnp.concatenate([proj, cmp_subblock_proj(kv_cmp[:, :n_new_rows], w1)], axis=1)
    ckv = cmp_tokens(proj, pe, w1, b1, w2)
    o_cmp, p = cmp_attend(q, q_pos, ckv, slopes)
    tok = block_tokens(select_blocks(p, q_pos, total))
    bi = jnp.arange(B)[:, None, None, None]
    hi = jnp.arange(NSA_KV_HEADS)[None, None, :, None]
    tok_past = jnp.minimum(tok, past_len - 1)
    pages = page_table[bi, tok_past // PAGE_SIZE]
    kv_past = pool_slc[pages, tok_past % PAGE_SIZE, hi]
    kv_new = kv_slc[bi, jnp.clip(tok - past_len, 0, T - 1), hi]
    kv_sel = jnp.where((tok >= past_len)[..., None, None], kv_new, kv_past)
    o_slc = slc_attend(q, q_pos, tok, kv_sel, slopes)
    kv_w = jnp.concatenate([win_buf, kv_win], axis=1)
    k_pos = past_len - win_buf.shape[1] + jnp.arange(kv_w.shape[1])
    o_win = win_attend(q, q_pos, kv_w, k_pos, slopes)
    o = gate_merge(gates, o_cmp, o_slc, o_win, h.dtype)
    om = mem_attention(qm, mem_kv)
    new_win = kv_w[:, kv_w.shape[1] - min(WINDOW, total):]
    return jnp.concatenate([o, om.astype(o.dtype)], axis=-1), (kv_cmp, kv_slc, new_win)


def setup_inputs(seed: int = 0) -> dict:
    key = jax.random.key(seed)
    keys = iter(jax.random.split(key, 48))

    def nrm(shape, scale):
        return scale * jax.random.normal(next(keys), shape, jnp.float32)

    n_pages = PAST_LEN // PAGE_SIZE
    n_pool = (DEC_BATCH * n_pages * 5) // 4
    win_len = min(WINDOW, PAST_LEN)
    kv_row = (NSA_KV_HEADS, 2, NSA_HEAD_DIM)
    page_table = jax.random.permutation(next(keys), n_pool)[:DEC_BATCH * n_pages].reshape(DEC_BATCH, n_pages).astype(jnp.int32)
    ssm_a_im = jnp.pi * jnp.arange(SSM_STATE, dtype=jnp.float32) + nrm((N_SSM_LAYERS, SSM_GROUPS, SSM_STATE), 0.01)
    ssm_log_dt = jax.random.uniform(next(keys), (N_SSM_LAYERS, SSM_GROUPS), jnp.float32, math.log(DT_MIN), math.log(DT_MAX))
    x_prompt = nrm((BATCH, SEQ, D_MODEL), 1.0)
    x_sample = nrm((DEC_BATCH, DEC_SEQ, D_MODEL), 1.0)
    mem_prompt = nrm((BATCH, N_MEM, D_MODEL), 1.0)
    state_ssm_re = nrm((N_SSM_LAYERS, DEC_BATCH, SSM_GROUPS, SSM_STATE), 0.1)
    state_ssm_im = nrm((N_SSM_LAYERS, DEC_BATCH, SSM_GROUPS, SSM_STATE), 0.1)
    cache_cmp_kv = nrm((N_NSA_LAYERS, n_pool, PAGE_SIZE) + kv_row, 1.0)
    cache_slc_kv = nrm((N_NSA_LAYERS, n_pool, PAGE_SIZE) + kv_row, 1.0)
    cache_win_kv = nrm((N_NSA_LAYERS, DEC_BATCH, win_len) + kv_row, 1.0)
    cache_mem_kv = nrm((DEPTH, DEC_BATCH, N_MEM, MEM_HEADS, 2, MEM_HEAD_DIM), 1.0)
    return {
        'x_prompt': x_prompt,
        'x_sample': x_sample,
        'mem_prompt': mem_prompt,
        'state_ssm_re': state_ssm_re,
        'state_ssm_im': state_ssm_im,
        'cache_cmp_kv': cache_cmp_kv,
        'cache_slc_kv': cache_slc_kv,
        'cache_win_kv': cache_win_kv,
        'cache_mem_kv': cache_mem_kv,
        'page_table': page_table,
        'norm_mix_pre': 1.0 + nrm((DEPTH, D_MODEL), 0.05),
        'norm_mix_post': 1.0 + nrm((DEPTH, D_MODEL), 0.05),
        'norm_ffn_pre': 1.0 + nrm((DEPTH, D_MODEL), 0.05),
        'norm_ffn_post': 1.0 + nrm((DEPTH, D_MODEL), 0.05),
        'w_out': nrm((DEPTH, MIX_WIDTH, D_MODEL), MIX_WIDTH ** -0.5),
        'w_mem_kv': nrm((DEPTH, D_MODEL, MEM_HEADS * 2 * MEM_HEAD_DIM), D_MODEL ** -0.5),
        'w_ffn_in': nrm((DEPTH, D_MODEL, 2 * D_FF), D_MODEL ** -0.5),
        'w_ffn_out': nrm((DEPTH, D_FF, D_MODEL), D_FF ** -0.5),
        'ssm_w_in': nrm((N_SSM_LAYERS, D_MODEL, SSM_IN), D_MODEL ** -0.5),
        'ssm_a_re': -0.5 + nrm((N_SSM_LAYERS, SSM_GROUPS, SSM_STATE), 0.01),
        'ssm_a_im': ssm_a_im,
        'ssm_log_dt': ssm_log_dt,
        'ssm_b_re': nrm((N_SSM_LAYERS, SSM_GROUPS, SSM_STATE, SSM_GROUP), (2 * SSM_GROUP) ** -0.5),
        'ssm_b_im': nrm((N_SSM_LAYERS, SSM_GROUPS, SSM_STATE, SSM_GROUP), (2 * SSM_GROUP) ** -0.5),
        'ssm_c_re': nrm((N_SSM_LAYERS, SSM_GROUPS, SSM_GROUP, SSM_STATE), SSM_STATE ** -0.5),
        'ssm_c_im': nrm((N_SSM_LAYERS, SSM_GROUPS, SSM_GROUP, SSM_STATE), SSM_STATE ** -0.5),
        'ssm_d': nrm((N_SSM_LAYERS, SSM_WIDTH), 1.0),
        'ssm_w_glu': nrm((N_SSM_LAYERS, SSM_WIDTH, SSM_WIDTH), SSM_WIDTH ** -0.5),
        'ssm_b_glu': nrm((N_SSM_LAYERS, SSM_WIDTH), 0.01),
        'nsa_w_in': nrm((N_NSA_LAYERS, D_MODEL, NSA_IN), D_MODEL ** -0.5),
        'nsa_cmp_pe': nrm((N_NSA_LAYERS, CMP_BLOCK, 2, NSA_HEAD_DIM), 0.02),
        'nsa_cmp_w1': nrm((N_NSA_LAYERS, CMP_BLOCK, 2, NSA_HEAD_DIM, CMP_HIDDEN), (CMP_BLOCK * NSA_HEAD_DIM) ** -0.5),
        'nsa_cmp_b1': nrm((N_NSA_LAYERS, 2, CMP_HIDDEN), 0.01),
        'nsa_cmp_w2': nrm((N_NSA_LAYERS, 2, CMP_HIDDEN, NSA_HEAD_DIM), CMP_HIDDEN ** -0.5),
    }


def reference(x_prompt, x_sample, mem_prompt, state_ssm_re, state_ssm_im, cache_cmp_kv, cache_slc_kv,
              cache_win_kv, cache_mem_kv, page_table, norm_mix_pre, norm_mix_post, norm_ffn_pre, norm_ffn_post,
              w_out, w_mem_kv, w_ffn_in, w_ffn_out, ssm_w_in, ssm_a_re, ssm_a_im, ssm_log_dt, ssm_b_re, ssm_b_im,
              ssm_c_re, ssm_c_im, ssm_d, ssm_w_glu, ssm_b_glu, nsa_w_in, nsa_cmp_pe, nsa_cmp_w1, nsa_cmp_b1,
              nsa_cmp_w2):
    slopes = alibi_slopes()
    xp, xs = x_prompt, x_sample
    ssm_re_p, ssm_im_p, ssm_re_s, ssm_im_s = [], [], [], []
    cmp_p, slc_p, win_p, cmp_s, slc_s, win_s = [], [], [], [], [], []
    mem_p = []
    for i in range(DEPTH):
        j = i // N_MIXERS
        mkv_p = mem_kv_proj(mem_prompt, w_mem_kv[i])
        mem_p.append(mkv_p)
        mkv_s = cache_mem_kv[i]
        hp = rmsnorm(xp, norm_mix_pre[i])
        hs = rmsnorm(xs, norm_mix_pre[i])
        if i % N_MIXERS == 0:
            h0_p = jnp.zeros((xp.shape[0], SSM_GROUPS, SSM_STATE), jnp.complex64)
            h0_s = lax.complex(state_ssm_re[j].astype(F32), state_ssm_im[j].astype(F32))
            mp, hl_p = s5_mixer(hp, h0_p, mkv_p, ssm_w_in[j], ssm_a_re[j], ssm_a_im[j], ssm_log_dt[j], ssm_b_re[j],
                                ssm_b_im[j], ssm_c_re[j], ssm_c_im[j], ssm_d[j], ssm_w_glu[j], ssm_b_glu[j])
            ms, hl_s = s5_mixer(hs, h0_s, mkv_s, ssm_w_in[j], ssm_a_re[j], ssm_a_im[j], ssm_log_dt[j], ssm_b_re[j],
                                ssm_b_im[j], ssm_c_re[j], ssm_c_im[j], ssm_d[j], ssm_w_glu[j], ssm_b_glu[j])
            ssm_re_p.append(jnp.real(hl_p))
            ssm_im_p.append(jnp.imag(hl_p))
            ssm_re_s.append(jnp.real(hl_s))
            ssm_im_s.append(jnp.imag(hl_s))
        else:
            mp, (kc, ks, kw) = nsa_prompt(hp, mkv_p, nsa_w_in[j], nsa_cmp_pe[j], nsa_cmp_w1[j], nsa_cmp_b1[j],
                                          nsa_cmp_w2[j], slopes)
            ms, (sc, ss, sw) = nsa_sample(hs, mkv_s, cache_cmp_kv[j], cache_slc_kv[j], cache_win_kv[j], page_table,
                                          nsa_w_in[j], nsa_cmp_pe[j], nsa_cmp_w1[j], nsa_cmp_b1[j], nsa_cmp_w2[j],
                                          slopes)
            cmp_p.append(kc)
            slc_p.append(ks)
            win_p.append(kw)
            cmp_s.append(sc)
            slc_s.append(ss)
            win_s.append(sw)
        xp = xp + rmsnorm(mp @ w_out[i], norm_mix_post[i])
        xs = xs + rmsnorm(ms @ w_out[i], norm_mix_post[i])
        xp = xp + rmsnorm(swiglu_ffn(rmsnorm(xp, norm_ffn_pre[i]), w_ffn_in[i], w_ffn_out[i]), norm_ffn_post[i])
        xs = xs + rmsnorm(swiglu_ffn(rmsnorm(xs, norm_ffn_pre[i]), w_ffn_in[i], w_ffn_out[i]), norm_ffn_post[i])
    return (xp, xs, jnp.stack(ssm_re_p), jnp.stack(ssm_im_p), jnp.stack(ssm_re_s), jnp.stack(ssm_im_s),
            jnp.stack(cmp_p), jnp.stack(slc_p), jnp.stack(win_p), jnp.stack(cmp_s), jnp.stack(slc_s),
            jnp.stack(win_s), jnp.stack(mem_p))
```

```python
import functools
import math

import numpy as np
import jax
import jax.numpy as jnp
from jax import lax
from jax.experimental import pallas as pl
from jax.experimental.pallas import tpu as pltpu

F32 = jnp.float32
BF16 = jnp.bfloat16

D_MODEL = 1024
PAGE = 128
MEM_HEADS = 4
HEAD_DIM = 64
MEM_WIDTH = MEM_HEADS * HEAD_DIM
SSM_WIDTH = D_MODEL - MEM_WIDTH
SSM_GROUP = 16
SSM_GROUPS = SSM_WIDTH // SSM_GROUP
SSM_STATE = 64
NSA_HEADS = 12
KV_HEADS = 3
GROUP = NSA_HEADS // KV_HEADS
NSA_WIDTH = NSA_HEADS * HEAD_DIM
KV_WIDTH = KV_HEADS * 2 * HEAD_DIM
CMP_BLOCK = 32
CMP_STRIDE = 16
CMP_HIDDEN = 2 * HEAD_DIM
SEL_BLOCK = 64
SEL_TOP = 16
WINDOW = 512
RMS_EPS = 1e-6
NEG = -1e30
BIG = 1e30
SCALE = HEAD_DIM ** -0.5
LANES = 128
SUBLANES = 8
VMEM_LIMIT = 56 * 1024 * 1024
SLOPES = [2.0 ** (-8.0 * (h + 1) / NSA_HEADS) for h in range(NSA_HEADS)]


def _cparams(*sem):
    return pltpu.CompilerParams(dimension_semantics=sem, vmem_limit_bytes=VMEM_LIMIT)


def _rms(x, g):
    return x * lax.rsqrt(jnp.mean(x * x, axis=-1, keepdims=True) + RMS_EPS) * g


def _gelu(x):
    return 0.5 * x * (1.0 + jnp.tanh(math.sqrt(2.0 / math.pi) * (x + 0.044715 * (x * x * x))))


def _dot(a, b):
    return jnp.dot(a, b, preferred_element_type=F32)


def _dot_nt(a, b):
    return lax.dot_general(a, b, (((1,), (1,)), ((), ())), preferred_element_type=F32)


def _norm_proj_kernel(x_ref, g_ref, w_ref, *o_refs, splits, do_norm):
    x = x_ref[...]
    if do_norm:
        x = _rms(x, g_ref[...])
    z = _dot(x.astype(BF16), w_ref[...])
    for (start, width), o in zip(splits, o_refs):
        o[...] = z[:, start:start + width]


def _norm_proj(x, g, w, splits, do_norm=True, tm=256):
    rows, d = x.shape
    n = w.shape[1]
    tm = min(tm, rows)
    kern = functools.partial(_norm_proj_kernel, splits=tuple(splits), do_norm=do_norm)
    return pl.pallas_call(
        kern,
        grid=(rows // tm,),
        in_specs=[pl.BlockSpec((tm, d), lambda i: (i, 0)),
                  pl.BlockSpec((1, d), lambda i: (0, 0)),
                  pl.BlockSpec((d, n), lambda i: (0, 0))],
        out_specs=[pl.BlockSpec((tm, wd), lambda i: (i, 0)) for _, wd in splits],
        out_shape=[jax.ShapeDtypeStruct((rows, wd), F32) for _, wd in splits],
        compiler_params=_cparams("parallel"),
    )(x, g.reshape(1, d), w)


def _mem_attn_kernel(q_ref, kv_ref, o_ref):
    q = q_ref[0]
    kv = kv_ref[0]
    outs = []
    for h in range(MEM_HEADS):
        qh = (q[:, h * HEAD_DIM:(h + 1) * HEAD_DIM] * SCALE).astype(BF16)
        k = kv[:, h * 2 * HEAD_DIM:h * 2 * HEAD_DIM + HEAD_DIM].astype(BF16)
        v = kv[:, h * 2 * HEAD_DIM + HEAD_DIM:(h + 1) * 2 * HEAD_DIM].astype(BF16)
        s = _dot_nt(qh, k)
        p = jnp.exp(s - jnp.max(s, axis=-1, keepdims=True))
        l = jnp.sum(p, axis=-1, keepdims=True)
        outs.append(_dot(p.astype(BF16), v) / l)
    o_ref[0] = jnp.concatenate(outs, axis=-1)


def _mem_attn(qm, mem_kv, tm=256):
    b, t, _ = qm.shape
    tm = min(tm, t)
    n_mem = mem_kv.shape[1]
    return pl.pallas_call(
        _mem_attn_kernel,
        grid=(b, t // tm),
        in_specs=[pl.BlockSpec((1, tm, MEM_WIDTH), lambda i, j: (i, j, 0)),
                  pl.BlockSpec((1, n_mem, 2 * MEM_WIDTH), lambda i, j: (i, 0, 0))],
        out_specs=pl.BlockSpec((1, tm, MEM_WIDTH), lambda i, j: (i, j, 0)),
        out_shape=jax.ShapeDtypeStruct((b, t, MEM_WIDTH), F32),
        compiler_params=_cparams("parallel", "parallel"),
    )(qm, mem_kv)


def _post_kernel(x_ref, m1_ref, m2_ref, wo1_ref, wo2_ref, g1_ref, g2_ref, wg_ref, wu_ref, wd_ref, g3_ref,
                 o_ref):
    a = _dot(m1_ref[...].astype(BF16), wo1_ref[...]) + _dot(m2_ref[...].astype(BF16), wo2_ref[...])
    x1 = x_ref[...] + _rms(a, g1_ref[...])
    h = _rms(x1, g2_ref[...]).astype(BF16)
    gate = _dot(h, wg_ref[...])
    up = _dot(h, wu_ref[...])
    act = (gate * jax.nn.sigmoid(gate) * up).astype(BF16)
    f = _dot(act, wd_ref[...])
    o_ref[...] = x1 + _rms(f, g3_ref[...])


def _post(x, m1, m2, wo1, wo2, g1, g2, wg, wu, wd, g3, tm=256):
    rows, d = x.shape
    tm = min(tm, rows)
    w1, w2 = m1.shape[1], m2.shape[1]
    dff = wg.shape[1]

    def const(shape):
        return pl.BlockSpec(shape, lambda i: (0, 0), pipeline_mode=pl.Buffered(1))

    return pl.pallas_call(
        _post_kernel,
        grid=(rows // tm,),
        in_specs=[pl.BlockSpec((tm, d), lambda i: (i, 0)),
                  pl.BlockSpec((tm, w1), lambda i: (i, 0)),
                  pl.BlockSpec((tm, w2), lambda i: (i, 0)),
                  const((w1, d)), const((w2, d)), const((1, d)), const((1, d)),
                  const((d, dff)), const((d, dff)), const((dff, d)), const((1, d))],
        out_specs=pl.BlockSpec((tm, d), lambda i: (i, 0)),
        out_shape=jax.ShapeDtypeStruct((rows, d), F32),
        compiler_params=_cparams("parallel"),
    )(x, m1, m2, wo1, wo2, g1.reshape(1, d), g2.reshape(1, d), wg, wu, wd, g3.reshape(1, d))


def _s5_operators(a_re, a_im, log_dt, b_re, b_im, c_re, c_im, q):
    hp = lax.Precision.HIGHEST
    a_re, a_im = a_re.astype(F32), a_im.astype(F32)
    dt = jnp.exp(log_dt.astype(F32))[:, None]
    mag = jnp.exp(a_re * dt)
    ab_r, ab_i = mag * jnp.cos(a_im * dt), mag * jnp.sin(a_im * dt)
    den = a_re * a_re + a_im * a_im
    nr, ni = ab_r - 1.0, ab_i
    f_r = (nr * a_re + ni * a_im) / den
    f_i = (ni * a_re - nr * a_im) / den
    bb_r = f_r[..., None] * b_re - f_i[..., None] * b_im
    bb_i = f_r[..., None] * b_im + f_i[..., None] * b_re
    pw_r, pw_i = [jnp.ones_like(ab_r)], [jnp.zeros_like(ab_i)]
    for _ in range(q):
        r, i = pw_r[-1], pw_i[-1]
        pw_r.append(r * ab_r - i * ab_i)
        pw_i.append(r * ab_i + i * ab_r)
    pw_r, pw_i = jnp.stack(pw_r), jnp.stack(pw_i)
    w_r = pw_r[:, :, :, None] * bb_r[None] - pw_i[:, :, :, None] * bb_i[None]
    w_i = pw_r[:, :, :, None] * bb_i[None] + pw_i[:, :, :, None] * bb_r[None]
    kk = (jnp.einsum('gip,tgpj->tgij', c_re, w_r, precision=hp)
          - jnp.einsum('gip,tgpj->tgij', c_im, w_i, precision=hp))
    tau = jnp.arange(q)[None, :] - jnp.arange(q)[:, None]
    kt = kk[jnp.clip(tau, 0, q)]
    kt = jnp.where((tau >= 0)[:, :, None, None, None], kt, 0.0)
    g = a_re.shape[0]
    toep = kt.transpose(2, 0, 4, 1, 3).reshape(g, q * SSM_GROUP, q * SSM_GROUP)
    rev = q - 1 - jnp.arange(q)
    bst_r = w_r[rev].transpose(1, 0, 3, 2).reshape(g, q * SSM_GROUP, SSM_STATE)
    bst_i = w_i[rev].transpose(1, 0, 3, 2).reshape(g, q * SSM_GROUP, SSM_STATE)
    ar, ai = pw_r[1:], pw_i[1:]
    co_r = c_re[None] * ar[:, :, None, :] - c_im[None] * ai[:, :, None, :]
    co_i = -c_re[None] * ai[:, :, None, :] - c_im[None] * ar[:, :, None, :]
    cout_r = co_r.transpose(1, 3, 0, 2).reshape(g, SSM_STATE, q * SSM_GROUP)
    cout_i = co_i.transpose(1, 3, 0, 2).reshape(g, SSM_STATE, q * SSM_GROUP)
    return (toep.astype(BF16), bst_r.astype(BF16), bst_i.astype(BF16), cout_r.astype(BF16),
            cout_i.astype(BF16), pw_r[q][:, None, :], pw_i[q][:, None, :])


def _s5_scan_kernel(u_ref, toep_ref, bsr_ref, bsi_ref, cor_ref, coi_ref, aqr_ref, aqi_ref, h0r_ref, h0i_ref,
                    y_ref, hlr_ref, hli_ref, sr_ref, si_ref, hpr_ref, hpi_ref, *, gs, nc, nb):
    for g in range(gs):
        u = u_ref[g].astype(BF16)
        sr_ref[g] = _dot(u, bsr_ref[g])
        si_ref[g] = _dot(u, bsi_ref[g])

    def step(c, carry):
        rows = pl.ds(pl.multiple_of(c * nb, SUBLANES), nb)
        new = []
        for g in range(gs):
            hr, hi = carry[2 * g], carry[2 * g + 1]
            hpr_ref[g, rows, :] = hr
            hpi_ref[g, rows, :] = hi
            ar, ai = aqr_ref[g], aqi_ref[g]
            new.append(ar * hr - ai * hi + sr_ref[g, rows, :])
            new.append(ar * hi + ai * hr + si_ref[g, rows, :])
        return tuple(new)

    init = []
    for g in range(gs):
        init += [h0r_ref[g], h0i_ref[g]]
    fin = lax.fori_loop(0, nc, step, tuple(init))
    for g in range(gs):
        hlr_ref[g] = fin[2 * g]
        hli_ref[g] = fin[2 * g + 1]
        u = u_ref[g].astype(BF16)
        y_ref[g] = (_dot(u, toep_ref[g]) + _dot(hpr_ref[g].astype(BF16), cor_ref[g])
                    + _dot(hpi_ref[g].astype(BF16), coi_ref[g]))


def _s5_scan(u, h0r, h0i, ops, q, gs=4):
    b, l, _ = u.shape
    nc = l // q
    qw = q * SSM_GROUP
    toep, bsr, bsi, cor, coi, aqr, aqi = ops
    ug = u.reshape(b, nc, q, SSM_GROUPS, SSM_GROUP).transpose(3, 1, 0, 2, 4).reshape(SSM_GROUPS, nc * b, qw)
    h0r_g = h0r.transpose(1, 0, 2)
    h0i_g = h0i.transpose(1, 0, 2)
    rows = nc * b

    def gspec(r, c):
        return pl.BlockSpec((gs, r, c), lambda i: (i, 0, 0))

    kern = functools.partial(_s5_scan_kernel, gs=gs, nc=nc, nb=b)
    y, hlr, hli = pl.pallas_call(
        kern,
        grid=(SSM_GROUPS // gs,),
        in_specs=[gspec(rows, qw), gspec(qw, qw), gspec(qw, SSM_STATE), gspec(qw, SSM_STATE),
                  gspec(SSM_STATE, qw), gspec(SSM_STATE, qw), gspec(1, SSM_STATE), gspec(1, SSM_STATE),
                  gspec(b, SSM_STATE), gspec(b, SSM_STATE)],
        out_specs=[gspec(rows, qw), gspec(b, SSM_STATE), gspec(b, SSM_STATE)],
        out_shape=[jax.ShapeDtypeStruct((SSM_GROUPS, rows, qw), F32),
                   jax.ShapeDtypeStruct((SSM_GROUPS, b, SSM_STATE), F32),
                   jax.ShapeDtypeStruct((SSM_GROUPS, b, SSM_STATE), F32)],
        scratch_shapes=[pltpu.VMEM((gs, rows, SSM_STATE), F32) for _ in range(4)],
        compiler_params=_cparams("parallel"),
    )(ug, toep, bsr, bsi, cor, coi, aqr, aqi, h0r_g, h0i_g)
    y = y.reshape(SSM_GROUPS, nc, b, q, SSM_GROUP).transpose(2, 1, 3, 0, 4).reshape(b, l, SSM_WIDTH)
    return y, hlr.transpose(1, 0, 2), hli.transpose(1, 0, 2)


def _s5_glu_kernel(y_ref, u_ref, d_ref, w_ref, b_ref, o_ref):
    v = _gelu(y_ref[...] + d_ref[...] * u_ref[...])
    o_ref[...] = v * jax.nn.sigmoid(_dot(v.astype(BF16), w_ref[...]) + b_ref[...])


def _s5_glu(y, u, d, w, bias, tm=256):
    rows, n = y.shape
    tm = min(tm, rows)
    return pl.pallas_call(
        _s5_glu_kernel,
        grid=(rows // tm,),
        in_specs=[pl.BlockSpec((tm, n), lambda i: (i, 0)), pl.BlockSpec((tm, n), lambda i: (i, 0)),
                  pl.BlockSpec((1, n), lambda i: (0, 0)), pl.BlockSpec((n, n), lambda i: (0, 0)),
                  pl.BlockSpec((1, n), lambda i: (0, 0))],
        out_specs=pl.BlockSpec((tm, n), lambda i: (i, 0)),
        out_shape=jax.ShapeDtypeStruct((rows, n), F32),
        compiler_params=_cparams("parallel"),
    )(y, u, d.reshape(1, n), w, bias.reshape(1, n))


CMP_PAGES = 16
SUBS = PAGE // CMP_STRIDE


def _cmp_weights(pe, w1, b1, w2):
    w = w1.reshape(2, CMP_STRIDE // 2, 2, 2, HEAD_DIM, CMP_HIDDEN)
    eye = jnp.eye(2, dtype=w1.dtype)
    wp = jnp.einsum('apjcef,cd->pjcedaf', w, eye)
    wp = wp.reshape(CMP_STRIDE // 2, 2 * 2 * HEAD_DIM, 2 * 2 * CMP_HIDDEN)
    bias = (jnp.einsum('jce,jcef->cf', pe, w1, precision=lax.Precision.HIGHEST) + b1).reshape(1, 2 * CMP_HIDDEN)
    w2b = jnp.einsum('cfe,cd->cfde', w2, eye).reshape(2 * CMP_HIDDEN, 2 * HEAD_DIM)
    return wp.astype(BF16), bias.astype(F32), w2b.astype(BF16)


def _cmp_tokens_kernel(pt_ref, *refs):
    page_refs = refs[:CMP_PAGES * KV_HEADS]
    wp_ref, bias_ref, w2_ref, o_ref, carry_ref = refs[CMP_PAGES * KV_HEADS:]
    n = CMP_PAGES * SUBS

    @pl.when(pl.program_id(1) == 0)
    def _():
        carry_ref[...] = jnp.zeros_like(carry_ref)

    acc = None
    for jp in range(CMP_STRIDE // 2):
        halves = []
        for jj in range(2):
            j = 2 * jp + jj
            rows = []
            for h in range(KV_HEADS):
                for k in range(CMP_PAGES):
                    rows.append(page_refs[k * KV_HEADS + h][0, pl.ds(j, SUBS, stride=CMP_STRIDE), :])
            halves.append(jnp.concatenate(rows, axis=0))
        lhs = jnp.concatenate(halves, axis=1).astype(BF16)
        part = _dot(lhs, wp_ref[jp])
        acc = part if acc is None else acc + part
    row = lax.broadcasted_iota(jnp.int32, (n, 1), 0)
    outs = []
    for h in range(KV_HEADS):
        p = acc[h * n:(h + 1) * n]
        first = jnp.concatenate([p[:, 0:CMP_HIDDEN], p[:, 2 * CMP_HIDDEN:3 * CMP_HIDDEN]], axis=1)
        second = jnp.concatenate([p[:, CMP_HIDDEN:2 * CMP_HIDDEN], p[:, 3 * CMP_HIDDEN:]], axis=1)
        prev = jnp.where(row == 0, carry_ref[h:h + 1, :], pltpu.roll(first, 1, axis=0))
        carry_ref[h:h + 1, :] = first[n - 1:n, :]
        hid = _gelu(prev + second + bias_ref[...])
        outs.append(_dot(hid.astype(BF16), w2_ref[...]))
    o_ref[0] = jnp.concatenate(outs, axis=1)


def _cmp_tokens(pool, page_table, wp, bias, w2b):
    b, n_pages = page_table.shape
    steps = n_pages // CMP_PAGES
    n = CMP_PAGES * SUBS

    def page_spec(k, h):
        return pl.BlockSpec((1, PAGE, 2 * HEAD_DIM), lambda i, s, pt: (pt[i, s * CMP_PAGES + k], 0, h))

    grid_spec = pltpu.PrefetchScalarGridSpec(
        num_scalar_prefetch=1,
        grid=(b, steps),
        in_specs=[page_spec(k, h) for k in range(CMP_PAGES) for h in range(KV_HEADS)] + [
            pl.BlockSpec(wp.shape, lambda i, s, pt: (0, 0, 0)),
            pl.BlockSpec(bias.shape, lambda i, s, pt: (0, 0)),
            pl.BlockSpec(w2b.shape, lambda i, s, pt: (0, 0))],
        out_specs=pl.BlockSpec((1, n, KV_WIDTH), lambda i, s, pt: (i, s, 0)),
        scratch_shapes=[pltpu.VMEM((SUBLANES, 2 * CMP_HIDDEN), F32)],
    )
    return pl.pallas_call(
        _cmp_tokens_kernel,
        grid_spec=grid_spec,
        out_shape=jax.ShapeDtypeStruct((b, steps * n, KV_WIDTH), F32),
        compiler_params=_cparams("parallel", "arbitrary"),
    )(page_table, *([pool] * (CMP_PAGES * KV_HEADS)), wp, bias, w2b)


def _cover_matrix(n_tok, n_sel):
    i = np.arange(n_tok)[:, None]
    start = (i - 1) * CMP_STRIDE
    sj = np.arange(n_sel)[None, :] * SEL_BLOCK
    cov = (start < sj + SEL_BLOCK) & (start + CMP_BLOCK > sj) & (i >= 1)
    return cov.astype(np.float32)


def _split_hi_lo(x):
    hi = x.astype(BF16)
    lo = (x - hi.astype(F32)).astype(BF16)
    return hi, lo


def _nsa_prompt_kernel(q_ref, ckv_ref, kslc_ref, kwin_ref, gates_ref, cover_ref, o_ref, sel_ref, *, tq, tk):
    q0 = pl.program_id(1) * tq
    qt = (q_ref[0] * SCALE).T
    gt = jax.nn.sigmoid(gates_ref[0]).T
    qpos = q0 + lax.broadcasted_iota(jnp.int32, (1, tq), 1)
    lane = lax.broadcasted_iota(jnp.int32, (1, GROUP * tq), 1)
    n_cmp = ckv_ref.shape[1]
    n_sel = cover_ref.shape[0]
    out_rows = []
    for h in range(KV_HEADS):
        q4 = jnp.concatenate([qt[(h * GROUP + g) * HEAD_DIM:(h * GROUP + g + 1) * HEAD_DIM, :]
                              for g in range(GROUP)], axis=1).astype(BF16)
        slope4 = jnp.zeros((1, GROUP * tq), F32)
        for g in range(GROUP):
            slope4 = jnp.where((lane >= g * tq) & (lane < (g + 1) * tq), SLOPES[h * GROUP + g], slope4)

        def tile4(x):
            return jnp.concatenate([x] * GROUP, axis=1)

        ckv = ckv_ref[0, :, h * 2 * HEAD_DIM:(h + 1) * 2 * HEAD_DIM]
        ck = ckv[:, :HEAD_DIM].astype(BF16)
        cvt = ckv.T[HEAD_DIM:, :].astype(BF16)
        slot = lax.broadcasted_iota(jnp.int32, (n_cmp, 1), 0)
        dist = qpos - ((slot + 1) * CMP_STRIDE - 1)
        mask = tile4(((dist >= 0) & (slot >= 1)).astype(F32))
        s = _dot(ck, q4) - slope4 * tile4(dist.astype(F32))
        s = jnp.where(mask > 0, s, NEG)
        p = jnp.exp(s - jnp.max(s, axis=0, keepdims=True)) * mask
        p = p / jnp.maximum(jnp.sum(p, axis=0, keepdims=True), 1e-30)
        o_cmp = _dot(cvt, p.astype(BF16))
        psum = p[:, 0:tq]
        for g in range(1, GROUP):
            psum = psum + p[:, g * tq:(g + 1) * tq]
        p_hi, p_lo = _split_hi_lo(psum)
        cov = cover_ref[...].astype(BF16)
        imp = _dot(cov, p_hi) + _dot(cov, p_lo)
        jj = lax.broadcasted_iota(jnp.int32, (n_sel, 1), 0)
        cur = qpos // SEL_BLOCK
        forced = (jj == 0) | (jj == cur) | (jj == cur - 1)
        valid = jj * SEL_BLOCK <= qpos
        imp = jnp.where(forced, BIG, jnp.where(valid, imp, NEG))
        cnt = jnp.zeros((n_sel, tq), F32)
        for j2 in range(n_sel):
            r = imp[j2:j2 + 1, :]
            beats = jnp.where(r > imp, 1.0, jnp.where((r == imp) & (j2 < jj), 1.0, 0.0))
            cnt = cnt + beats
        sel = jnp.where(cnt < SEL_TOP, 1.0, 0.0)
        per_tile = tk // SEL_BLOCK
        for kk in range(n_sel // per_tile):
            sel_ref[kk] = sel[kk * per_tile:(kk + 1) * per_tile, :]

        def attend(kv_ref, lo, hi, mask_fn):
            def body(kt, carry):
                m, l, acc = carry
                k0 = pl.multiple_of(kt * tk, tk)
                kv = kv_ref[0, pl.ds(k0, tk), h * 2 * HEAD_DIM:(h + 1) * 2 * HEAD_DIM]
                k = kv[:, :HEAD_DIM].astype(BF16)
                vt = kv.T[HEAD_DIM:, :].astype(BF16)
                kpos = k0 + lax.broadcasted_iota(jnp.int32, (tk, 1), 0)
                d = qpos - kpos
                msk = tile4(mask_fn(kt, d).astype(F32))
                sc = _dot(k, q4) - slope4 * tile4(d.astype(F32))
                sc = jnp.where(msk > 0, sc, NEG)
                m_new = jnp.maximum(m, jnp.max(sc, axis=0, keepdims=True))
                alpha = jnp.exp(m - m_new)
                pp = jnp.exp(sc - m_new) * msk
                l_new = alpha * l + jnp.sum(pp, axis=0, keepdims=True)
                acc_new = alpha * acc + _dot(vt, pp.astype(BF16))
                return m_new, l_new, acc_new

            init = (jnp.full((1, GROUP * tq), NEG, F32), jnp.zeros((1, GROUP * tq), F32),
                    jnp.zeros((HEAD_DIM, GROUP * tq), F32))
            m, l, acc = lax.fori_loop(lo, hi, body, init)
            return acc / jnp.maximum(l, 1e-30)

        half = lax.broadcasted_iota(jnp.int32, (tk, 1), 0) // SEL_BLOCK

        def slc_mask(kt, d):
            rows = sel_ref[kt]
            sel = rows[0:1, :]
            for r in range(1, per_tile):
                sel = jnp.where(half == r, rows[r:r + 1, :], sel)
            return (d >= 0) & (sel > 0)

        def win_mask(kt, d):
            return (d >= 0) & (d < WINDOW)

        hi = (q0 + tq) // tk
        o_slc = attend(kslc_ref, 0, hi, slc_mask)
        o_win = attend(kwin_ref, jnp.maximum(q0 - WINDOW, 0) // tk, hi, win_mask)

        for g in range(GROUP):
            hh = h * GROUP + g
            sl = slice(g * tq, (g + 1) * tq)
            out_rows.append(gt[hh:hh + 1, :] * o_cmp[:, sl]
                            + gt[NSA_HEADS + hh:NSA_HEADS + hh + 1, :] * o_slc[:, sl]
                            + gt[2 * NSA_HEADS + hh:2 * NSA_HEADS + hh + 1, :] * o_win[:, sl])
    o_ref[0] = jnp.concatenate(out_rows, axis=0).T


def _nsa_prompt_attn(q, ckv, kv_slc, kv_win, gates, tq=128, tk=128):
    b, t, _ = q.shape
    n_cmp = ckv.shape[1]
    n_sel = t // SEL_BLOCK
    cover_t = jnp.asarray(_cover_matrix(n_cmp, n_sel).T)
    kern = functools.partial(_nsa_prompt_kernel, tq=tq, tk=tk)
    return pl.pallas_call(
        kern,
        grid=(b, t // tq),
        in_specs=[pl.BlockSpec((1, tq, NSA_WIDTH), lambda i, j: (i, j, 0)),
                  pl.BlockSpec((1, n_cmp, KV_WIDTH), lambda i, j: (i, 0, 0)),
                  pl.BlockSpec((1, t, KV_WIDTH), lambda i, j: (i, 0, 0)),
                  pl.BlockSpec((1, t, KV_WIDTH), lambda i, j: (i, 0, 0)),
                  pl.BlockSpec((1, tq, LANES), lambda i, j: (i, j, 0)),
                  pl.BlockSpec((n_sel, n_cmp), lambda i, j: (0, 0))],
        out_specs=pl.BlockSpec((1, tq, NSA_WIDTH), lambda i, j: (i, j, 0)),
        out_shape=jax.ShapeDtypeStruct((b, t, NSA_WIDTH), F32),
        scratch_shapes=[pltpu.VMEM((n_sel * SEL_BLOCK // tk, tk // SEL_BLOCK, tq), F32)],
        compiler_params=_cparams("parallel", "parallel"),
    )(q, ckv, kv_slc, kv_win, gates, cover_t)


TS = SUBLANES
SEL_LANES = 384


def _nsa_sample_select_kernel(q_ref, ckv_ref, cover_ref, ocmp_ref, idx_ref, *, nbatch, past_len, n_sel):
    n_cmp = ckv_ref.shape[1]
    rowi = lax.broadcasted_iota(jnp.int32, (GROUP * TS, 1), 0)
    tpos = past_len + rowi % TS
    slot = lax.broadcasted_iota(jnp.int32, (1, n_cmp), 1)
    dist = tpos - ((slot + 1) * CMP_STRIDE - 1)
    mask = ((dist >= 0) & (slot >= 1)).astype(F32)
    jj = lax.broadcasted_iota(jnp.int32, (1, SEL_LANES), 1)
    jjf = jj.astype(F32)
    qp8 = past_len + lax.broadcasted_iota(jnp.int32, (TS, 1), 0)
    cur = qp8 // SEL_BLOCK
    forced = (jj == 0) | (jj == cur) | (jj == cur - 1)
    valid = jj * SEL_BLOCK <= qp8
    lane = lax.broadcasted_iota(jnp.int32, (1, LANES), 1)
    cov = cover_ref[...].astype(BF16)
    for bb in range(nbatch):
        q = q_ref[bb] * SCALE
        idx_out = jnp.zeros((TS, LANES), jnp.int32)
        heads = []
        for h in range(KV_HEADS):
            qh = jnp.concatenate([q[:, (h * GROUP + g) * HEAD_DIM:(h * GROUP + g + 1) * HEAD_DIM]
                                  for g in range(GROUP)], axis=0).astype(BF16)
            slope = jnp.zeros((GROUP * TS, 1), F32)
            for g in range(GROUP):
                slope = jnp.where(rowi // TS == g, SLOPES[h * GROUP + g], slope)
            ckv = ckv_ref[bb, :, h * 2 * HEAD_DIM:(h + 1) * 2 * HEAD_DIM]
            s = _dot_nt(qh, ckv[:, :HEAD_DIM].astype(BF16)) - slope * dist.astype(F32)
            s = jnp.where(mask > 0, s, NEG)
            p = jnp.exp(s - jnp.max(s, axis=-1, keepdims=True)) * mask
            p = p / jnp.maximum(jnp.sum(p, axis=-1, keepdims=True), 1e-30)
            o = _dot(p.astype(BF16), ckv[:, HEAD_DIM:].astype(BF16))
            heads += [o[g * TS:(g + 1) * TS, :] for g in range(GROUP)]
            psum = p[0:TS]
            for g in range(1, GROUP):
                psum = psum + p[g * TS:(g + 1) * TS]
            p_hi, p_lo = _split_hi_lo(psum)
            imp = _dot(p_hi, cov) + _dot(p_lo, cov)
            imp = jnp.where(forced, BIG, jnp.where(valid, imp, NEG))
            imp = jnp.where(jj < n_sel, imp, -3e38)
            for k in range(SEL_TOP):
                best = jnp.max(imp, axis=-1, keepdims=True)
                pick = jnp.min(jnp.where(imp == best, jjf, float(SEL_LANES)), axis=-1, keepdims=True)
                idx_out = jnp.where(lane == h * SEL_TOP + k, pick.astype(jnp.int32), idx_out)
                imp = jnp.where(jjf == pick, -3e38, imp)
        ocmp_ref[bb] = jnp.concatenate(heads, axis=-1)
        idx_ref[bb] = idx_out


def _nsa_sample_select(q, ckv, past_len, nbatch=4):
    b = q.shape[0]
    n_cmp = ckv.shape[1]
    n_sel = -(-(past_len + 4) // SEL_BLOCK)
    cov = np.zeros((n_cmp, SEL_LANES), np.float32)
    cov[:, :n_sel] = _cover_matrix(n_cmp, n_sel)
    kern = functools.partial(_nsa_sample_select_kernel, nbatch=nbatch, past_len=past_len, n_sel=n_sel)
    return pl.pallas_call(
        kern,
        grid=(b // nbatch,),
        in_specs=[pl.BlockSpec((nbatch, TS, NSA_WIDTH), lambda i: (i, 0, 0)),
                  pl.BlockSpec((nbatch, n_cmp, KV_WIDTH), lambda i: (i, 0, 0)),
                  pl.BlockSpec((n_cmp, SEL_LANES), lambda i: (0, 0))],
        out_specs=[pl.BlockSpec((nbatch, TS, NSA_WIDTH), lambda i: (i, 0, 0)),
                   pl.BlockSpec((nbatch, TS, LANES), lambda i: (i, 0, 0))],
        out_shape=[jax.ShapeDtypeStruct((b, TS, NSA_WIDTH), F32),
                   jax.ShapeDtypeStruct((b, TS, LANES), jnp.int32)],
        compiler_params=_cparams("parallel"),
    )(q, ckv, jnp.asarray(cov))


def _nsa_sample_attend_kernel(idx_ref, pt_ref, q_ref, ocmp_ref, gates_ref, knew_ref, wbuf_ref, wnew_ref,
                              pool_ref, o_ref, buf_ref, sem, *, t_dec, past_len):
    b = pl.program_id(0)
    last_blk = past_len // SEL_BLOCK
    per_page = PAGE // SEL_BLOCK

    def block_copy(t, h, k):
        blk = jnp.minimum(idx_ref[((b * t_dec + t) * KV_HEADS + h) * SEL_TOP + k], last_blk - 1)
        page = pt_ref[b, blk // per_page]
        r0 = pl.multiple_of((blk % per_page) * SEL_BLOCK, SEL_BLOCK)
        return pltpu.make_async_copy(
            pool_ref.at[page, pl.ds(r0, SEL_BLOCK), pl.ds(h * 2 * HEAD_DIM, 2 * HEAD_DIM)],
            buf_ref.at[t, h, k], sem.at[0])

    for t in range(t_dec):
        for h in range(KV_HEADS):
            for k in range(SEL_TOP):
                block_copy(t, h, k).start()
    for t in range(t_dec):
        for h in range(KV_HEADS):
            for k in range(SEL_TOP):
                block_copy(t, h, k).wait()

    rowg = lax.broadcasted_iota(jnp.int32, (SUBLANES, 1), 0)
    slope_col = [jnp.zeros((SUBLANES, 1), F32) for _ in range(KV_HEADS)]
    for h in range(KV_HEADS):
        for g in range(GROUP):
            slope_col[h] = jnp.where(rowg == g, SLOPES[h * GROUP + g], slope_col[h])
    n_tok = SEL_TOP * SEL_BLOCK
    within = lax.broadcasted_iota(jnp.int32, (1, n_tok), 1) % SEL_BLOCK
    slot_of = lax.broadcasted_iota(jnp.int32, (1, n_tok), 1) // SEL_BLOCK
    pad_rows = jnp.zeros((SEL_BLOCK - TS, 2 * HEAD_DIM), F32)
    n_win = wbuf_ref.shape[1]
    wlane = lax.broadcasted_iota(jnp.int32, (1, n_win + TS), 1)
    wpos = jnp.where(wlane < n_win, past_len - n_win + wlane, past_len + wlane - n_win)

    for h in range(KV_HEADS):
        hs = slice(h * 2 * HEAD_DIM, (h + 1) * 2 * HEAD_DIM)
        new_blk = jnp.concatenate([knew_ref[0, :, hs], pad_rows], axis=0)
        kvw = jnp.concatenate([wbuf_ref[0, :, hs], wnew_ref[0, :, hs]], axis=0)
        kw = kvw[:, :HEAD_DIM].astype(BF16)
        vw = kvw[:, HEAD_DIM:].astype(BF16)
        for t in range(t_dec):
            qpos = past_len + t
            qh = (q_ref[0, t, h] * SCALE).astype(BF16)
            blocks = []
            tok = jnp.zeros((1, n_tok), jnp.int32)
            for k in range(SEL_TOP):
                blk = idx_ref[((b * t_dec + t) * KV_HEADS + h) * SEL_TOP + k]
                blocks.append(jnp.where(blk >= last_blk, new_blk, buf_ref[t, h, k]))
                tok = jnp.where(slot_of == k, blk * SEL_BLOCK + within, tok)
            kv = jnp.concatenate(blocks, axis=0)
            d = qpos - tok
            msk = (d >= 0).astype(F32)
            s = _dot_nt(qh, kv[:, :HEAD_DIM].astype(BF16)) - slope_col[h] * d.astype(F32)
            s = jnp.where(msk > 0, s, NEG)
            p = jnp.exp(s - jnp.max(s, axis=-1, keepdims=True)) * msk
            p = p / jnp.maximum(jnp.sum(p, axis=-1, keepdims=True), 1e-30)
            o_slc = _dot(p.astype(BF16), kv[:, HEAD_DIM:].astype(BF16))
            dw = qpos - wpos
            mw = ((dw >= 0) & (dw < WINDOW)).astype(F32)
            sw = _dot_nt(qh, kw) - slope_col[h] * dw.astype(F32)
            sw = jnp.where(mw > 0, sw, NEG)
            pw = jnp.exp(sw - jnp.max(sw, axis=-1, keepdims=True)) * mw
            pw = pw / jnp.maximum(jnp.sum(pw, axis=-1, keepdims=True), 1e-30)
            o_win = _dot(pw.astype(BF16), vw)
            gts = jax.nn.sigmoid(gates_ref[0, t, h])
            o_ref[0, t, h] = (gts[:, 0:1] * ocmp_ref[0, t, h] + gts[:, 1:2] * o_slc + gts[:, 2:3] * o_win)


def _nsa_sample_attend(idx, page_table, q5, ocmp5, gates5, kv_slc_new, win_buf, kv_win_new, pool_slc,
                       t_dec, past_len):
    b = q5.shape[0]
    n_win = win_buf.shape[1]
    blk5 = (1, t_dec, KV_HEADS, SUBLANES, HEAD_DIM)
    grid_spec = pltpu.PrefetchScalarGridSpec(
        num_scalar_prefetch=2,
        grid=(b,),
        in_specs=[pl.BlockSpec(blk5, lambda i, *_: (i, 0, 0, 0, 0)),
                  pl.BlockSpec(blk5, lambda i, *_: (i, 0, 0, 0, 0)),
                  pl.BlockSpec((1, t_dec, KV_HEADS, SUBLANES, LANES), lambda i, *_: (i, 0, 0, 0, 0)),
                  pl.BlockSpec((1, TS, KV_WIDTH), lambda i, *_: (i, 0, 0)),
                  pl.BlockSpec((1, n_win, KV_WIDTH), lambda i, *_: (i, 0, 0)),
                  pl.BlockSpec((1, TS, KV_WIDTH), lambda i, *_: (i, 0, 0)),
                  pl.BlockSpec(memory_space=pl.ANY)],
        out_specs=pl.BlockSpec(blk5, lambda i, *_: (i, 0, 0, 0, 0)),
        scratch_shapes=[pltpu.VMEM((t_dec, KV_HEADS, SEL_TOP, SEL_BLOCK, 2 * HEAD_DIM), F32),
                        pltpu.SemaphoreType.DMA((1,))],
    )
    kern = functools.partial(_nsa_sample_attend_kernel, t_dec=t_dec, past_len=past_len)
    return pl.pallas_call(
        kern,
        grid_spec=grid_spec,
        out_shape=jax.ShapeDtypeStruct((b, t_dec, KV_HEADS, SUBLANES, HEAD_DIM), F32),
        compiler_params=_cparams("arbitrary"),
    )(idx[:, :t_dec, :KV_HEADS * SEL_TOP].reshape(-1), page_table, q5, ocmp5, gates5, kv_slc_new, win_buf, kv_win_new, pool_slc)


NSA_SPLITS = ((0, NSA_WIDTH), (NSA_WIDTH, KV_WIDTH), (NSA_WIDTH + KV_WIDTH, KV_WIDTH),
              (NSA_WIDTH + 2 * KV_WIDTH, KV_WIDTH), (NSA_WIDTH + 3 * KV_WIDTH, MEM_WIDTH),
              (NSA_WIDTH + 3 * KV_WIDTH + MEM_WIDTH, LANES))


def _nsa_in_weight(w_in):
    o = NSA_WIDTH + 3 * KV_WIDTH
    n_gate = 3 * NSA_HEADS
    gates = jnp.pad(w_in[:, o:o + n_gate], ((0, 0), (0, LANES - n_gate)))
    return jnp.concatenate([w_in[:, :o], w_in[:, o + n_gate:], gates], axis=1).astype(BF16)


def _to_heads5(x, t_dec):
    b = x.shape[0]
    w = x.shape[-1] // NSA_HEADS
    x = x[:, :t_dec].reshape(b, t_dec, KV_HEADS, GROUP, w)
    return jnp.pad(x, ((0, 0), (0, 0), (0, 0), (0, SUBLANES - GROUP), (0, 0)))


def kernel(x_prompt, x_sample, mem_prompt, state_ssm_re, state_ssm_im, cache_cmp_kv, cache_slc_kv, cache_win_kv, cache_mem_kv, page_table, norm_mix_pre, norm_mix_post, norm_ffn_pre, norm_ffn_post, w_out, w_mem_kv, w_ffn_in, w_ffn_out, ssm_w_in, ssm_a_re, ssm_a_im, ssm_log_dt, ssm_b_re, ssm_b_im, ssm_c_re, ssm_c_im, ssm_d, ssm_w_glu, ssm_b_glu, nsa_w_in, nsa_cmp_pe, nsa_cmp_w1, nsa_cmp_b1, nsa_cmp_w2):
    bp, seq, d = x_prompt.shape
    bs, t_dec, _ = x_sample.shape
    n_mem = mem_prompt.shape[1]
    depth = w_out.shape[0]
    past_len = page_table.shape[1] * PAGE
    d_ff = w_ffn_out.shape[1]

    xp = x_prompt.reshape(bp * seq, d)
    xs = jnp.pad(x_sample, ((0, 0), (0, TS - t_dec), (0, 0))).reshape(bs * TS, d)
    mem2d = mem_prompt.reshape(bp * n_mem, d)

    outs = {k: [] for k in ('ssm_re_p', 'ssm_im_p', 'ssm_re_s', 'ssm_im_s', 'cmp_p', 'slc_p', 'win_p',
                            'cmp_s', 'slc_s', 'win_s', 'mem_p')}
    for i in range(depth):
        j = i // 2
        (mkv_p,) = _norm_proj(mem2d, norm_mix_pre[i], w_mem_kv[i].astype(BF16), ((0, 2 * MEM_WIDTH),),
                              do_norm=False)
        mkv_p = mkv_p.reshape(bp, n_mem, 2 * MEM_WIDTH)
        outs['mem_p'].append(mkv_p.reshape(bp, n_mem, MEM_HEADS, 2, HEAD_DIM))
        mkv_s = cache_mem_kv[i].reshape(bs, n_mem, 2 * MEM_WIDTH)
        if i % 2 == 0:
            w_in = ssm_w_in[j].astype(BF16)
            wglu = ssm_w_glu[j].astype(BF16)
            mix = []
            for x, b, t, q, mkv, h0r, h0i in (
                    (xp, bp, seq, 16, mkv_p, jnp.zeros((bp, SSM_GROUPS, SSM_STATE), F32),
                     jnp.zeros((bp, SSM_GROUPS, SSM_STATE), F32)),
                    (xs, bs, TS, t_dec, mkv_s, state_ssm_re[j].astype(F32), state_ssm_im[j].astype(F32))):
                u, qm = _norm_proj(x, norm_mix_pre[i], w_in, ((0, SSM_WIDTH), (SSM_WIDTH, MEM_WIDTH)))
                ops = _s5_operators(ssm_a_re[j], ssm_a_im[j], ssm_log_dt[j], ssm_b_re[j], ssm_b_im[j],
                                    ssm_c_re[j], ssm_c_im[j], q)
                u3 = u.reshape(b, t, SSM_WIDTH)
                t_real = (t // q) * q if t != TS else t_dec
                y, hlr, hli = _s5_scan(u3[:, :t_real], h0r, h0i, ops, q)
                if t_real != t:
                    y = jnp.pad(y, ((0, 0), (0, t - t_real), (0, 0)))
                yg = _s5_glu(y.reshape(b * t, SSM_WIDTH), u, ssm_d[j], wglu, ssm_b_glu[j])
                om = _mem_attn(qm.reshape(b, t, MEM_WIDTH), mkv).reshape(b * t, MEM_WIDTH)
                mix.append((yg, om))
                if b == bp and t == seq:
                    outs['ssm_re_p'].append(hlr)
                    outs['ssm_im_p'].append(hli)
                else:
                    outs['ssm_re_s'].append(hlr)
                    outs['ssm_im_s'].append(hli)
            (m1p, m2p), (m1s, m2s) = mix
        else:
            w_in = _nsa_in_weight(nsa_w_in[j])
            wp, cbias, w2b = _cmp_weights(nsa_cmp_pe[j], nsa_cmp_w1[j], nsa_cmp_b1[j], nsa_cmp_w2[j])
            q, kc, ks, kw, qm, gates = _norm_proj(xp, norm_mix_pre[i], w_in, NSA_SPLITS)
            kc3, ks3, kw3 = (a.reshape(bp, seq, KV_WIDTH) for a in (kc, ks, kw))
            ident = jnp.arange(bp * (seq // PAGE), dtype=jnp.int32).reshape(bp, seq // PAGE)
            ckv = _cmp_tokens(kc.reshape(bp * seq // PAGE, PAGE, KV_WIDTH), ident, wp, cbias, w2b)
            m1p = _nsa_prompt_attn(q.reshape(bp, seq, NSA_WIDTH), ckv, ks3, kw3,
                                   gates.reshape(bp, seq, LANES)).reshape(bp * seq, NSA_WIDTH)
            m2p = _mem_attn(qm.reshape(bp, seq, MEM_WIDTH), mkv_p).reshape(bp * seq, MEM_WIDTH)
            kv_shape = (KV_HEADS, 2, HEAD_DIM)
            outs['cmp_p'].append(kc3.reshape((bp, seq) + kv_shape))
            outs['slc_p'].append(ks3.reshape((bp, seq) + kv_shape))
            outs['win_p'].append(kw3[:, seq - min(WINDOW, seq):].reshape((bp, min(WINDOW, seq)) + kv_shape))
            q, kc, ks, kw, qm, gates = _norm_proj(xs, norm_mix_pre[i], w_in, NSA_SPLITS)
            kc3, ks3, kw3 = (a.reshape(bs, TS, KV_WIDTH) for a in (kc, ks, kw))
            pool_cmp = cache_cmp_kv[j].reshape(-1, PAGE, KV_WIDTH)
            pool_slc = cache_slc_kv[j].reshape(-1, PAGE, KV_WIDTH)
            win_buf = cache_win_kv[j].reshape(bs, -1, KV_WIDTH)
            ckv = _cmp_tokens(pool_cmp, page_table, wp, cbias, w2b)
            q3 = q.reshape(bs, TS, NSA_WIDTH)
            ocmp, idx = _nsa_sample_select(q3, ckv, past_len)
            gates3 = gates.reshape(bs, TS, LANES)[:, :, :3 * NSA_HEADS].reshape(bs, TS, 3, NSA_HEADS)
            gates5 = _to_heads5(gates3.transpose(0, 1, 3, 2).reshape(bs, TS, NSA_HEADS * 3), t_dec)
            gates5 = jnp.pad(gates5, ((0, 0),) * 4 + ((0, LANES - 3),))
            o5 = _nsa_sample_attend(idx, page_table, _to_heads5(q3, t_dec), _to_heads5(ocmp, t_dec), gates5,
                                    ks3, win_buf, kw3, pool_slc, t_dec, past_len)
            o = o5[:, :, :, :GROUP].reshape(bs, t_dec, NSA_WIDTH)
            m1s = jnp.pad(o, ((0, 0), (0, TS - t_dec), (0, 0))).reshape(bs * TS, NSA_WIDTH)
            m2s = _mem_attn(qm.reshape(bs, TS, MEM_WIDTH), mkv_s).reshape(bs * TS, MEM_WIDTH)
            outs['cmp_s'].append(kc3[:, :t_dec].reshape((bs, t_dec) + kv_shape))
            outs['slc_s'].append(ks3[:, :t_dec].reshape((bs, t_dec) + kv_shape))
            kv_w = jnp.concatenate([win_buf, kw3[:, :t_dec]], axis=1)
            n_keep = min(WINDOW, past_len + t_dec)
            outs['win_s'].append(kv_w[:, kv_w.shape[1] - n_keep:].reshape((bs, n_keep) + kv_shape))
        wo = w_out[i].astype(BF16)
        wg = w_ffn_in[i][:, :d_ff].astype(BF16)
        wu = w_ffn_in[i][:, d_ff:].astype(BF16)
        wd = w_ffn_out[i].astype(BF16)
        w1 = m1p.shape[1]
        args = (wo[:w1], wo[w1:], norm_mix_post[i], norm_ffn_pre[i], wg, wu, wd, norm_ffn_post[i])
        xp = _post(xp, m1p, m2p, *args)
        xs = _post(xs, m1s, m2s, *args)

    st = lambda k: jnp.stack(outs[k])
    y_sample = xs.reshape(bs, TS, d)[:, :t_dec]
    return (xp.reshape(bp, seq, d), y_sample, st('ssm_re_p'), st('ssm_im_p'), st('ssm_re_s'), st('ssm_im_s'),
            st('cmp_p'), st('slc_p'), st('win_p'), st('cmp_s'), st('slc_s'), st('win_s'), st('mem_p'))
```

```python
import functools
import math

import numpy as np
import jax
import jax.numpy as jnp
from jax import lax
from jax.experimental import pallas as pl
from jax.experimental.pallas import tpu as pltpu

F32 = jnp.float32
BF16 = jnp.bfloat16

D_MODEL = 1024
PAGE = 128
MEM_HEADS = 4
HEAD_DIM = 64
MEM_WIDTH = MEM_HEADS * HEAD_DIM
SSM_WIDTH = D_MODEL - MEM_WIDTH
SSM_GROUP = 16
SSM_GROUPS = SSM_WIDTH // SSM_GROUP
SSM_STATE = 64
NSA_HEADS = 12
KV_HEADS = 3
GROUP = NSA_HEADS // KV_HEADS
NSA_WIDTH = NSA_HEADS * HEAD_DIM
KV_WIDTH = KV_HEADS * 2 * HEAD_DIM
CMP_BLOCK = 32
CMP_STRIDE = 16
CMP_HIDDEN = 2 * HEAD_DIM
SEL_BLOCK = 64
SEL_TOP = 16
WINDOW = 512
RMS_EPS = 1e-6
NEG = -1e30
BIG = 1e30
SCALE = HEAD_DIM ** -0.5
LANES = 128
SUBLANES = 8
VMEM_LIMIT = 56 * 1024 * 1024
SLOPES = [2.0 ** (-8.0 * (h + 1) / NSA_HEADS) for h in range(NSA_HEADS)]


def _cparams(*sem):
    return pltpu.CompilerParams(dimension_semantics=sem, vmem_limit_bytes=VMEM_LIMIT)


def _rms(x, g):
    return x * lax.rsqrt(jnp.mean(x * x, axis=-1, keepdims=True) + RMS_EPS) * g


def _gelu(x):
    return 0.5 * x * (1.0 + jnp.tanh(math.sqrt(2.0 / math.pi) * (x + 0.044715 * (x * x * x))))


def _dot(a, b):
    return jnp.dot(a, b, preferred_element_type=F32)


def _dot_nt(a, b):
    return lax.dot_general(a, b, (((1,), (1,)), ((), ())), preferred_element_type=F32)


def _kv_cache_t(cache):
    n = cache.ndim
    return jnp.moveaxis(cache, n - 4, n - 1)


def _norm_proj_kernel(x_ref, g_ref, w_ref, *o_refs, splits, do_norm):
    x = x_ref[...]
    if do_norm:
        x = _rms(x, g_ref[...])
    z = _dot(x.astype(BF16), w_ref[...])
    for (start, width), o in zip(splits, o_refs):
        o[...] = z[:, start:start + width]


def _norm_proj(x, g, w, splits, do_norm=True, tm=256):
    rows, d = x.shape
    n = w.shape[1]
    tm = min(tm, rows)
    kern = functools.partial(_norm_proj_kernel, splits=tuple(splits), do_norm=do_norm)
    return pl.pallas_call(
        kern,
        grid=(rows // tm,),
        in_specs=[pl.BlockSpec((tm, d), lambda i: (i, 0)),
                  pl.BlockSpec((1, d), lambda i: (0, 0)),
                  pl.BlockSpec((d, n), lambda i: (0, 0))],
        out_specs=[pl.BlockSpec((tm, wd), lambda i: (i, 0)) for _, wd in splits],
        out_shape=[jax.ShapeDtypeStruct((rows, wd), F32) for _, wd in splits],
        compiler_params=_cparams("parallel"),
        name="norm_proj",
    )(x, g.reshape(1, d), w)


def _softmax_rows(s):
    p = jnp.exp(s - jnp.max(s, axis=-1, keepdims=True))
    return p, jnp.sum(p, axis=-1, keepdims=True)


def _mem_attn_kernel(q_ref, kv_ref, o_ref):
    q = q_ref[0]
    kv = kv_ref[0]
    outs = []
    for h in range(MEM_HEADS):
        qh = (q[:, h * HEAD_DIM:(h + 1) * HEAD_DIM] * SCALE).astype(BF16)
        k = kv[:, h * 2 * HEAD_DIM:h * 2 * HEAD_DIM + HEAD_DIM].astype(BF16)
        v = kv[:, h * 2 * HEAD_DIM + HEAD_DIM:(h + 1) * 2 * HEAD_DIM].astype(BF16)
        p, l = _softmax_rows(_dot_nt(qh, k))
        outs.append(_dot(p.astype(BF16), v) / l)
    o_ref[0] = jnp.concatenate(outs, axis=-1)


def _mem_attn(qm, mem_kv, tm=256):
    b, t, _ = qm.shape
    tm = min(tm, t)
    n_mem = mem_kv.shape[1]
    return pl.pallas_call(
        _mem_attn_kernel,
        grid=(b, t // tm),
        in_specs=[pl.BlockSpec((1, tm, MEM_WIDTH), lambda i, j: (i, j, 0)),
                  pl.BlockSpec((1, n_mem, 2 * MEM_WIDTH), lambda i, j: (i, 0, 0))],
        out_specs=pl.BlockSpec((1, tm, MEM_WIDTH), lambda i, j: (i, j, 0)),
        out_shape=jax.ShapeDtypeStruct((b, t, MEM_WIDTH), F32),
        compiler_params=_cparams("parallel", "parallel"),
        name="mem_attn",
    )(qm, mem_kv)


def _mem_attn_t_kernel(q_ref, kv_ref, o_ref):
    q = q_ref[0]
    outs = []
    for h in range(MEM_HEADS):
        qh = (q[:, h * HEAD_DIM:(h + 1) * HEAD_DIM] * SCALE).astype(BF16)
        p, l = _softmax_rows(_dot(qh, kv_ref[0, h, 0].astype(BF16)))
        outs.append(_dot_nt(p.astype(BF16), kv_ref[0, h, 1].astype(BF16)) / l)
    o_ref[0] = jnp.concatenate(outs, axis=-1)


def _mem_attn_t(qm, mem_kv_t):
    b, t, _ = qm.shape
    n_mem = mem_kv_t.shape[-1]
    return pl.pallas_call(
        _mem_attn_t_kernel,
        grid=(b,),
        in_specs=[pl.BlockSpec((1, t, MEM_WIDTH), lambda i: (i, 0, 0)),
                  pl.BlockSpec((1, MEM_HEADS, 2, HEAD_DIM, n_mem), lambda i: (i, 0, 0, 0, 0))],
        out_specs=pl.BlockSpec((1, t, MEM_WIDTH), lambda i: (i, 0, 0)),
        out_shape=jax.ShapeDtypeStruct((b, t, MEM_WIDTH), F32),
        compiler_params=_cparams("parallel"),
        name="mem_attn_t",
    )(qm, mem_kv_t)


def _post_kernel(x_ref, m1_ref, m2_ref, wo1_ref, wo2_ref, g1_ref, g2_ref, wg_ref, wu_ref, wd_ref, g3_ref,
                 o_ref):
    a = _dot(m1_ref[...].astype(BF16), wo1_ref[...]) + _dot(m2_ref[...].astype(BF16), wo2_ref[...])
    x1 = x_ref[...] + _rms(a, g1_ref[...])
    h = _rms(x1, g2_ref[...]).astype(BF16)
    gate = _dot(h, wg_ref[...])
    up = _dot(h, wu_ref[...])
    act = (gate * jax.nn.sigmoid(gate) * up).astype(BF16)
    f = _dot(act, wd_ref[...])
    o_ref[...] = x1 + _rms(f, g3_ref[...])


def _post(x, m1, m2, wo1, wo2, g1, g2, wg, wu, wd, g3, tm=256):
    rows, d = x.shape
    tm = min(tm, rows)
    w1, w2 = m1.shape[1], m2.shape[1]
    dff = wg.shape[1]

    def const(shape):
        return pl.BlockSpec(shape, lambda i: (0, 0), pipeline_mode=pl.Buffered(1))

    return pl.pallas_call(
        _post_kernel,
        grid=(rows // tm,),
        in_specs=[pl.BlockSpec((tm, d), lambda i: (i, 0)),
                  pl.BlockSpec((tm, w1), lambda i: (i, 0)),
                  pl.BlockSpec((tm, w2), lambda i: (i, 0)),
                  const((w1, d)), const((w2, d)), const((1, d)), const((1, d)),
                  const((d, dff)), const((d, dff)), const((dff, d)), const((1, d))],
        out_specs=pl.BlockSpec((tm, d), lambda i: (i, 0)),
        out_shape=jax.ShapeDtypeStruct((rows, d), F32),
        compiler_params=_cparams("parallel"),
        name="post_ffn",
    )(x, m1, m2, wo1, wo2, g1.reshape(1, d), g2.reshape(1, d), wg, wu, wd, g3.reshape(1, d))


def _s5_operators(a_re, a_im, log_dt, b_re, b_im, c_re, c_im, q):
    hp = lax.Precision.HIGHEST
    a_re, a_im = a_re.astype(F32), a_im.astype(F32)
    dt = jnp.exp(log_dt.astype(F32))[:, None]
    mag = jnp.exp(a_re * dt)
    ab_r, ab_i = mag * jnp.cos(a_im * dt), mag * jnp.sin(a_im * dt)
    den = a_re * a_re + a_im * a_im
    nr, ni = ab_r - 1.0, ab_i
    f_r = (nr * a_re + ni * a_im) / den
    f_i = (ni * a_re - nr * a_im) / den
    bb_r = f_r[..., None] * b_re - f_i[..., None] * b_im
    bb_i = f_r[..., None] * b_im + f_i[..., None] * b_re
    pw_r, pw_i = [jnp.ones_like(ab_r)], [jnp.zeros_like(ab_i)]
    for _ in range(q):
        r, i = pw_r[-1], pw_i[-1]
        pw_r.append(r * ab_r - i * ab_i)
        pw_i.append(r * ab_i + i * ab_r)
    pw_r, pw_i = jnp.stack(pw_r), jnp.stack(pw_i)
    w_r = pw_r[:, :, :, None] * bb_r[None] - pw_i[:, :, :, None] * bb_i[None]
    w_i = pw_r[:, :, :, None] * bb_i[None] + pw_i[:, :, :, None] * bb_r[None]
    kk = (jnp.einsum('gip,tgpj->tgij', c_re, w_r, precision=hp)
          - jnp.einsum('gip,tgpj->tgij', c_im, w_i, precision=hp))
    tau = jnp.arange(q)[None, :] - jnp.arange(q)[:, None]
    kt = kk[jnp.clip(tau, 0, q)]
    kt = jnp.where((tau >= 0)[:, :, None, None, None], kt, 0.0)
    g = a_re.shape[0]
    toep = kt.transpose(2, 0, 4, 1, 3).reshape(g, q * SSM_GROUP, q * SSM_GROUP)
    rev = q - 1 - jnp.arange(q)
    bst_r = w_r[rev].transpose(1, 0, 3, 2).reshape(g, q * SSM_GROUP, SSM_STATE)
    bst_i = w_i[rev].transpose(1, 0, 3, 2).reshape(g, q * SSM_GROUP, SSM_STATE)
    ar, ai = pw_r[1:], pw_i[1:]
    co_r = c_re[None] * ar[:, :, None, :] - c_im[None] * ai[:, :, None, :]
    co_i = -c_re[None] * ai[:, :, None, :] - c_im[None] * ar[:, :, None, :]
    cout_r = co_r.transpose(1, 3, 0, 2).reshape(g, SSM_STATE, q * SSM_GROUP)
    cout_i = co_i.transpose(1, 3, 0, 2).reshape(g, SSM_STATE, q * SSM_GROUP)
    return (toep.astype(BF16), bst_r.astype(BF16), bst_i.astype(BF16), cout_r.astype(BF16),
            cout_i.astype(BF16), pw_r[q][:, None, :], pw_i[q][:, None, :])


def _s5_scan_kernel(u_ref, toep_ref, bsr_ref, bsi_ref, cor_ref, coi_ref, aqr_ref, aqi_ref, h0r_ref, h0i_ref,
                    y_ref, hlr_ref, hli_ref, sr_ref, si_ref, hpr_ref, hpi_ref, *, gs, nc, nb):
    for g in range(gs):
        u = u_ref[g].astype(BF16)
        sr_ref[g] = _dot(u, bsr_ref[g])
        si_ref[g] = _dot(u, bsi_ref[g])

    def step(c, carry):
        rows = pl.ds(pl.multiple_of(c * nb, SUBLANES), nb)
        new = []
        for g in range(gs):
            hr, hi = carry[2 * g], carry[2 * g + 1]
            hpr_ref[g, rows, :] = hr
            hpi_ref[g, rows, :] = hi
            ar, ai = aqr_ref[g], aqi_ref[g]
            new.append(ar * hr - ai * hi + sr_ref[g, rows, :])
            new.append(ar * hi + ai * hr + si_ref[g, rows, :])
        return tuple(new)

    init = []
    for g in range(gs):
        init += [h0r_ref[g], h0i_ref[g]]
    fin = lax.fori_loop(0, nc, step, tuple(init))
    for g in range(gs):
        hlr_ref[g] = fin[2 * g]
        hli_ref[g] = fin[2 * g + 1]
        u = u_ref[g].astype(BF16)
        y_ref[g] = (_dot(u, toep_ref[g]) + _dot(hpr_ref[g].astype(BF16), cor_ref[g])
                    + _dot(hpi_ref[g].astype(BF16), coi_ref[g]))


def _s5_scan(u, h0r, h0i, ops, q, gs=4):
    b, l, _ = u.shape
    nc = l // q
    qw = q * SSM_GROUP
    toep, bsr, bsi, cor, coi, aqr, aqi = ops
    ug = u.reshape(b, nc, q, SSM_GROUPS, SSM_GROUP).transpose(3, 1, 0, 2, 4).reshape(SSM_GROUPS, nc * b, qw)
    h0r_g = h0r.transpose(1, 0, 2)
    h0i_g = h0i.transpose(1, 0, 2)
    rows = nc * b

    def gspec(r, c):
        return pl.BlockSpec((gs, r, c), lambda i: (i, 0, 0))

    kern = functools.partial(_s5_scan_kernel, gs=gs, nc=nc, nb=b)
    y, hlr, hli = pl.pallas_call(
        kern,
        grid=(SSM_GROUPS // gs,),
        in_specs=[gspec(rows, qw), gspec(qw, qw), gspec(qw, SSM_STATE), gspec(qw, SSM_STATE),
                  gspec(SSM_STATE, qw), gspec(SSM_STATE, qw), gspec(1, SSM_STATE), gspec(1, SSM_STATE),
                  gspec(b, SSM_STATE), gspec(b, SSM_STATE)],
        out_specs=[gspec(rows, qw), gspec(b, SSM_STATE), gspec(b, SSM_STATE)],
        out_shape=[jax.ShapeDtypeStruct((SSM_GROUPS, rows, qw), F32),
                   jax.ShapeDtypeStruct((SSM_GROUPS, b, SSM_STATE), F32),
                   jax.ShapeDtypeStruct((SSM_GROUPS, b, SSM_STATE), F32)],
        scratch_shapes=[pltpu.VMEM((gs, rows, SSM_STATE), F32) for _ in range(4)],
        compiler_params=_cparams("parallel"),
        name="s5_scan_groups",
    )(ug, toep, bsr, bsi, cor, coi, aqr, aqi, h0r_g, h0i_g)
    y = y.reshape(SSM_GROUPS, nc, b, q, SSM_GROUP).transpose(2, 1, 3, 0, 4).reshape(b, l, SSM_WIDTH)
    return y, hlr.transpose(1, 0, 2), hli.transpose(1, 0, 2)


OCT = LANES // SSM_GROUP
N_OCT = SSM_GROUPS // OCT
OCT_STATE = OCT * SSM_STATE


def _s5_octet_operators(ops, q):
    toep, bsr, bsi, cor, coi, aqr, aqi = ops
    eye = jnp.eye(OCT, dtype=toep.dtype)
    qq = q * LANES
    t8 = jnp.einsum('ogsjti,gh->osgjthi', toep.reshape(N_OCT, OCT, q, SSM_GROUP, q, SSM_GROUP), eye)
    t8 = t8.reshape(N_OCT, qq, qq)

    def bst8(b):
        b8 = jnp.einsum('ogsjp,gh->osgjhp', b.reshape(N_OCT, OCT, q, SSM_GROUP, SSM_STATE), eye)
        return b8.reshape(N_OCT, qq, OCT_STATE)

    def cout8(c):
        c8 = jnp.einsum('ogpti,gh->ogpthi', c.reshape(N_OCT, OCT, SSM_STATE, q, SSM_GROUP), eye)
        return c8.reshape(N_OCT, OCT_STATE, qq)

    return (t8, bst8(bsr), bst8(bsi), cout8(cor), cout8(coi),
            aqr.reshape(N_OCT, 1, OCT_STATE), aqi.reshape(N_OCT, 1, OCT_STATE))


def _s5_octet_kernel(x_ref, toep_ref, bsr_ref, bsi_ref, cor_ref, coi_ref, aqr_ref, aqi_ref, h0r_ref, h0i_ref,
                     y_ref, hlr_ref, hli_ref, sr_ref, si_ref, hpr_ref, hpi_ref, hr_ref, hi_ref, *, ncb, nb):
    @pl.when(pl.program_id(1) == 0)
    def _():
        hr_ref[...] = h0r_ref[0]
        hi_ref[...] = h0i_ref[0]

    x = x_ref[0]
    sr_ref[...] = _dot(x, bsr_ref[0])
    si_ref[...] = _dot(x, bsi_ref[0])
    ar, ai = aqr_ref[0], aqi_ref[0]

    def step(c, carry):
        hr, hi = carry
        rows = pl.ds(pl.multiple_of(c * nb, SUBLANES), nb)
        hpr_ref[rows, :] = hr
        hpi_ref[rows, :] = hi
        return ar * hr - ai * hi + sr_ref[rows, :], ar * hi + ai * hr + si_ref[rows, :]

    hr, hi = lax.fori_loop(0, ncb, step, (hr_ref[...], hi_ref[...]))
    hr_ref[...] = hr
    hi_ref[...] = hi
    hlr_ref[0] = hr
    hli_ref[0] = hi
    y_ref[0] = (_dot(x, toep_ref[0]) + _dot(hpr_ref[...].astype(BF16), cor_ref[0])
                + _dot(hpi_ref[...].astype(BF16), coi_ref[0]))


def _s5_scan_octets(u, h0r, h0i, ops, q, row_blocks=4):
    b, l, _ = u.shape
    nc = l // q
    qq = q * LANES
    rows = nc * b
    rb = rows // row_blocks
    ncb = nc // row_blocks
    ops8 = _s5_octet_operators(ops, q)
    x = u.reshape(b, nc, q, N_OCT, LANES).transpose(3, 1, 0, 2, 4).reshape(N_OCT, rows, qq).astype(BF16)
    h0r8 = h0r.reshape(b, N_OCT, OCT_STATE).transpose(1, 0, 2)
    h0i8 = h0i.reshape(b, N_OCT, OCT_STATE).transpose(1, 0, 2)

    def wspec(r, c):
        return pl.BlockSpec((1, r, c), lambda o, i: (o, 0, 0))

    kern = functools.partial(_s5_octet_kernel, ncb=ncb, nb=b)
    y, hlr, hli = pl.pallas_call(
        kern,
        grid=(N_OCT, row_blocks),
        in_specs=[pl.BlockSpec((1, rb, qq), lambda o, i: (o, i, 0)),
                  wspec(qq, qq), wspec(qq, OCT_STATE), wspec(qq, OCT_STATE), wspec(OCT_STATE, qq),
                  wspec(OCT_STATE, qq), wspec(1, OCT_STATE), wspec(1, OCT_STATE),
                  wspec(b, OCT_STATE), wspec(b, OCT_STATE)],
        out_specs=[pl.BlockSpec((1, rb, qq), lambda o, i: (o, i, 0)), wspec(b, OCT_STATE), wspec(b, OCT_STATE)],
        out_shape=[jax.ShapeDtypeStruct((N_OCT, rows, qq), F32),
                   jax.ShapeDtypeStruct((N_OCT, b, OCT_STATE), F32),
                   jax.ShapeDtypeStruct((N_OCT, b, OCT_STATE), F32)],
        scratch_shapes=[pltpu.VMEM((rb, OCT_STATE), F32) for _ in range(4)]
        + [pltpu.VMEM((b, OCT_STATE), F32) for _ in range(2)],
        compiler_params=_cparams("parallel", "arbitrary"),
        name="s5_scan_octets",
    )(x, *ops8, h0r8, h0i8)
    y = y.reshape(N_OCT, nc, b, q, LANES).transpose(2, 1, 3, 0, 4).reshape(b, l, SSM_WIDTH)
    unpack = lambda h: h.transpose(1, 0, 2).reshape(b, SSM_GROUPS, SSM_STATE)
    return y, unpack(hlr), unpack(hli)


def _s5_glu_kernel(y_ref, u_ref, d_ref, w_ref, b_ref, o_ref):
    v = _gelu(y_ref[...] + d_ref[...] * u_ref[...])
    o_ref[...] = v * jax.nn.sigmoid(_dot(v.astype(BF16), w_ref[...]) + b_ref[...])


def _s5_glu(y, u, d, w, bias, tm=256):
    rows, n = y.shape
    tm = min(tm, rows)
    return pl.pallas_call(
        _s5_glu_kernel,
        grid=(rows // tm,),
        in_specs=[pl.BlockSpec((tm, n), lambda i: (i, 0)), pl.BlockSpec((tm, n), lambda i: (i, 0)),
                  pl.BlockSpec((1, n), lambda i: (0, 0)), pl.BlockSpec((n, n), lambda i: (0, 0)),
                  pl.BlockSpec((1, n), lambda i: (0, 0))],
        out_specs=pl.BlockSpec((tm, n), lambda i: (i, 0)),
        out_shape=jax.ShapeDtypeStruct((rows, n), F32),
        compiler_params=_cparams("parallel"),
        name="s5_glu",
    )(y, u, d.reshape(1, n), w, bias.reshape(1, n))


CMP_PAGES = 16
SUBS = PAGE // CMP_STRIDE


def _cmp_weights(pe, w1, b1, w2):
    w = w1.reshape(2, CMP_STRIDE // 2, 2, 2, HEAD_DIM, CMP_HIDDEN)
    eye = jnp.eye(2, dtype=w1.dtype)
    wp = jnp.einsum('apjcef,cd->pjcedaf', w, eye)
    wp = wp.reshape(CMP_STRIDE // 2, 2 * 2 * HEAD_DIM, 2 * 2 * CMP_HIDDEN)
    bias = (jnp.einsum('jce,jcef->cf', pe, w1, precision=lax.Precision.HIGHEST) + b1).reshape(1, 2 * CMP_HIDDEN)
    w2b = jnp.einsum('cfe,cd->cfde', w2, eye).reshape(2 * CMP_HIDDEN, 2 * HEAD_DIM)
    return wp.astype(BF16), bias.astype(F32), w2b.astype(BF16)


def _cmp_tokens_kernel(pt_ref, *refs, transposed):
    nblk = CMP_PAGES * KV_HEADS
    page_refs = refs[:nblk]
    n = CMP_PAGES * SUBS
    if transposed:
        wp_ref, bias_ref, w2_ref, o_ref, carry_ref, xs_ref = refs[nblk:]
        for i in range(nblk):
            xs_ref[i] = page_refs[i][0, 0].T

        def rows_of(i, j):
            return xs_ref[i, pl.ds(j, SUBS, stride=CMP_STRIDE), :]
    else:
        wp_ref, bias_ref, w2_ref, o_ref, carry_ref = refs[nblk:]

        def rows_of(i, j):
            return page_refs[i][0, pl.ds(j, SUBS, stride=CMP_STRIDE), :]

    @pl.when(pl.program_id(1) == 0)
    def _():
        carry_ref[...] = jnp.zeros_like(carry_ref)

    acc = None
    for jp in range(CMP_STRIDE // 2):
        halves = []
        for jj in range(2):
            j = 2 * jp + jj
            rows = [rows_of(k * KV_HEADS + h, j) for h in range(KV_HEADS) for k in range(CMP_PAGES)]
            halves.append(jnp.concatenate(rows, axis=0))
        lhs = jnp.concatenate(halves, axis=1).astype(BF16)
        part = _dot(lhs, wp_ref[jp])
        acc = part if acc is None else acc + part
    row = lax.broadcasted_iota(jnp.int32, (n, 1), 0)
    outs = []
    for h in range(KV_HEADS):
        p = acc[h * n:(h + 1) * n]
        first = jnp.concatenate([p[:, 0:CMP_HIDDEN], p[:, 2 * CMP_HIDDEN:3 * CMP_HIDDEN]], axis=1)
        second = jnp.concatenate([p[:, CMP_HIDDEN:2 * CMP_HIDDEN], p[:, 3 * CMP_HIDDEN:]], axis=1)
        prev = jnp.where(row == 0, carry_ref[h:h + 1, :], pltpu.roll(first, 1, axis=0))
        carry_ref[h:h + 1, :] = first[n - 1:n, :]
        hid = _gelu(prev + second + bias_ref[...])
        outs.append(_dot(hid.astype(BF16), w2_ref[...]))
    o_ref[0] = jnp.concatenate(outs, axis=1)


def _cmp_tokens(pool, page_table, wp, bias, w2b, transposed):
    b, n_pages = page_table.shape
    steps = n_pages // CMP_PAGES
    n = CMP_PAGES * SUBS

    def page_spec(k, h):
        if transposed:
            return pl.BlockSpec((1, 1, 2 * HEAD_DIM, PAGE), lambda i, s, pt: (pt[i, s * CMP_PAGES + k], h, 0, 0))
        return pl.BlockSpec((1, PAGE, 2 * HEAD_DIM), lambda i, s, pt: (pt[i, s * CMP_PAGES + k], 0, h))

    scratch = [pltpu.VMEM((SUBLANES, 2 * CMP_HIDDEN), F32)]
    if transposed:
        scratch.append(pltpu.VMEM((CMP_PAGES * KV_HEADS, PAGE, 2 * HEAD_DIM), F32))
    grid_spec = pltpu.PrefetchScalarGridSpec(
        num_scalar_prefetch=1,
        grid=(b, steps),
        in_specs=[page_spec(k, h) for k in range(CMP_PAGES) for h in range(KV_HEADS)] + [
            pl.BlockSpec(wp.shape, lambda i, s, pt: (0, 0, 0)),
            pl.BlockSpec(bias.shape, lambda i, s, pt: (0, 0)),
            pl.BlockSpec(w2b.shape, lambda i, s, pt: (0, 0))],
        out_specs=pl.BlockSpec((1, n, KV_WIDTH), lambda i, s, pt: (i, s, 0)),
        scratch_shapes=scratch,
    )
    return pl.pallas_call(
        functools.partial(_cmp_tokens_kernel, transposed=transposed),
        grid_spec=grid_spec,
        out_shape=jax.ShapeDtypeStruct((b, steps * n, KV_WIDTH), F32),
        compiler_params=_cparams("parallel", "arbitrary"),
        name="cmp_tokens_t" if transposed else "cmp_tokens",
    )(page_table, *([pool] * (CMP_PAGES * KV_HEADS)), wp, bias, w2b)


def _cover_matrix(n_tok, n_sel):
    i = np.arange(n_tok)[:, None]
    start = (i - 1) * CMP_STRIDE
    sj = np.arange(n_sel)[None, :] * SEL_BLOCK
    cov = (start < sj + SEL_BLOCK) & (start + CMP_BLOCK > sj) & (i >= 1)
    return cov.astype(np.float32)


def _split_hi_lo(x):
    hi = x.astype(BF16)
    lo = (x - hi.astype(F32)).astype(BF16)
    return hi, lo


def _nsa_prompt_kernel(q_ref, ckv_ref, kslc_ref, kwin_ref, gates_ref, cover_ref, o_ref, sel_ref, acc_ref,
                       *, tq, tk):
    q0 = pl.program_id(1) * tq
    qt = (q_ref[0] * SCALE).T
    gt = jax.nn.sigmoid(gates_ref[0]).T
    qpos = q0 + lax.broadcasted_iota(jnp.int32, (1, tq), 1)
    lane = lax.broadcasted_iota(jnp.int32, (1, GROUP * tq), 1)
    n_cmp = ckv_ref.shape[1]
    n_sel = cover_ref.shape[0]
    per_tile = tk // SEL_BLOCK
    wide = GROUP * tq

    def tile4(x):
        return jnp.concatenate([x] * GROUP, axis=1)

    rel = lax.broadcasted_iota(jnp.int32, (tk, 1), 0) - lax.broadcasted_iota(jnp.int32, (1, tq), 1)
    rel4 = tile4(rel.astype(F32))
    q4, slope4, brel = [], [], []
    for h in range(KV_HEADS):
        q4.append(jnp.concatenate([qt[(h * GROUP + g) * HEAD_DIM:(h * GROUP + g + 1) * HEAD_DIM, :]
                                   for g in range(GROUP)], axis=1).astype(BF16))
        sl = jnp.zeros((1, wide), F32)
        for g in range(GROUP):
            sl = jnp.where((lane >= g * tq) & (lane < (g + 1) * tq), SLOPES[h * GROUP + g], sl)
        slope4.append(sl)
        brel.append(sl * rel4)

    o_cmp = []
    slot = lax.broadcasted_iota(jnp.int32, (n_cmp, 1), 0)
    dist = qpos - ((slot + 1) * CMP_STRIDE - 1)
    cmask = tile4(((dist >= 0) & (slot >= 1)).astype(F32))
    cdist4 = tile4(dist.astype(F32))
    cov = cover_ref[...].astype(BF16)
    jj = lax.broadcasted_iota(jnp.int32, (n_sel, 1), 0)
    cur = qpos // SEL_BLOCK
    forced = (jj == 0) | (jj == cur) | (jj == cur - 1)
    valid = jj * SEL_BLOCK <= qpos
    for h in range(KV_HEADS):
        ckv = ckv_ref[0, :, h * 2 * HEAD_DIM:(h + 1) * 2 * HEAD_DIM]
        ck = ckv[:, :HEAD_DIM].astype(BF16)
        cvt = ckv.T[HEAD_DIM:, :].astype(BF16)
        s = _dot(ck, q4[h]) - slope4[h] * cdist4
        s = jnp.where(cmask > 0, s, NEG)
        p = jnp.exp(s - jnp.max(s, axis=0, keepdims=True)) * cmask
        p = p / jnp.maximum(jnp.sum(p, axis=0, keepdims=True), 1e-30)
        o_cmp.append(_dot(cvt, p.astype(BF16)))
        psum = p[:, 0:tq]
        for g in range(1, GROUP):
            psum = psum + p[:, g * tq:(g + 1) * tq]
        p_hi, p_lo = _split_hi_lo(psum)
        imp = _dot(cov, p_hi) + _dot(cov, p_lo)
        imp = jnp.where(forced, BIG, jnp.where(valid, imp, NEG))
        cnt = jnp.zeros((n_sel, tq), F32)
        for j2 in range(n_sel):
            r = imp[j2:j2 + 1, :]
            cnt = cnt + jnp.where(r > imp, 1.0, jnp.where((r == imp) & (j2 < jj), 1.0, 0.0))
        sel = jnp.where(cnt < SEL_TOP, 1.0, 0.0)
        for kk in range(n_sel // per_tile):
            sel_ref[h, kk] = sel[kk * per_tile:(kk + 1) * per_tile, :]

    half = lax.broadcasted_iota(jnp.int32, (tk, 1), 0) // SEL_BLOCK

    def attend(kv_ref, lo, hi, mask_fn):
        for h in range(KV_HEADS):
            acc_ref[h] = jnp.zeros((HEAD_DIM, wide), F32)

        def body(kt, carry):
            k0 = pl.multiple_of(kt * tk, tk)
            d = (q0 - k0) - rel
            off = (q0 - k0).astype(F32)
            new = []
            for h in range(KV_HEADS):
                m, l = carry[2 * h], carry[2 * h + 1]
                kv = kv_ref[0, pl.ds(k0, tk), h * 2 * HEAD_DIM:(h + 1) * 2 * HEAD_DIM]
                k = kv[:, :HEAD_DIM].astype(BF16)
                vt = kv.T[HEAD_DIM:, :].astype(BF16)
                ok4 = tile4(mask_fn(h, kt, d))
                sc = jnp.where(ok4 > 0, _dot(k, q4[h]) + brel[h], NEG)
                c = slope4[h] * off
                m_new = jnp.maximum(m, jnp.max(sc, axis=0, keepdims=True) - c)
                pp = jnp.exp(sc - (m_new + c))
                alpha = jnp.exp(m - m_new)
                new.append(m_new)
                new.append(alpha * l + jnp.sum(pp, axis=0, keepdims=True))
                acc_ref[h] = alpha * acc_ref[h] + _dot(vt, pp.astype(BF16))
            return tuple(new)

        init = []
        for h in range(KV_HEADS):
            init += [jnp.full((1, wide), NEG, F32), jnp.zeros((1, wide), F32)]
        fin = lax.fori_loop(lo, hi, body, tuple(init))
        return [acc_ref[h] / jnp.maximum(fin[2 * h + 1], 1e-30) for h in range(KV_HEADS)]

    def slc_mask(h, kt, d):
        rows = sel_ref[h, kt]
        sel = rows[0:1, :]
        for r in range(1, per_tile):
            sel = jnp.where(half == r, rows[r:r + 1, :], sel)
        return jnp.where(d >= 0, sel, 0.0)

    def win_mask(h, kt, d):
        return jnp.where(d >= 0, jnp.where(d < WINDOW, 1.0, 0.0), 0.0)

    hi = (q0 + tq) // tk
    o_slc = attend(kslc_ref, 0, hi, slc_mask)
    o_win = attend(kwin_ref, jnp.maximum(q0 - WINDOW, 0) // tk, hi, win_mask)

    out_rows = []
    for h in range(KV_HEADS):
        for g in range(GROUP):
            hh = h * GROUP + g
            sl = slice(g * tq, (g + 1) * tq)
            out_rows.append(gt[hh:hh + 1, :] * o_cmp[h][:, sl]
                            + gt[NSA_HEADS + hh:NSA_HEADS + hh + 1, :] * o_slc[h][:, sl]
                            + gt[2 * NSA_HEADS + hh:2 * NSA_HEADS + hh + 1, :] * o_win[h][:, sl])
    o_ref[0] = jnp.concatenate(out_rows, axis=0).T


def _nsa_prompt_attn(q, ckv, kv_slc, kv_win, gates, tq=128, tk=128):
    b, t, _ = q.shape
    n_cmp = ckv.shape[1]
    n_sel = t // SEL_BLOCK
    cover_t = jnp.asarray(_cover_matrix(n_cmp, n_sel).T)
    kern = functools.partial(_nsa_prompt_kernel, tq=tq, tk=tk)
    return pl.pallas_call(
        kern,
        grid=(b, t // tq),
        in_specs=[pl.BlockSpec((1, tq, NSA_WIDTH), lambda i, j: (i, j, 0)),
                  pl.BlockSpec((1, n_cmp, KV_WIDTH), lambda i, j: (i, 0, 0)),
                  pl.BlockSpec((1, t, KV_WIDTH), lambda i, j: (i, 0, 0)),
                  pl.BlockSpec((1, t, KV_WIDTH), lambda i, j: (i, 0, 0)),
                  pl.BlockSpec((1, tq, LANES), lambda i, j: (i, j, 0)),
                  pl.BlockSpec((n_sel, n_cmp), lambda i, j: (0, 0))],
        out_specs=pl.BlockSpec((1, tq, NSA_WIDTH), lambda i, j: (i, j, 0)),
        out_shape=jax.ShapeDtypeStruct((b, t, NSA_WIDTH), F32),
        scratch_shapes=[pltpu.VMEM((KV_HEADS, n_sel * SEL_BLOCK // tk, tk // SEL_BLOCK, tq), F32),
                        pltpu.VMEM((KV_HEADS, HEAD_DIM, GROUP * tq), F32)],
        compiler_params=_cparams("parallel", "parallel"),
        name="nsa_prompt_attn",
    )(q, ckv, kv_slc, kv_win, gates, cover_t)


TS = SUBLANES
SEL_LANES = 384


def _nsa_sample_select_kernel(q_ref, ckv_ref, cover_ref, ocmp_ref, idx_ref, *, nbatch, past_len, n_sel):
    n_cmp = ckv_ref.shape[1]
    rowi = lax.broadcasted_iota(jnp.int32, (GROUP * TS, 1), 0)
    tpos = past_len + rowi % TS
    slot = lax.broadcasted_iota(jnp.int32, (1, n_cmp), 1)
    dist = tpos - ((slot + 1) * CMP_STRIDE - 1)
    mask = ((dist >= 0) & (slot >= 1)).astype(F32)
    jj = lax.broadcasted_iota(jnp.int32, (1, SEL_LANES), 1)
    jjf = jj.astype(F32)
    qp8 = past_len + lax.broadcasted_iota(jnp.int32, (TS, 1), 0)
    cur = qp8 // SEL_BLOCK
    forced = (jj == 0) | (jj == cur) | (jj == cur - 1)
    valid = jj * SEL_BLOCK <= qp8
    lane = lax.broadcasted_iota(jnp.int32, (1, LANES), 1)
    cov = cover_ref[...].astype(BF16)
    for bb in range(nbatch):
        q = q_ref[bb] * SCALE
        idx_out = jnp.zeros((TS, LANES), jnp.int32)
        heads = []
        for h in range(KV_HEADS):
            qh = jnp.concatenate([q[:, (h * GROUP + g) * HEAD_DIM:(h * GROUP + g + 1) * HEAD_DIM]
                                  for g in range(GROUP)], axis=0).astype(BF16)
            slope = jnp.zeros((GROUP * TS, 1), F32)
            for g in range(GROUP):
                slope = jnp.where(rowi // TS == g, SLOPES[h * GROUP + g], slope)
            ckv = ckv_ref[bb, :, h * 2 * HEAD_DIM:(h + 1) * 2 * HEAD_DIM]
            s = _dot_nt(qh, ckv[:, :HEAD_DIM].astype(BF16)) - slope * dist.astype(F32)
            s = jnp.where(mask > 0, s, NEG)
            p = jnp.exp(s - jnp.max(s, axis=-1, keepdims=True)) * mask
            p = p / jnp.maximum(jnp.sum(p, axis=-1, keepdims=True), 1e-30)
            o = _dot(p.astype(BF16), ckv[:, HEAD_DIM:].astype(BF16))
            heads += [o[g * TS:(g + 1) * TS, :] for g in range(GROUP)]
            psum = p[0:TS]
            for g in range(1, GROUP):
                psum = psum + p[g * TS:(g + 1) * TS]
            p_hi, p_lo = _split_hi_lo(psum)
            imp = _dot(p_hi, cov) + _dot(p_lo, cov)
            imp = jnp.where(forced, BIG, jnp.where(valid, imp, NEG))
            imp = jnp.where(jj < n_sel, imp, -3e38)
            for k in range(SEL_TOP):
                best = jnp.max(imp, axis=-1, keepdims=True)
                pick = jnp.min(jnp.where(imp == best, jjf, float(SEL_LANES)), axis=-1, keepdims=True)
                idx_out = jnp.where(lane == h * SEL_TOP + k, pick.astype(jnp.int32), idx_out)
                imp = jnp.where(jjf == pick, -3e38, imp)
        ocmp_ref[bb] = jnp.concatenate(heads, axis=-1)
        idx_ref[bb] = idx_out


def _nsa_sample_select(q, ckv, past_len, nbatch=4):
    b = q.shape[0]
    n_cmp = ckv.shape[1]
    n_sel = -(-(past_len + 4) // SEL_BLOCK)
    cov = np.zeros((n_cmp, SEL_LANES), np.float32)
    cov[:, :n_sel] = _cover_matrix(n_cmp, n_sel)
    kern = functools.partial(_nsa_sample_select_kernel, nbatch=nbatch, past_len=past_len, n_sel=n_sel)
    return pl.pallas_call(
        kern,
        grid=(b // nbatch,),
        in_specs=[pl.BlockSpec((nbatch, TS, NSA_WIDTH), lambda i: (i, 0, 0)),
                  pl.BlockSpec((nbatch, n_cmp, KV_WIDTH), lambda i: (i, 0, 0)),
                  pl.BlockSpec((n_cmp, SEL_LANES), lambda i: (0, 0))],
        out_specs=[pl.BlockSpec((nbatch, TS, NSA_WIDTH), lambda i: (i, 0, 0)),
                   pl.BlockSpec((nbatch, TS, LANES), lambda i: (i, 0, 0))],
        out_shape=[jax.ShapeDtypeStruct((b, TS, NSA_WIDTH), F32),
                   jax.ShapeDtypeStruct((b, TS, LANES), jnp.int32)],
        compiler_params=_cparams("parallel"),
        name="nsa_sample_select",
    )(q, ckv, jnp.asarray(cov))


def _nsa_sample_attend_kernel(idx_ref, pt_ref, q_ref, ocmp_ref, gates_ref, knew_ref, wbuf_ref, wnew_ref,
                              pool_ref, o_ref, buf_ref, sem, *, t_dec, past_len):
    b = pl.program_id(0)
    last_blk = past_len // SEL_BLOCK
    per_page = PAGE // SEL_BLOCK

    def block_of(t, h, k):
        return idx_ref[((b * t_dec + t) * KV_HEADS + h) * SEL_TOP + k]

    def page_copy(t, h, k):
        page = pt_ref[b, jnp.minimum(block_of(t, h, k), last_blk - 1) // per_page]
        return pltpu.make_async_copy(pool_ref.at[page, h], buf_ref.at[t, h, k], sem.at[0])

    for t in range(t_dec):
        for h in range(KV_HEADS):
            for k in range(SEL_TOP):
                page_copy(t, h, k).start()
    for t in range(t_dec):
        for h in range(KV_HEADS):
            for k in range(SEL_TOP):
                page_copy(t, h, k).wait()

    def rows_to_t(x):
        return jnp.concatenate([x, jnp.zeros((PAGE - TS, 2 * HEAD_DIM), F32)], axis=0).T

    rowg = lax.broadcasted_iota(jnp.int32, (SUBLANES, 1), 0)
    slope_col = [jnp.zeros((SUBLANES, 1), F32) for _ in range(KV_HEADS)]
    for h in range(KV_HEADS):
        for g in range(GROUP):
            slope_col[h] = jnp.where(rowg == g, SLOPES[h * GROUP + g], slope_col[h])
    n_tok = SEL_TOP * PAGE
    lane = lax.broadcasted_iota(jnp.int32, (1, n_tok), 1)
    row = lane % PAGE
    row_half = row // SEL_BLOCK
    slot_of = lane // PAGE
    n_win = wbuf_ref.shape[-1]
    wlane = lax.broadcasted_iota(jnp.int32, (1, n_win + PAGE), 1)
    wpos = jnp.where(wlane < n_win, past_len - n_win + wlane, past_len + wlane - n_win)

    for h in range(KV_HEADS):
        hs = slice(h * 2 * HEAD_DIM, (h + 1) * 2 * HEAD_DIM)
        new_t = rows_to_t(knew_ref[0, :, hs])
        wnew_t = rows_to_t(wnew_ref[0, :, hs])
        kw_t = jnp.concatenate([wbuf_ref[0, h, 0], wnew_t[:HEAD_DIM]], axis=1).astype(BF16)
        vw_t = jnp.concatenate([wbuf_ref[0, h, 1], wnew_t[HEAD_DIM:]], axis=1).astype(BF16)
        for t in range(t_dec):
            qpos = past_len + t
            qh = (q_ref[0, t, h] * SCALE).astype(BF16)
            k_tiles, v_tiles = [], []
            tok = jnp.zeros((1, n_tok), jnp.int32)
            want_half = jnp.zeros((1, n_tok), jnp.int32)
            for k in range(SEL_TOP):
                blk = block_of(t, h, k)
                is_new = blk >= last_blk
                k_tiles.append(jnp.where(is_new, new_t[:HEAD_DIM], buf_ref[t, h, k, 0]))
                v_tiles.append(jnp.where(is_new, new_t[HEAD_DIM:], buf_ref[t, h, k, 1]))
                tok = jnp.where(slot_of == k, (blk // per_page) * PAGE + row, tok)
                want_half = jnp.where(slot_of == k, blk % per_page, want_half)
            k_t = jnp.concatenate(k_tiles, axis=1).astype(BF16)
            v_t = jnp.concatenate(v_tiles, axis=1).astype(BF16)
            d = qpos - tok
            msk = jnp.where(d >= 0, jnp.where(row_half == want_half, 1.0, 0.0), 0.0)
            s = _dot(qh, k_t) - slope_col[h] * d.astype(F32)
            s = jnp.where(msk > 0, s, NEG)
            p = jnp.exp(s - jnp.max(s, axis=-1, keepdims=True)) * msk
            p = p / jnp.maximum(jnp.sum(p, axis=-1, keepdims=True), 1e-30)
            o_slc = _dot_nt(p.astype(BF16), v_t)
            dw = qpos - wpos
            mw = jnp.where(dw >= 0, jnp.where(dw < WINDOW, 1.0, 0.0), 0.0)
            sw = _dot(qh, kw_t) - slope_col[h] * dw.astype(F32)
            sw = jnp.where(mw > 0, sw, NEG)
            pw = jnp.exp(sw - jnp.max(sw, axis=-1, keepdims=True)) * mw
            pw = pw / jnp.maximum(jnp.sum(pw, axis=-1, keepdims=True), 1e-30)
            o_win = _dot_nt(pw.astype(BF16), vw_t)
            gts = jax.nn.sigmoid(gates_ref[0, t, h])
            o_ref[0, t, h] = (gts[:, 0:1] * ocmp_ref[0, t, h] + gts[:, 1:2] * o_slc + gts[:, 2:3] * o_win)


def _nsa_sample_attend(idx, page_table, q5, ocmp5, gates5, kv_slc_new, win_buf_t, kv_win_new, pool_slc_t,
                       t_dec, past_len):
    b = q5.shape[0]
    n_win = win_buf_t.shape[-1]
    blk5 = (1, t_dec, KV_HEADS, SUBLANES, HEAD_DIM)
    grid_spec = pltpu.PrefetchScalarGridSpec(
        num_scalar_prefetch=2,
        grid=(b,),
        in_specs=[pl.BlockSpec(blk5, lambda i, *_: (i, 0, 0, 0, 0)),
                  pl.BlockSpec(blk5, lambda i, *_: (i, 0, 0, 0, 0)),
                  pl.BlockSpec((1, t_dec, KV_HEADS, SUBLANES, LANES), lambda i, *_: (i, 0, 0, 0, 0)),
                  pl.BlockSpec((1, TS, KV_WIDTH), lambda i, *_: (i, 0, 0)),
                  pl.BlockSpec((1, KV_HEADS, 2, HEAD_DIM, n_win), lambda i, *_: (i, 0, 0, 0, 0)),
                  pl.BlockSpec((1, TS, KV_WIDTH), lambda i, *_: (i, 0, 0)),
                  pl.BlockSpec(memory_space=pl.ANY)],
        out_specs=pl.BlockSpec(blk5, lambda i, *_: (i, 0, 0, 0, 0)),
        scratch_shapes=[pltpu.VMEM((t_dec, KV_HEADS, SEL_TOP, 2, HEAD_DIM, PAGE), F32),
                        pltpu.SemaphoreType.DMA((1,))],
    )
    kern = functools.partial(_nsa_sample_attend_kernel, t_dec=t_dec, past_len=past_len)
    return pl.pallas_call(
        kern,
        grid_spec=grid_spec,
        out_shape=jax.ShapeDtypeStruct((b, t_dec, KV_HEADS, SUBLANES, HEAD_DIM), F32),
        compiler_params=_cparams("arbitrary"),
        name="nsa_sample_attend",
    )(idx[:, :t_dec, :KV_HEADS * SEL_TOP].reshape(-1), page_table, q5, ocmp5, gates5, kv_slc_new, win_buf_t,
      kv_win_new, pool_slc_t)


NSA_SPLITS = ((0, NSA_WIDTH), (NSA_WIDTH, KV_WIDTH), (NSA_WIDTH + KV_WIDTH, KV_WIDTH),
              (NSA_WIDTH + 2 * KV_WIDTH, KV_WIDTH), (NSA_WIDTH + 3 * KV_WIDTH, MEM_WIDTH),
              (NSA_WIDTH + 3 * KV_WIDTH + MEM_WIDTH, LANES))


def _nsa_in_weight(w_in):
    o = NSA_WIDTH + 3 * KV_WIDTH
    n_gate = 3 * NSA_HEADS
    gates = jnp.pad(w_in[:, o:o + n_gate], ((0, 0), (0, LANES - n_gate)))
    return jnp.concatenate([w_in[:, :o], w_in[:, o + n_gate:], gates], axis=1).astype(BF16)


def _to_heads5(x, t_dec):
    b = x.shape[0]
    w = x.shape[-1] // NSA_HEADS
    x = x[:, :t_dec].reshape(b, t_dec, KV_HEADS, GROUP, w)
    return jnp.pad(x, ((0, 0), (0, 0), (0, 0), (0, SUBLANES - GROUP), (0, 0)))


def kernel(x_prompt, x_sample, mem_prompt, state_ssm_re, state_ssm_im, cache_cmp_kv, cache_slc_kv, cache_win_kv, cache_mem_kv, page_table, norm_mix_pre, norm_mix_post, norm_ffn_pre, norm_ffn_post, w_out, w_mem_kv, w_ffn_in, w_ffn_out, ssm_w_in, ssm_a_re, ssm_a_im, ssm_log_dt, ssm_b_re, ssm_b_im, ssm_c_re, ssm_c_im, ssm_d, ssm_w_glu, ssm_b_glu, nsa_w_in, nsa_cmp_pe, nsa_cmp_w1, nsa_cmp_b1, nsa_cmp_w2):
    bp, seq, d = x_prompt.shape
    bs, t_dec, _ = x_sample.shape
    n_mem = mem_prompt.shape[1]
    depth = w_out.shape[0]
    past_len = page_table.shape[1] * PAGE
    d_ff = w_ffn_out.shape[1]
    chunk = 16

    xp = x_prompt.reshape(bp * seq, d)
    xs = jnp.pad(x_sample, ((0, 0), (0, TS - t_dec), (0, 0))).reshape(bs * TS, d)
    mem2d = mem_prompt.reshape(bp * n_mem, d)

    outs = {k: [] for k in ('ssm_re_p', 'ssm_im_p', 'ssm_re_s', 'ssm_im_s', 'cmp_p', 'slc_p', 'win_p',
                            'cmp_s', 'slc_s', 'win_s', 'mem_p')}
    for i in range(depth):
        j = i // 2
        (mkv_p,) = _norm_proj(mem2d, norm_mix_pre[i], w_mem_kv[i].astype(BF16), ((0, 2 * MEM_WIDTH),),
                              do_norm=False)
        mkv_p = mkv_p.reshape(bp, n_mem, 2 * MEM_WIDTH)
        outs['mem_p'].append(mkv_p.reshape(bp, n_mem, MEM_HEADS, 2, HEAD_DIM))
        mkv_s_t = _kv_cache_t(cache_mem_kv[i])
        if i % 2 == 0:
            w_in = ssm_w_in[j].astype(BF16)
            wglu = ssm_w_glu[j].astype(BF16)
            ssm_w = (ssm_a_re[j], ssm_a_im[j], ssm_log_dt[j], ssm_b_re[j], ssm_b_im[j], ssm_c_re[j], ssm_c_im[j])
            ssm_split = ((0, SSM_WIDTH), (SSM_WIDTH, MEM_WIDTH))
            u, qm = _norm_proj(xp, norm_mix_pre[i], w_in, ssm_split)
            zero = jnp.zeros((bp, SSM_GROUPS, SSM_STATE), F32)
            y, hlr, hli = _s5_scan_octets(u.reshape(bp, seq, SSM_WIDTH), zero, zero,
                                          _s5_operators(*ssm_w, chunk), chunk)
            m1p = _s5_glu(y.reshape(bp * seq, SSM_WIDTH), u, ssm_d[j], wglu, ssm_b_glu[j])
            m2p = _mem_attn(qm.reshape(bp, seq, MEM_WIDTH), mkv_p).reshape(bp * seq, MEM_WIDTH)
            outs['ssm_re_p'].append(hlr)
            outs['ssm_im_p'].append(hli)
            u, qm = _norm_proj(xs, norm_mix_pre[i], w_in, ssm_split)
            y, hlr, hli = _s5_scan(u.reshape(bs, TS, SSM_WIDTH)[:, :t_dec], state_ssm_re[j].astype(F32),
                                   state_ssm_im[j].astype(F32), _s5_operators(*ssm_w, t_dec), t_dec)
            y = jnp.pad(y, ((0, 0), (0, TS - t_dec), (0, 0)))
            m1s = _s5_glu(y.reshape(bs * TS, SSM_WIDTH), u, ssm_d[j], wglu, ssm_b_glu[j])
            m2s = _mem_attn_t(qm.reshape(bs, TS, MEM_WIDTH), mkv_s_t).reshape(bs * TS, MEM_WIDTH)
            outs['ssm_re_s'].append(hlr)
            outs['ssm_im_s'].append(hli)
        else:
            w_in = _nsa_in_weight(nsa_w_in[j])
            wp, cbias, w2b = _cmp_weights(nsa_cmp_pe[j], nsa_cmp_w1[j], nsa_cmp_b1[j], nsa_cmp_w2[j])
            kv_shape = (KV_HEADS, 2, HEAD_DIM)
            q, kc, ks, kw, qm, gates = _norm_proj(xp, norm_mix_pre[i], w_in, NSA_SPLITS)
            kc3, ks3, kw3 = (a.reshape(bp, seq, KV_WIDTH) for a in (kc, ks, kw))
            ident = jnp.arange(bp * (seq // PAGE), dtype=jnp.int32).reshape(bp, seq // PAGE)
            ckv = _cmp_tokens(kc.reshape(bp * seq // PAGE, PAGE, KV_WIDTH), ident, wp, cbias, w2b, False)
            m1p = _nsa_prompt_attn(q.reshape(bp, seq, NSA_WIDTH), ckv, ks3, kw3,
                                   gates.reshape(bp, seq, LANES)).reshape(bp * seq, NSA_WIDTH)
            m2p = _mem_attn(qm.reshape(bp, seq, MEM_WIDTH), mkv_p).reshape(bp * seq, MEM_WIDTH)
            outs['cmp_p'].append(kc3.reshape((bp, seq) + kv_shape))
            outs['slc_p'].append(ks3.reshape((bp, seq) + kv_shape))
            outs['win_p'].append(kw3[:, seq - min(WINDOW, seq):].reshape((bp, min(WINDOW, seq)) + kv_shape))
            q, kc, ks, kw, qm, gates = _norm_proj(xs, norm_mix_pre[i], w_in, NSA_SPLITS)
            kc3, ks3, kw3 = (a.reshape(bs, TS, KV_WIDTH) for a in (kc, ks, kw))
            pool_cmp_t = _kv_cache_t(cache_cmp_kv[j])
            pool_cmp_t = pool_cmp_t.reshape(pool_cmp_t.shape[0], KV_HEADS, 2 * HEAD_DIM, PAGE)
            pool_slc_t = _kv_cache_t(cache_slc_kv[j])
            win_buf_t = _kv_cache_t(cache_win_kv[j])
            ckv = _cmp_tokens(pool_cmp_t, page_table, wp, cbias, w2b, True)
            q3 = q.reshape(bs, TS, NSA_WIDTH)
            ocmp, idx = _nsa_sample_select(q3, ckv, past_len)
            gates3 = gates.reshape(bs, TS, LANES)[:, :, :3 * NSA_HEADS].reshape(bs, TS, 3, NSA_HEADS)
            gates5 = _to_heads5(gates3.transpose(0, 1, 3, 2).reshape(bs, TS, NSA_HEADS * 3), t_dec)
            gates5 = jnp.pad(gates5, ((0, 0),) * 4 + ((0, LANES - 3),))
            o5 = _nsa_sample_attend(idx, page_table, _to_heads5(q3, t_dec), _to_heads5(ocmp, t_dec), gates5,
                                    ks3, win_buf_t, kw3, pool_slc_t, t_dec, past_len)
            o = o5[:, :, :, :GROUP].reshape(bs, t_dec, NSA_WIDTH)
            m1s = jnp.pad(o, ((0, 0), (0, TS - t_dec), (0, 0))).reshape(bs * TS, NSA_WIDTH)
            m2s = _mem_attn_t(qm.reshape(bs, TS, MEM_WIDTH), mkv_s_t).reshape(bs * TS, MEM_WIDTH)
            outs['cmp_s'].append(kc3[:, :t_dec].reshape((bs, t_dec) + kv_shape))
            outs['slc_s'].append(ks3[:, :t_dec].reshape((bs, t_dec) + kv_shape))
            kv_w = jnp.concatenate([cache_win_kv[j], kw3[:, :t_dec].reshape((bs, t_dec) + kv_shape)], axis=1)
            n_keep = min(WINDOW, past_len + t_dec)
            outs['win_s'].append(kv_w[:, kv_w.shape[1] - n_keep:])
        wo = w_out[i].astype(BF16)
        wg = w_ffn_in[i][:, :d_ff].astype(BF16)
        wu = w_ffn_in[i][:, d_ff:].astype(BF16)
        wd = w_ffn_out[i].astype(BF16)
        w1 = m1p.shape[1]
        args = (wo[:w1], wo[w1:], norm_mix_post[i], norm_ffn_pre[i], wg, wu, wd, norm_ffn_post[i])
        xp = _post(xp, m1p, m2p, *args)
        xs = _post(xs, m1s, m2s, *args)

    st = lambda k: jnp.stack(outs[k])
    y_sample = xs.reshape(bs, TS, d)[:, :t_dec]
    return (xp.reshape(bp, seq, d), y_sample, st('ssm_re_p'), st('ssm_im_p'), st('ssm_re_s'), st('ssm_im_s'),
            st('cmp_p'), st('slc_p'), st('win_p'), st('cmp_s'), st('slc_s'), st('win_s'), st('mem_p'))
```

```python
import functools
import math

import numpy as np
import jax
import jax.numpy as jnp
from jax import lax
from jax.experimental import pallas as pl
from jax.experimental.pallas import tpu as pltpu

F32 = jnp.float32
BF16 = jnp.bfloat16

D_MODEL = 1024
PAGE = 128
MEM_HEADS = 4
HEAD_DIM = 64
MEM_WIDTH = MEM_HEADS * HEAD_DIM
SSM_WIDTH = D_MODEL - MEM_WIDTH
SSM_GROUP = 16
SSM_GROUPS = SSM_WIDTH // SSM_GROUP
SSM_STATE = 64
NSA_HEADS = 12
KV_HEADS = 3
GROUP = NSA_HEADS // KV_HEADS
NSA_WIDTH = NSA_HEADS * HEAD_DIM
KV_WIDTH = KV_HEADS * 2 * HEAD_DIM
CMP_BLOCK = 32
CMP_STRIDE = 16
CMP_HIDDEN = 2 * HEAD_DIM
SEL_BLOCK = 64
SEL_TOP = 16
WINDOW = 512
RMS_EPS = 1e-6
NEG = -1e30
BIG = 1e30
SCALE = HEAD_DIM ** -0.5
LANES = 128
SUBLANES = 8
VMEM_LIMIT = 56 * 1024 * 1024
SLOPES = [2.0 ** (-8.0 * (h + 1) / NSA_HEADS) for h in range(NSA_HEADS)]


def _cparams(*sem):
    return pltpu.CompilerParams(dimension_semantics=sem, vmem_limit_bytes=VMEM_LIMIT)


def _rms(x, g):
    return x * lax.rsqrt(jnp.mean(x * x, axis=-1, keepdims=True) + RMS_EPS) * g


def _gelu(x):
    return 0.5 * x * (1.0 + jnp.tanh(math.sqrt(2.0 / math.pi) * (x + 0.044715 * (x * x * x))))


def _dot(a, b):
    return jnp.dot(a, b, preferred_element_type=F32)


def _dot_nt(a, b):
    return lax.dot_general(a, b, (((1,), (1,)), ((), ())), preferred_element_type=F32)


def _kv_cache_t(cache):
    n = cache.ndim
    return jnp.moveaxis(cache, n - 4, n - 1)


def _norm_proj_kernel(x_ref, g_ref, w_ref, *o_refs, splits, do_norm):
    x = x_ref[...]
    if do_norm:
        x = _rms(x, g_ref[...])
    z = _dot(x.astype(BF16), w_ref[...])
    for (start, width), o in zip(splits, o_refs):
        o[...] = z[:, start:start + width]


def _norm_proj(x, g, w, splits, do_norm=True, tm=256):
    rows, d = x.shape
    n = w.shape[1]
    tm = min(tm, rows)
    kern = functools.partial(_norm_proj_kernel, splits=tuple(splits), do_norm=do_norm)
    return pl.pallas_call(
        kern,
        grid=(rows // tm,),
        in_specs=[pl.BlockSpec((tm, d), lambda i: (i, 0)),
                  pl.BlockSpec((1, d), lambda i: (0, 0)),
                  pl.BlockSpec((d, n), lambda i: (0, 0))],
        out_specs=[pl.BlockSpec((tm, wd), lambda i: (i, 0)) for _, wd in splits],
        out_shape=[jax.ShapeDtypeStruct((rows, wd), F32) for _, wd in splits],
        compiler_params=_cparams("parallel"),
        name="norm_proj",
    )(x, g.reshape(1, d), w)


def _softmax_rows(s):
    p = jnp.exp(s - jnp.max(s, axis=-1, keepdims=True))
    return p, jnp.sum(p, axis=-1, keepdims=True)


def _mem_attn_kernel(q_ref, kv_ref, o_ref):
    q = q_ref[0]
    kv = kv_ref[0]
    outs = []
    for h in range(MEM_HEADS):
        qh = (q[:, h * HEAD_DIM:(h + 1) * HEAD_DIM] * SCALE).astype(BF16)
        k = kv[:, h * 2 * HEAD_DIM:h * 2 * HEAD_DIM + HEAD_DIM].astype(BF16)
        v = kv[:, h * 2 * HEAD_DIM + HEAD_DIM:(h + 1) * 2 * HEAD_DIM].astype(BF16)
        p, l = _softmax_rows(_dot_nt(qh, k))
        outs.append(_dot(p.astype(BF16), v) / l)
    o_ref[0] = jnp.concatenate(outs, axis=-1)


def _mem_attn(qm, mem_kv, tm=256):
    b, t, _ = qm.shape
    tm = min(tm, t)
    n_mem = mem_kv.shape[1]
    return pl.pallas_call(
        _mem_attn_kernel,
        grid=(b, t // tm),
        in_specs=[pl.BlockSpec((1, tm, MEM_WIDTH), lambda i, j: (i, j, 0)),
                  pl.BlockSpec((1, n_mem, 2 * MEM_WIDTH), lambda i, j: (i, 0, 0))],
        out_specs=pl.BlockSpec((1, tm, MEM_WIDTH), lambda i, j: (i, j, 0)),
        out_shape=jax.ShapeDtypeStruct((b, t, MEM_WIDTH), F32),
        compiler_params=_cparams("parallel", "parallel"),
        name="mem_attn",
    )(qm, mem_kv)


def _mem_attn_t_kernel(q_ref, kv_ref, o_ref):
    q = q_ref[0]
    outs = []
    for h in range(MEM_HEADS):
        qh = (q[:, h * HEAD_DIM:(h + 1) * HEAD_DIM] * SCALE).astype(BF16)
        p, l = _softmax_rows(_dot(qh, kv_ref[0, h, 0].astype(BF16)))
        outs.append(_dot_nt(p.astype(BF16), kv_ref[0, h, 1].astype(BF16)) / l)
    o_ref[0] = jnp.concatenate(outs, axis=-1)


def _mem_attn_t(qm, mem_kv_t):
    b, t, _ = qm.shape
    n_mem = mem_kv_t.shape[-1]
    return pl.pallas_call(
        _mem_attn_t_kernel,
        grid=(b,),
        in_specs=[pl.BlockSpec((1, t, MEM_WIDTH), lambda i: (i, 0, 0)),
                  pl.BlockSpec((1, MEM_HEADS, 2, HEAD_DIM, n_mem), lambda i: (i, 0, 0, 0, 0))],
        out_specs=pl.BlockSpec((1, t, MEM_WIDTH), lambda i: (i, 0, 0)),
        out_shape=jax.ShapeDtypeStruct((b, t, MEM_WIDTH), F32),
        compiler_params=_cparams("parallel"),
        name="mem_attn_t",
    )(qm, mem_kv_t)


def _post_kernel(x_ref, m1_ref, m2_ref, wo1_ref, wo2_ref, g1_ref, g2_ref, wg_ref, wu_ref, wd_ref, g3_ref,
                 o_ref):
    a = _dot(m1_ref[...].astype(BF16), wo1_ref[...]) + _dot(m2_ref[...].astype(BF16), wo2_ref[...])
    x1 = x_ref[...] + _rms(a, g1_ref[...])
    h = _rms(x1, g2_ref[...]).astype(BF16)
    gate = _dot(h, wg_ref[...])
    up = _dot(h, wu_ref[...])
    act = (gate * jax.nn.sigmoid(gate) * up).astype(BF16)
    f = _dot(act, wd_ref[...])
    o_ref[...] = x1 + _rms(f, g3_ref[...])


def _post(x, m1, m2, wo1, wo2, g1, g2, wg, wu, wd, g3, tm=256):
    rows, d = x.shape
    tm = min(tm, rows)
    w1, w2 = m1.shape[1], m2.shape[1]
    dff = wg.shape[1]

    def const(shape):
        return pl.BlockSpec(shape, lambda i: (0, 0), pipeline_mode=pl.Buffered(1))

    return pl.pallas_call(
        _post_kernel,
        grid=(rows // tm,),
        in_specs=[pl.BlockSpec((tm, d), lambda i: (i, 0)),
                  pl.BlockSpec((tm, w1), lambda i: (i, 0)),
                  pl.BlockSpec((tm, w2), lambda i: (i, 0)),
                  const((w1, d)), const((w2, d)), const((1, d)), const((1, d)),
                  const((d, dff)), const((d, dff)), const((dff, d)), const((1, d))],
        out_specs=pl.BlockSpec((tm, d), lambda i: (i, 0)),
        out_shape=jax.ShapeDtypeStruct((rows, d), F32),
        compiler_params=_cparams("parallel"),
        name="post_ffn",
    )(x, m1, m2, wo1, wo2, g1.reshape(1, d), g2.reshape(1, d), wg, wu, wd, g3.reshape(1, d))


def _s5_operators(a_re, a_im, log_dt, b_re, b_im, c_re, c_im, q):
    hp = lax.Precision.HIGHEST
    a_re, a_im = a_re.astype(F32), a_im.astype(F32)
    dt = jnp.exp(log_dt.astype(F32))[:, None]
    mag = jnp.exp(a_re * dt)
    ab_r, ab_i = mag * jnp.cos(a_im * dt), mag * jnp.sin(a_im * dt)
    den = a_re * a_re + a_im * a_im
    nr, ni = ab_r - 1.0, ab_i
    f_r = (nr * a_re + ni * a_im) / den
    f_i = (ni * a_re - nr * a_im) / den
    bb_r = f_r[..., None] * b_re - f_i[..., None] * b_im
    bb_i = f_r[..., None] * b_im + f_i[..., None] * b_re
    pw_r, pw_i = [jnp.ones_like(ab_r)], [jnp.zeros_like(ab_i)]
    for _ in range(q):
        r, i = pw_r[-1], pw_i[-1]
        pw_r.append(r * ab_r - i * ab_i)
        pw_i.append(r * ab_i + i * ab_r)
    pw_r, pw_i = jnp.stack(pw_r), jnp.stack(pw_i)
    w_r = pw_r[:, :, :, None] * bb_r[None] - pw_i[:, :, :, None] * bb_i[None]
    w_i = pw_r[:, :, :, None] * bb_i[None] + pw_i[:, :, :, None] * bb_r[None]
    kk = (jnp.einsum('gip,tgpj->tgij', c_re, w_r, precision=hp)
          - jnp.einsum('gip,tgpj->tgij', c_im, w_i, precision=hp))
    tau = jnp.arange(q)[None, :] - jnp.arange(q)[:, None]
    kt = kk[jnp.clip(tau, 0, q)]
    kt = jnp.where((tau >= 0)[:, :, None, None, None], kt, 0.0)
    g = a_re.shape[0]
    toep = kt.transpose(2, 0, 4, 1, 3).reshape(g, q * SSM_GROUP, q * SSM_GROUP)
    rev = q - 1 - jnp.arange(q)
    bst_r = w_r[rev].transpose(1, 0, 3, 2).reshape(g, q * SSM_GROUP, SSM_STATE)
    bst_i = w_i[rev].transpose(1, 0, 3, 2).reshape(g, q * SSM_GROUP, SSM_STATE)
    ar, ai = pw_r[1:], pw_i[1:]
    co_r = c_re[None] * ar[:, :, None, :] - c_im[None] * ai[:, :, None, :]
    co_i = -c_re[None] * ai[:, :, None, :] - c_im[None] * ar[:, :, None, :]
    cout_r = co_r.transpose(1, 3, 0, 2).reshape(g, SSM_STATE, q * SSM_GROUP)
    cout_i = co_i.transpose(1, 3, 0, 2).reshape(g, SSM_STATE, q * SSM_GROUP)
    return (toep.astype(BF16), bst_r.astype(BF16), bst_i.astype(BF16), cout_r.astype(BF16),
            cout_i.astype(BF16), pw_r[q][:, None, :], pw_i[q][:, None, :])


def _s5_scan_kernel(u_ref, toep_ref, bsr_ref, bsi_ref, cor_ref, coi_ref, aqr_ref, aqi_ref, h0r_ref, h0i_ref,
                    y_ref, hlr_ref, hli_ref, sr_ref, si_ref, hpr_ref, hpi_ref, *, gs, nc, nb):
    for g in range(gs):
        u = u_ref[g].astype(BF16)
        sr_ref[g] = _dot(u, bsr_ref[g])
        si_ref[g] = _dot(u, bsi_ref[g])

    def step(c, carry):
        rows = pl.ds(pl.multiple_of(c * nb, SUBLANES), nb)
        new = []
        for g in range(gs):
            hr, hi = carry[2 * g], carry[2 * g + 1]
            hpr_ref[g, rows, :] = hr
            hpi_ref[g, rows, :] = hi
            ar, ai = aqr_ref[g], aqi_ref[g]
            new.append(ar * hr - ai * hi + sr_ref[g, rows, :])
            new.append(ar * hi + ai * hr + si_ref[g, rows, :])
        return tuple(new)

    init = []
    for g in range(gs):
        init += [h0r_ref[g], h0i_ref[g]]
    fin = lax.fori_loop(0, nc, step, tuple(init))
    for g in range(gs):
        hlr_ref[g] = fin[2 * g]
        hli_ref[g] = fin[2 * g + 1]
        u = u_ref[g].astype(BF16)
        y_ref[g] = (_dot(u, toep_ref[g]) + _dot(hpr_ref[g].astype(BF16), cor_ref[g])
                    + _dot(hpi_ref[g].astype(BF16), coi_ref[g]))


def _s5_scan(u, h0r, h0i, ops, q, gs=4):
    b, l, _ = u.shape
    nc = l // q
    qw = q * SSM_GROUP
    toep, bsr, bsi, cor, coi, aqr, aqi = ops
    ug = u.reshape(b, nc, q, SSM_GROUPS, SSM_GROUP).transpose(3, 1, 0, 2, 4).reshape(SSM_GROUPS, nc * b, qw)
    h0r_g = h0r.transpose(1, 0, 2)
    h0i_g = h0i.transpose(1, 0, 2)
    rows = nc * b

    def gspec(r, c):
        return pl.BlockSpec((gs, r, c), lambda i: (i, 0, 0))

    kern = functools.partial(_s5_scan_kernel, gs=gs, nc=nc, nb=b)
    y, hlr, hli = pl.pallas_call(
        kern,
        grid=(SSM_GROUPS // gs,),
        in_specs=[gspec(rows, qw), gspec(qw, qw), gspec(qw, SSM_STATE), gspec(qw, SSM_STATE),
                  gspec(SSM_STATE, qw), gspec(SSM_STATE, qw), gspec(1, SSM_STATE), gspec(1, SSM_STATE),
                  gspec(b, SSM_STATE), gspec(b, SSM_STATE)],
        out_specs=[gspec(rows, qw), gspec(b, SSM_STATE), gspec(b, SSM_STATE)],
        out_shape=[jax.ShapeDtypeStruct((SSM_GROUPS, rows, qw), F32),
                   jax.ShapeDtypeStruct((SSM_GROUPS, b, SSM_STATE), F32),
                   jax.ShapeDtypeStruct((SSM_GROUPS, b, SSM_STATE), F32)],
        scratch_shapes=[pltpu.VMEM((gs, rows, SSM_STATE), F32) for _ in range(4)],
        compiler_params=_cparams("parallel"),
        name="s5_scan_groups",
    )(ug, toep, bsr, bsi, cor, coi, aqr, aqi, h0r_g, h0i_g)
    y = y.reshape(SSM_GROUPS, nc, b, q, SSM_GROUP).transpose(2, 1, 3, 0, 4).reshape(b, l, SSM_WIDTH)
    return y, hlr.transpose(1, 0, 2), hli.transpose(1, 0, 2)


OCT = LANES // SSM_GROUP
N_OCT = SSM_GROUPS // OCT
OCT_STATE = OCT * SSM_STATE


def _s5_octet_operators(ops, q):
    toep, bsr, bsi, cor, coi, aqr, aqi = ops
    eye = jnp.eye(OCT, dtype=toep.dtype)
    qq = q * LANES
    lag = toep.reshape(N_OCT, OCT, q, SSM_GROUP, q, SSM_GROUP)[:, :, 0]
    k8 = jnp.einsum('ogjti,gh->otgjhi', lag, eye).reshape(N_OCT, q, LANES, LANES)

    def bst8(b):
        b8 = jnp.einsum('ogsjp,gh->osgjhp', b.reshape(N_OCT, OCT, q, SSM_GROUP, SSM_STATE), eye)
        return b8.reshape(N_OCT, qq, OCT_STATE)

    def cout8(c):
        c8 = jnp.einsum('ogpti,gh->ogpthi', c.reshape(N_OCT, OCT, SSM_STATE, q, SSM_GROUP), eye)
        return c8.reshape(N_OCT, OCT_STATE, qq)

    return (k8, bst8(bsr), bst8(bsi), cout8(cor), cout8(coi),
            aqr.reshape(N_OCT, 1, OCT_STATE), aqi.reshape(N_OCT, 1, OCT_STATE))


def _s5_octet_kernel(x_ref, lag_ref, bsr_ref, bsi_ref, cor_ref, coi_ref, aqr_ref, aqi_ref, h0r_ref, h0i_ref,
                     y_ref, hlr_ref, hli_ref, sr_ref, si_ref, hpr_ref, hpi_ref, hr_ref, hi_ref, toep_ref,
                     *, ncb, nb, q):
    @pl.when(pl.program_id(1) == 0)
    def _():
        hr_ref[...] = h0r_ref[0]
        hi_ref[...] = h0i_ref[0]
        zero = jnp.zeros((LANES, LANES), BF16)
        for s in range(q):
            for t in range(q):
                toep_ref[s * LANES:(s + 1) * LANES, t * LANES:(t + 1) * LANES] = (
                    lag_ref[0, t - s] if t >= s else zero)

    x = x_ref[0]
    sr_ref[...] = _dot(x, bsr_ref[0])
    si_ref[...] = _dot(x, bsi_ref[0])
    ar, ai = aqr_ref[0], aqi_ref[0]

    def step(c, carry):
        hr, hi = carry
        rows = pl.ds(pl.multiple_of(c * nb, SUBLANES), nb)
        hpr_ref[rows, :] = hr
        hpi_ref[rows, :] = hi
        return ar * hr - ai * hi + sr_ref[rows, :], ar * hi + ai * hr + si_ref[rows, :]

    hr, hi = lax.fori_loop(0, ncb, step, (hr_ref[...], hi_ref[...]))
    hr_ref[...] = hr
    hi_ref[...] = hi
    hlr_ref[0] = hr
    hli_ref[0] = hi
    y_ref[0] = (_dot(x, toep_ref[...]) + _dot(hpr_ref[...].astype(BF16), cor_ref[0])
                + _dot(hpi_ref[...].astype(BF16), coi_ref[0]))


def _s5_scan_octets(u, h0r, h0i, ops, q, row_blocks=4):
    b, l, _ = u.shape
    nc = l // q
    qq = q * LANES
    rows = nc * b
    rb = rows // row_blocks
    ncb = nc // row_blocks
    ops8 = _s5_octet_operators(ops, q)
    x = u.reshape(b, nc, q, N_OCT, LANES).transpose(3, 1, 0, 2, 4).reshape(N_OCT, rows, qq).astype(BF16)
    h0r8 = h0r.reshape(b, N_OCT, OCT_STATE).transpose(1, 0, 2)
    h0i8 = h0i.reshape(b, N_OCT, OCT_STATE).transpose(1, 0, 2)

    def wspec(r, c):
        return pl.BlockSpec((1, r, c), lambda o, i: (o, 0, 0))

    kern = functools.partial(_s5_octet_kernel, ncb=ncb, nb=b, q=q)
    y, hlr, hli = pl.pallas_call(
        kern,
        grid=(N_OCT, row_blocks),
        in_specs=[pl.BlockSpec((1, rb, qq), lambda o, i: (o, i, 0)),
                  pl.BlockSpec((1, q, LANES, LANES), lambda o, i: (o, 0, 0, 0)),
                  wspec(qq, OCT_STATE), wspec(qq, OCT_STATE), wspec(OCT_STATE, qq),
                  wspec(OCT_STATE, qq), wspec(1, OCT_STATE), wspec(1, OCT_STATE),
                  wspec(b, OCT_STATE), wspec(b, OCT_STATE)],
        out_specs=[pl.BlockSpec((1, rb, qq), lambda o, i: (o, i, 0)), wspec(b, OCT_STATE), wspec(b, OCT_STATE)],
        out_shape=[jax.ShapeDtypeStruct((N_OCT, rows, qq), F32),
                   jax.ShapeDtypeStruct((N_OCT, b, OCT_STATE), F32),
                   jax.ShapeDtypeStruct((N_OCT, b, OCT_STATE), F32)],
        scratch_shapes=[pltpu.VMEM((rb, OCT_STATE), F32) for _ in range(4)]
        + [pltpu.VMEM((b, OCT_STATE), F32) for _ in range(2)] + [pltpu.VMEM((qq, qq), BF16)],
        compiler_params=_cparams("parallel", "arbitrary"),
        name="s5_scan_octets",
    )(x, *ops8, h0r8, h0i8)
    y = y.reshape(N_OCT, nc, b, q, LANES).transpose(2, 1, 3, 0, 4).reshape(b, l, SSM_WIDTH)
    unpack = lambda h: h.transpose(1, 0, 2).reshape(b, SSM_GROUPS, SSM_STATE)
    return y, unpack(hlr), unpack(hli)


def _s5_glu_kernel(y_ref, u_ref, d_ref, w_ref, b_ref, o_ref):
    v = _gelu(y_ref[...] + d_ref[...] * u_ref[...])
    o_ref[...] = v * jax.nn.sigmoid(_dot(v.astype(BF16), w_ref[...]) + b_ref[...])


def _s5_glu(y, u, d, w, bias, tm=256):
    rows, n = y.shape
    tm = min(tm, rows)
    return pl.pallas_call(
        _s5_glu_kernel,
        grid=(rows // tm,),
        in_specs=[pl.BlockSpec((tm, n), lambda i: (i, 0)), pl.BlockSpec((tm, n), lambda i: (i, 0)),
                  pl.BlockSpec((1, n), lambda i: (0, 0)), pl.BlockSpec((n, n), lambda i: (0, 0)),
                  pl.BlockSpec((1, n), lambda i: (0, 0))],
        out_specs=pl.BlockSpec((tm, n), lambda i: (i, 0)),
        out_shape=jax.ShapeDtypeStruct((rows, n), F32),
        compiler_params=_cparams("parallel"),
        name="s5_glu",
    )(y, u, d.reshape(1, n), w, bias.reshape(1, n))


CMP_PAGES = 16
CMP_GROUPS = 2
SUBS = PAGE // CMP_STRIDE


def _cmp_weights(pe, w1, b1, w2):
    w = w1.reshape(2, CMP_STRIDE // 2, 2, 2, HEAD_DIM, CMP_HIDDEN)
    eye = jnp.eye(2, dtype=w1.dtype)
    wp = jnp.einsum('apjcef,cd->pjcedaf', w, eye)
    wp = wp.reshape(CMP_STRIDE // 2, 2 * 2 * HEAD_DIM, 2 * 2 * CMP_HIDDEN)
    bias = (jnp.einsum('jce,jcef->cf', pe, w1, precision=lax.Precision.HIGHEST) + b1).reshape(1, 2 * CMP_HIDDEN)
    w2b = jnp.einsum('cfe,cd->cfde', w2, eye).reshape(2 * CMP_HIDDEN, 2 * HEAD_DIM)
    return wp.astype(BF16), bias.astype(F32), w2b.astype(BF16)


def _cmp_tokens_kernel(pt_ref, *refs, transposed):
    n = CMP_PAGES * SUBS
    if transposed:
        nblk = CMP_PAGES
        page_refs = refs[:nblk]
        wp_ref, bias_ref, w2_ref, o_ref, carry_ref = refs[nblk:nblk + 5]
        xs_refs = refs[nblk + 5:]
        per_group = CMP_PAGES * KV_HEADS // CMP_GROUPS

        def stage(pages):
            for k in pages:
                for h in range(KV_HEADS):
                    i = k * KV_HEADS + h
                    xs_refs[i // per_group][i % per_group] = page_refs[k][0, h].T

        def rows_of(i, j):
            return xs_refs[i // per_group][i % per_group, pl.ds(j, SUBS, stride=CMP_STRIDE), :]
    else:
        nblk = CMP_PAGES * KV_HEADS
        page_refs = refs[:nblk]
        wp_ref, bias_ref, w2_ref, o_ref, carry_ref = refs[nblk:]

        def rows_of(i, j):
            return page_refs[i][0, pl.ds(j, SUBS, stride=CMP_STRIDE), :]

    @pl.when(pl.program_id(1) == 0)
    def _():
        carry_ref[...] = jnp.zeros_like(carry_ref)

    groups = CMP_GROUPS if transposed else 1
    pages_per = CMP_PAGES // groups
    ng = pages_per * SUBS
    accs = []
    for gi in range(groups):
        pages = range(gi * pages_per, (gi + 1) * pages_per)
        if transposed:
            stage(pages)
        acc = None
        for jp in range(CMP_STRIDE // 2):
            halves = []
            for jj in range(2):
                j = 2 * jp + jj
                rows = [rows_of(k * KV_HEADS + h, j) for h in range(KV_HEADS) for k in pages]
                halves.append(jnp.concatenate(rows, axis=0))
            lhs = jnp.concatenate(halves, axis=1).astype(BF16)
            part = _dot(lhs, wp_ref[jp])
            acc = part if acc is None else acc + part
        accs.append(acc)
    row = lax.broadcasted_iota(jnp.int32, (n, 1), 0)
    outs = []
    for h in range(KV_HEADS):
        p = jnp.concatenate([a[h * ng:(h + 1) * ng] for a in accs], axis=0)
        first = jnp.concatenate([p[:, 0:CMP_HIDDEN], p[:, 2 * CMP_HIDDEN:3 * CMP_HIDDEN]], axis=1)
        second = jnp.concatenate([p[:, CMP_HIDDEN:2 * CMP_HIDDEN], p[:, 3 * CMP_HIDDEN:]], axis=1)
        prev = jnp.where(row == 0, carry_ref[h:h + 1, :], pltpu.roll(first, 1, axis=0))
        carry_ref[h:h + 1, :] = first[n - 1:n, :]
        hid = _gelu(prev + second + bias_ref[...])
        outs.append(_dot(hid.astype(BF16), w2_ref[...]))
    o_ref[0] = jnp.concatenate(outs, axis=1)


def _cmp_tokens(pool, page_table, wp, bias, w2b, transposed):
    b, n_pages = page_table.shape
    steps = n_pages // CMP_PAGES
    n = CMP_PAGES * SUBS

    def page_spec(k, h):
        return pl.BlockSpec((1, PAGE, 2 * HEAD_DIM), lambda i, s, pt: (pt[i, s * CMP_PAGES + k], 0, h))

    def page_spec_t(k):
        return pl.BlockSpec((1, KV_HEADS, 2 * HEAD_DIM, PAGE), lambda i, s, pt: (pt[i, s * CMP_PAGES + k], 0, 0, 0))

    scratch = [pltpu.VMEM((SUBLANES, 2 * CMP_HIDDEN), F32)]
    if transposed:
        scratch += [pltpu.VMEM((CMP_PAGES * KV_HEADS // CMP_GROUPS, PAGE, 2 * HEAD_DIM), F32)
                    for _ in range(CMP_GROUPS)]
        page_specs = [page_spec_t(k) for k in range(CMP_PAGES)]
    else:
        page_specs = [page_spec(k, h) for k in range(CMP_PAGES) for h in range(KV_HEADS)]
    grid_spec = pltpu.PrefetchScalarGridSpec(
        num_scalar_prefetch=1,
        grid=(b, steps),
        in_specs=page_specs + [
            pl.BlockSpec(wp.shape, lambda i, s, pt: (0, 0, 0)),
            pl.BlockSpec(bias.shape, lambda i, s, pt: (0, 0)),
            pl.BlockSpec(w2b.shape, lambda i, s, pt: (0, 0))],
        out_specs=pl.BlockSpec((1, n, KV_WIDTH), lambda i, s, pt: (i, s, 0)),
        scratch_shapes=scratch,
    )
    return pl.pallas_call(
        functools.partial(_cmp_tokens_kernel, transposed=transposed),
        grid_spec=grid_spec,
        out_shape=jax.ShapeDtypeStruct((b, steps * n, KV_WIDTH), F32),
        compiler_params=_cparams("parallel", "arbitrary"),
        name="cmp_tokens_t" if transposed else "cmp_tokens",
    )(page_table, *([pool] * len(page_specs)), wp, bias, w2b)


def _cover_matrix(n_tok, n_sel):
    i = np.arange(n_tok)[:, None]
    start = (i - 1) * CMP_STRIDE
    sj = np.arange(n_sel)[None, :] * SEL_BLOCK
    cov = (start < sj + SEL_BLOCK) & (start + CMP_BLOCK > sj) & (i >= 1)
    return cov.astype(np.float32)


def _split_hi_lo(x):
    hi = x.astype(BF16)
    lo = (x - hi.astype(F32)).astype(BF16)
    return hi, lo


def _nsa_prompt_kernel(q_ref, ckv_ref, kslc_ref, kwin_ref, gates_ref, cover_ref, o_ref, sel_ref, acc_ref,
                       s_ref, p_ref, *, tq, tk):
    q0 = pl.program_id(1) * tq
    qt = (q_ref[0] * SCALE).T
    gt = jax.nn.sigmoid(gates_ref[0]).T
    qpos = q0 + lax.broadcasted_iota(jnp.int32, (1, tq), 1)
    lane = lax.broadcasted_iota(jnp.int32, (1, GROUP * tq), 1)
    n_cmp = ckv_ref.shape[1]
    n_sel = cover_ref.shape[0]
    per_tile = tk // SEL_BLOCK
    wide = GROUP * tq

    def tile4(x):
        return jnp.concatenate([x] * GROUP, axis=1)

    ik = lax.broadcasted_iota(jnp.int32, (tk, 1), 0)
    rel = ik - lax.broadcasted_iota(jnp.int32, (1, tq), 1)
    row64 = lax.broadcasted_iota(jnp.int32, (HEAD_DIM, 1), 0)
    lane128 = lax.broadcasted_iota(jnp.int32, (1, 2 * HEAD_DIM), 1)
    kfeat = jnp.where((lane128 == HEAD_DIM) | (lane128 == HEAD_DIM + 1), ik.astype(F32), 0.0)
    q4, slope4, qaug = [], [], []
    for h in range(KV_HEADS):
        qf = jnp.concatenate([qt[(h * GROUP + g) * HEAD_DIM:(h * GROUP + g + 1) * HEAD_DIM, :]
                              for g in range(GROUP)], axis=1)
        q4.append(qf.astype(BF16))
        sl = jnp.zeros((1, wide), F32)
        for g in range(GROUP):
            sl = jnp.where((lane >= g * tq) & (lane < (g + 1) * tq), SLOPES[h * GROUP + g], sl)
        slope4.append(sl)
        sl_hi = sl.astype(BF16).astype(F32)
        extra = jnp.where(row64 == 0, sl_hi, jnp.where(row64 == 1, sl - sl_hi, 0.0))
        qaug.append(jnp.concatenate([qf, extra], axis=0).astype(BF16))

    o_cmp = []
    slot = lax.broadcasted_iota(jnp.int32, (n_cmp, 1), 0)
    dist = qpos - ((slot + 1) * CMP_STRIDE - 1)
    cmask = tile4(((dist >= 0) & (slot >= 1)).astype(F32))
    cdist4 = tile4(dist.astype(F32))
    cov = cover_ref[...].astype(BF16)
    jj = lax.broadcasted_iota(jnp.int32, (n_sel, 1), 0)
    cur = qpos // SEL_BLOCK
    forced = (jj == 0) | (jj == cur) | (jj == cur - 1)
    valid = jj * SEL_BLOCK <= qpos
    for h in range(KV_HEADS):
        ckv = ckv_ref[0, :, h * 2 * HEAD_DIM:(h + 1) * 2 * HEAD_DIM]
        ck = ckv[:, :HEAD_DIM].astype(BF16)
        cvt = ckv.T[HEAD_DIM:, :].astype(BF16)
        s = _dot(ck, q4[h]) - slope4[h] * cdist4
        s = jnp.where(cmask > 0, s, NEG)
        p = jnp.exp(s - jnp.max(s, axis=0, keepdims=True)) * cmask
        p = p / jnp.maximum(jnp.sum(p, axis=0, keepdims=True), 1e-30)
        o_cmp.append(_dot(cvt, p.astype(BF16)))
        psum = p[:, 0:tq]
        for g in range(1, GROUP):
            psum = psum + p[:, g * tq:(g + 1) * tq]
        p_hi, p_lo = _split_hi_lo(psum)
        imp = _dot(cov, p_hi) + _dot(cov, p_lo)
        imp = jnp.where(forced, BIG, jnp.where(valid, imp, NEG))
        cnt = jnp.zeros((n_sel, tq), F32)
        for j2 in range(n_sel):
            r = imp[j2:j2 + 1, :]
            cnt = cnt + jnp.where(r > imp, 1.0, jnp.where((r == imp) & (j2 < jj), 1.0, 0.0))
        sel = jnp.where(cnt < SEL_TOP, BIG, NEG)
        for kk in range(n_sel // per_tile):
            sel_ref[h, kk] = sel[kk * per_tile:(kk + 1) * per_tile, :]

    half = ik // SEL_BLOCK
    ones_rows = jnp.ones((SUBLANES, tk), F32)

    def attend(kv_ref, lo, hi, cap_fn):
        for h in range(KV_HEADS):
            acc_ref[h] = jnp.zeros((HEAD_DIM + SUBLANES, wide), F32)

        def body(kt, ms):
            k0 = pl.multiple_of(kt * tk, tk)
            d = (q0 - k0) - rel
            k0f = k0.astype(F32)
            for h in range(KV_HEADS):
                kv = kv_ref[0, pl.ds(k0, tk), h * 2 * HEAD_DIM:(h + 1) * 2 * HEAD_DIM]
                k_aug = jnp.where(lane128 < HEAD_DIM, kv, kfeat).astype(BF16)
                s_ref[h] = _dot(k_aug, qaug[h])
            new, alphas = [], []
            for h in range(KV_HEADS):
                sc = jnp.minimum(s_ref[h], tile4(cap_fn(h, kt, d)))
                c = slope4[h] * k0f
                m_new = jnp.maximum(ms[h], jnp.max(sc, axis=0, keepdims=True) + c)
                p_ref[h] = jnp.exp(sc - (m_new - c)).astype(BF16)
                alphas.append(jnp.exp(ms[h] - m_new))
                new.append(m_new)
            for h in range(KV_HEADS):
                kv = kv_ref[0, pl.ds(k0, tk), h * 2 * HEAD_DIM:(h + 1) * 2 * HEAD_DIM]
                vt = jnp.concatenate([kv.T[HEAD_DIM:, :], ones_rows], axis=0).astype(BF16)
                acc_ref[h] = alphas[h] * acc_ref[h] + _dot(vt, p_ref[h])
            return tuple(new)

        lax.fori_loop(lo, hi, body, tuple(jnp.full((1, wide), NEG, F32) for _ in range(KV_HEADS)))
        return [acc_ref[h, :HEAD_DIM, :] / jnp.maximum(acc_ref[h, HEAD_DIM:HEAD_DIM + 1, :], 1e-30)
                for h in range(KV_HEADS)]

    def slc_cap(h, kt, d):
        rows = sel_ref[h, kt]
        cap = rows[0:1, :]
        for r in range(1, per_tile):
            cap = jnp.where(half == r, rows[r:r + 1, :], cap)
        return jnp.where(d >= 0, cap, NEG)

    def win_cap(h, kt, d):
        return jnp.where(d >= 0, jnp.where(d < WINDOW, BIG, NEG), NEG)

    hi = (q0 + tq) // tk
    o_slc = attend(kslc_ref, 0, hi, slc_cap)
    o_win = attend(kwin_ref, jnp.maximum(q0 - WINDOW, 0) // tk, hi, win_cap)

    out_rows = []
    for h in range(KV_HEADS):
        for g in range(GROUP):
            hh = h * GROUP + g
            sl = slice(g * tq, (g + 1) * tq)
            out_rows.append(gt[hh:hh + 1, :] * o_cmp[h][:, sl]
                            + gt[NSA_HEADS + hh:NSA_HEADS + hh + 1, :] * o_slc[h][:, sl]
                            + gt[2 * NSA_HEADS + hh:2 * NSA_HEADS + hh + 1, :] * o_win[h][:, sl])
    o_ref[0] = jnp.concatenate(out_rows, axis=0).T


def _nsa_prompt_attn(q, ckv, kv_slc, kv_win, gates, tq=128, tk=128):
    b, t, _ = q.shape
    n_cmp = ckv.shape[1]
    n_sel = t // SEL_BLOCK
    cover_t = jnp.asarray(_cover_matrix(n_cmp, n_sel).T)
    kern = functools.partial(_nsa_prompt_kernel, tq=tq, tk=tk)
    return pl.pallas_call(
        kern,
        grid=(b, t // tq),
        in_specs=[pl.BlockSpec((1, tq, NSA_WIDTH), lambda i, j: (i, j, 0)),
                  pl.BlockSpec((1, n_cmp, KV_WIDTH), lambda i, j: (i, 0, 0)),
                  pl.BlockSpec((1, t, KV_WIDTH), lambda i, j: (i, 0, 0)),
                  pl.BlockSpec((1, t, KV_WIDTH), lambda i, j: (i, 0, 0)),
                  pl.BlockSpec((1, tq, LANES), lambda i, j: (i, j, 0)),
                  pl.BlockSpec((n_sel, n_cmp), lambda i, j: (0, 0))],
        out_specs=pl.BlockSpec((1, tq, NSA_WIDTH), lambda i, j: (i, j, 0)),
        out_shape=jax.ShapeDtypeStruct((b, t, NSA_WIDTH), F32),
        scratch_shapes=[pltpu.VMEM((KV_HEADS, n_sel * SEL_BLOCK // tk, tk // SEL_BLOCK, tq), F32),
                        pltpu.VMEM((KV_HEADS, HEAD_DIM + SUBLANES, GROUP * tq), F32),
                        pltpu.VMEM((KV_HEADS, tk, GROUP * tq), F32),
                        pltpu.VMEM((KV_HEADS, tk, GROUP * tq), BF16)],
        compiler_params=_cparams("parallel", "parallel"),
        name="nsa_prompt_attn",
    )(q, ckv, kv_slc, kv_win, gates, cover_t)


TS = SUBLANES
SEL_LANES = 384


def _nsa_sample_select_kernel(q_ref, ckv_ref, cover_ref, ocmp_ref, idx_ref, *, nbatch, past_len, n_sel):
    n_cmp = ckv_ref.shape[1]
    rowi = lax.broadcasted_iota(jnp.int32, (GROUP * TS, 1), 0)
    tpos = past_len + rowi % TS
    slot = lax.broadcasted_iota(jnp.int32, (1, n_cmp), 1)
    dist = tpos - ((slot + 1) * CMP_STRIDE - 1)
    mask = ((dist >= 0) & (slot >= 1)).astype(F32)
    jj = lax.broadcasted_iota(jnp.int32, (1, SEL_LANES), 1)
    jjf = jj.astype(F32)
    qp8 = past_len + lax.broadcasted_iota(jnp.int32, (TS, 1), 0)
    cur = qp8 // SEL_BLOCK
    forced = (jj == 0) | (jj == cur) | (jj == cur - 1)
    valid = jj * SEL_BLOCK <= qp8
    lane = lax.broadcasted_iota(jnp.int32, (1, LANES), 1)
    cov = cover_ref[...].astype(BF16)
    for bb in range(nbatch):
        q = q_ref[bb] * SCALE
        idx_out = jnp.zeros((TS, LANES), jnp.int32)
        heads = []
        for h in range(KV_HEADS):
            qh = jnp.concatenate([q[:, (h * GROUP + g) * HEAD_DIM:(h * GROUP + g + 1) * HEAD_DIM]
                                  for g in range(GROUP)], axis=0).astype(BF16)
            slope = jnp.zeros((GROUP * TS, 1), F32)
            for g in range(GROUP):
                slope = jnp.where(rowi // TS == g, SLOPES[h * GROUP + g], slope)
            ckv = ckv_ref[bb, :, h * 2 * HEAD_DIM:(h + 1) * 2 * HEAD_DIM]
            s = _dot_nt(qh, ckv[:, :HEAD_DIM].astype(BF16)) - slope * dist.astype(F32)
            s = jnp.where(mask > 0, s, NEG)
            p = jnp.exp(s - jnp.max(s, axis=-1, keepdims=True)) * mask
            p = p / jnp.maximum(jnp.sum(p, axis=-1, keepdims=True), 1e-30)
            o = _dot(p.astype(BF16), ckv[:, HEAD_DIM:].astype(BF16))
            heads += [o[g * TS:(g + 1) * TS, :] for g in range(GROUP)]
            psum = p[0:TS]
            for g in range(1, GROUP):
                psum = psum + p[g * TS:(g + 1) * TS]
            p_hi, p_lo = _split_hi_lo(psum)
            imp = _dot(p_hi, cov) + _dot(p_lo, cov)
            imp = jnp.where(forced, BIG, jnp.where(valid, imp, NEG))
            imp = jnp.where(jj < n_sel, imp, -3e38)
            for k in range(SEL_TOP):
                best = jnp.max(imp, axis=-1, keepdims=True)
                pick = jnp.min(jnp.where(imp == best, jjf, float(SEL_LANES)), axis=-1, keepdims=True)
                idx_out = jnp.where(lane == h * SEL_TOP + k, pick.astype(jnp.int32), idx_out)
                imp = jnp.where(jjf == pick, -3e38, imp)
        ocmp_ref[bb] = jnp.concatenate(heads, axis=-1)
        idx_ref[bb] = idx_out


def _nsa_sample_select(q, ckv, past_len, nbatch=4):
    b = q.shape[0]
    n_cmp = ckv.shape[1]
    n_sel = -(-(past_len + 4) // SEL_BLOCK)
    cov = np.zeros((n_cmp, SEL_LANES), np.float32)
    cov[:, :n_sel] = _cover_matrix(n_cmp, n_sel)
    kern = functools.partial(_nsa_sample_select_kernel, nbatch=nbatch, past_len=past_len, n_sel=n_sel)
    return pl.pallas_call(
        kern,
        grid=(b // nbatch,),
        in_specs=[pl.BlockSpec((nbatch, TS, NSA_WIDTH), lambda i: (i, 0, 0)),
                  pl.BlockSpec((nbatch, n_cmp, KV_WIDTH), lambda i: (i, 0, 0)),
                  pl.BlockSpec((n_cmp, SEL_LANES), lambda i: (0, 0))],
        out_specs=[pl.BlockSpec((nbatch, TS, NSA_WIDTH), lambda i: (i, 0, 0)),
                   pl.BlockSpec((nbatch, TS, LANES), lambda i: (i, 0, 0))],
        out_shape=[jax.ShapeDtypeStruct((b, TS, NSA_WIDTH), F32),
                   jax.ShapeDtypeStruct((b, TS, LANES), jnp.int32)],
        compiler_params=_cparams("parallel"),
        name="nsa_sample_select",
    )(q, ckv, jnp.asarray(cov))


def _nsa_sample_attend_kernel(idx_ref, pt_ref, q_ref, ocmp_ref, gates_ref, knew_ref, wbuf_ref, wnew_ref,
                              pool_ref, o_ref, buf_ref, sem, *, t_dec, past_len):
    b = pl.program_id(0)
    last_blk = past_len // SEL_BLOCK
    per_page = PAGE // SEL_BLOCK

    nb = pl.num_programs(0)
    slot = b % 2

    def block_of(t, h, k, bb=None):
        bb = b if bb is None else bb
        return idx_ref[((bb * t_dec + t) * KV_HEADS + h) * SEL_TOP + k]

    def page_copies(bb, sl):
        out = []
        for t in range(t_dec):
            for h in range(KV_HEADS):
                for k in range(SEL_TOP):
                    page = pt_ref[bb, jnp.minimum(block_of(t, h, k, bb), last_blk - 1) // per_page]
                    out.append(pltpu.make_async_copy(pool_ref.at[page, h], buf_ref.at[sl, t, h, k], sem.at[sl]))
        return out

    @pl.when(b == 0)
    def _():
        for cp in page_copies(b, slot):
            cp.start()

    @pl.when(b + 1 < nb)
    def _():
        for cp in page_copies(b + 1, 1 - slot):
            cp.start()

    for cp in page_copies(b, slot):
        cp.wait()

    def rows_to_t(x):
        return jnp.concatenate([x, jnp.zeros((PAGE - TS, 2 * HEAD_DIM), F32)], axis=0).T

    rowg = lax.broadcasted_iota(jnp.int32, (SUBLANES, 1), 0)
    slope_col = [jnp.zeros((SUBLANES, 1), F32) for _ in range(KV_HEADS)]
    for h in range(KV_HEADS):
        for g in range(GROUP):
            slope_col[h] = jnp.where(rowg == g, SLOPES[h * GROUP + g], slope_col[h])
    n_tok = SEL_TOP * PAGE
    lane = lax.broadcasted_iota(jnp.int32, (1, n_tok), 1)
    row = lane % PAGE
    row_half = row // SEL_BLOCK
    slot_of = lane // PAGE
    n_win = wbuf_ref.shape[-1]
    wlane = lax.broadcasted_iota(jnp.int32, (1, n_win + PAGE), 1)
    wpos = jnp.where(wlane < n_win, past_len - n_win + wlane, past_len + wlane - n_win)

    for h in range(KV_HEADS):
        hs = slice(h * 2 * HEAD_DIM, (h + 1) * 2 * HEAD_DIM)
        new_t = rows_to_t(knew_ref[0, :, hs])
        wnew_t = rows_to_t(wnew_ref[0, :, hs])
        kw_t = jnp.concatenate([wbuf_ref[0, h, 0], wnew_t[:HEAD_DIM]], axis=1).astype(BF16)
        vw_t = jnp.concatenate([wbuf_ref[0, h, 1], wnew_t[HEAD_DIM:]], axis=1).astype(BF16)
        for t in range(t_dec):
            qpos = past_len + t
            qh = (q_ref[0, t, h] * SCALE).astype(BF16)
            k_tiles, v_tiles = [], []
            tok = jnp.zeros((1, n_tok), jnp.int32)
            want_half = jnp.zeros((1, n_tok), jnp.int32)
            for k in range(SEL_TOP):
                blk = block_of(t, h, k)
                is_new = blk >= last_blk
                k_tiles.append(jnp.where(is_new, new_t[:HEAD_DIM], buf_ref[slot, t, h, k, 0]))
                v_tiles.append(jnp.where(is_new, new_t[HEAD_DIM:], buf_ref[slot, t, h, k, 1]))
                tok = jnp.where(slot_of == k, (blk // per_page) * PAGE + row, tok)
                want_half = jnp.where(slot_of == k, blk % per_page, want_half)
            k_t = jnp.concatenate(k_tiles, axis=1).astype(BF16)
            v_t = jnp.concatenate(v_tiles, axis=1).astype(BF16)
            d = qpos - tok
            msk = jnp.where(d >= 0, jnp.where(row_half == want_half, 1.0, 0.0), 0.0)
            s = _dot(qh, k_t) - slope_col[h] * d.astype(F32)
            s = jnp.where(msk > 0, s, NEG)
            p = jnp.exp(s - jnp.max(s, axis=-1, keepdims=True)) * msk
            p = p / jnp.maximum(jnp.sum(p, axis=-1, keepdims=True), 1e-30)
            o_slc = _dot_nt(p.astype(BF16), v_t)
            dw = qpos - wpos
            mw = jnp.where(dw >= 0, jnp.where(dw < WINDOW, 1.0, 0.0), 0.0)
            sw = _dot(qh, kw_t) - slope_col[h] * dw.astype(F32)
            sw = jnp.where(mw > 0, sw, NEG)
            pw = jnp.exp(sw - jnp.max(sw, axis=-1, keepdims=True)) * mw
            pw = pw / jnp.maximum(jnp.sum(pw, axis=-1, keepdims=True), 1e-30)
            o_win = _dot_nt(pw.astype(BF16), vw_t)
            gts = jax.nn.sigmoid(gates_ref[0, t, h])
            o_ref[0, t, h] = (gts[:, 0:1] * ocmp_ref[0, t, h] + gts[:, 1:2] * o_slc + gts[:, 2:3] * o_win)


def _nsa_sample_attend(idx, page_table, q5, ocmp5, gates5, kv_slc_new, win_buf_t, kv_win_new, pool_slc_t,
                       t_dec, past_len):
    b = q5.shape[0]
    n_win = win_buf_t.shape[-1]
    blk5 = (1, t_dec, KV_HEADS, SUBLANES, HEAD_DIM)
    grid_spec = pltpu.PrefetchScalarGridSpec(
        num_scalar_prefetch=2,
        grid=(b,),
        in_specs=[pl.BlockSpec(blk5, lambda i, *_: (i, 0, 0, 0, 0)),
                  pl.BlockSpec(blk5, lambda i, *_: (i, 0, 0, 0, 0)),
                  pl.BlockSpec((1, t_dec, KV_HEADS, SUBLANES, LANES), lambda i, *_: (i, 0, 0, 0, 0)),
                  pl.BlockSpec((1, TS, KV_WIDTH), lambda i, *_: (i, 0, 0)),
                  pl.BlockSpec((1, KV_HEADS, 2, HEAD_DIM, n_win), lambda i, *_: (i, 0, 0, 0, 0)),
                  pl.BlockSpec((1, TS, KV_WIDTH), lambda i, *_: (i, 0, 0)),
                  pl.BlockSpec(memory_space=pl.ANY)],
        out_specs=pl.BlockSpec(blk5, lambda i, *_: (i, 0, 0, 0, 0)),
        scratch_shapes=[pltpu.VMEM((2, t_dec, KV_HEADS, SEL_TOP, 2, HEAD_DIM, PAGE), F32),
                        pltpu.SemaphoreType.DMA((2,))],
    )
    kern = functools.partial(_nsa_sample_attend_kernel, t_dec=t_dec, past_len=past_len)
    return pl.pallas_call(
        kern,
        grid_spec=grid_spec,
        out_shape=jax.ShapeDtypeStruct((b, t_dec, KV_HEADS, SUBLANES, HEAD_DIM), F32),
        compiler_params=_cparams("arbitrary"),
        name="nsa_sample_attend",
    )(idx[:, :t_dec, :KV_HEADS * SEL_TOP].reshape(-1), page_table, q5, ocmp5, gates5, kv_slc_new, win_buf_t,
      kv_win_new, pool_slc_t)


NSA_SPLITS = ((0, NSA_WIDTH), (NSA_WIDTH, KV_WIDTH), (NSA_WIDTH + KV_WIDTH, KV_WIDTH),
              (NSA_WIDTH + 2 * KV_WIDTH, KV_WIDTH), (NSA_WIDTH + 3 * KV_WIDTH, MEM_WIDTH),
              (NSA_WIDTH + 3 * KV_WIDTH + MEM_WIDTH, LANES))


def _nsa_in_weight(w_in):
    o = NSA_WIDTH + 3 * KV_WIDTH
    n_gate = 3 * NSA_HEADS
    gates = jnp.pad(w_in[:, o:o + n_gate], ((0, 0), (0, LANES - n_gate)))
    return jnp.concatenate([w_in[:, :o], w_in[:, o + n_gate:], gates], axis=1).astype(BF16)


def _to_heads5(x, t_dec):
    b = x.shape[0]
    w = x.shape[-1] // NSA_HEADS
    x = x[:, :t_dec].reshape(b, t_dec, KV_HEADS, GROUP, w)
    return jnp.pad(x, ((0, 0), (0, 0), (0, 0), (0, SUBLANES - GROUP), (0, 0)))


def kernel(x_prompt, x_sample, mem_prompt, state_ssm_re, state_ssm_im, cache_cmp_kv, cache_slc_kv, cache_win_kv, cache_mem_kv, page_table, norm_mix_pre, norm_mix_post, norm_ffn_pre, norm_ffn_post, w_out, w_mem_kv, w_ffn_in, w_ffn_out, ssm_w_in, ssm_a_re, ssm_a_im, ssm_log_dt, ssm_b_re, ssm_b_im, ssm_c_re, ssm_c_im, ssm_d, ssm_w_glu, ssm_b_glu, nsa_w_in, nsa_cmp_pe, nsa_cmp_w1, nsa_cmp_b1, nsa_cmp_w2):
    bp, seq, d = x_prompt.shape
    bs, t_dec, _ = x_sample.shape
    n_mem = mem_prompt.shape[1]
    depth = w_out.shape[0]
    past_len = page_table.shape[1] * PAGE
    d_ff = w_ffn_out.shape[1]
    chunk = 16

    xp = x_prompt.reshape(bp * seq, d)
    xs = jnp.pad(x_sample, ((0, 0), (0, TS - t_dec), (0, 0))).reshape(bs * TS, d)
    mem2d = mem_prompt.reshape(bp * n_mem, d)

    outs = {k: [] for k in ('ssm_re_p', 'ssm_im_p', 'ssm_re_s', 'ssm_im_s', 'cmp_p', 'slc_p', 'win_p',
                            'cmp_s', 'slc_s', 'win_s', 'mem_p')}
    for i in range(depth):
        j = i // 2
        (mkv_p,) = _norm_proj(mem2d, norm_mix_pre[i], w_mem_kv[i].astype(BF16), ((0, 2 * MEM_WIDTH),),
                              do_norm=False)
        mkv_p = mkv_p.reshape(bp, n_mem, 2 * MEM_WIDTH)
        outs['mem_p'].append(mkv_p.reshape(bp, n_mem, MEM_HEADS, 2, HEAD_DIM))
        mkv_s_t = _kv_cache_t(cache_mem_kv[i])
        if i % 2 == 0:
            w_in = ssm_w_in[j].astype(BF16)
            wglu = ssm_w_glu[j].astype(BF16)
            ssm_w = (ssm_a_re[j], ssm_a_im[j], ssm_log_dt[j], ssm_b_re[j], ssm_b_im[j], ssm_c_re[j], ssm_c_im[j])
            ssm_split = ((0, SSM_WIDTH), (SSM_WIDTH, MEM_WIDTH))
            u, qm = _norm_proj(xp, norm_mix_pre[i], w_in, ssm_split)
            zero = jnp.zeros((bp, SSM_GROUPS, SSM_STATE), F32)
            y, hlr, hli = _s5_scan_octets(u.reshape(bp, seq, SSM_WIDTH), zero, zero,
                                          _s5_operators(*ssm_w, chunk), chunk)
            m1p = _s5_glu(y.reshape(bp * seq, SSM_WIDTH), u, ssm_d[j], wglu, ssm_b_glu[j])
            m2p = _mem_attn(qm.reshape(bp, seq, MEM_WIDTH), mkv_p).reshape(bp * seq, MEM_WIDTH)
            outs['ssm_re_p'].append(hlr)
            outs['ssm_im_p'].append(hli)
            u, qm = _norm_proj(xs, norm_mix_pre[i], w_in, ssm_split)
            y, hlr, hli = _s5_scan(u.reshape(bs, TS, SSM_WIDTH)[:, :t_dec], state_ssm_re[j].astype(F32),
                                   state_ssm_im[j].astype(F32), _s5_operators(*ssm_w, t_dec), t_dec)
            y = jnp.pad(y, ((0, 0), (0, TS - t_dec), (0, 0)))
            m1s = _s5_glu(y.reshape(bs * TS, SSM_WIDTH), u, ssm_d[j], wglu, ssm_b_glu[j])
            m2s = _mem_attn_t(qm.reshape(bs, TS, MEM_WIDTH), mkv_s_t).reshape(bs * TS, MEM_WIDTH)
            outs['ssm_re_s'].append(hlr)
            outs['ssm_im_s'].append(hli)
        else:
            w_in = _nsa_in_weight(nsa_w_in[j])
            wp, cbias, w2b = _cmp_weights(nsa_cmp_pe[j], nsa_cmp_w1[j], nsa_cmp_b1[j], nsa_cmp_w2[j])
            kv_shape = (KV_HEADS, 2, HEAD_DIM)
            q, kc, ks, kw, qm, gates = _norm_proj(xp, norm_mix_pre[i], w_in, NSA_SPLITS)
            kc3, ks3, kw3 = (a.reshape(bp, seq, KV_WIDTH) for a in (kc, ks, kw))
            ident = jnp.arange(bp * (seq // PAGE), dtype=jnp.int32).reshape(bp, seq // PAGE)
            ckv = _cmp_tokens(kc.reshape(bp * seq // PAGE, PAGE, KV_WIDTH), ident, wp, cbias, w2b, False)
            m1p = _nsa_prompt_attn(q.reshape(bp, seq, NSA_WIDTH), ckv, ks3, kw3,
                                   gates.reshape(bp, seq, LANES)).reshape(bp * seq, NSA_WIDTH)
            m2p = _mem_attn(qm.reshape(bp, seq, MEM_WIDTH), mkv_p).reshape(bp * seq, MEM_WIDTH)
            outs['cmp_p'].append(kc3.reshape((bp, seq) + kv_shape))
            outs['slc_p'].append(ks3.reshape((bp, seq) + kv_shape))
            outs['win_p'].append(kw3[:, seq - min(WINDOW, seq):].reshape((bp, min(WINDOW, seq)) + kv_shape))
            q, kc, ks, kw, qm, gates = _norm_proj(xs, norm_mix_pre[i], w_in, NSA_SPLITS)
            kc3, ks3, kw3 = (a.reshape(bs, TS, KV_WIDTH) for a in (kc, ks, kw))
            pool_cmp_t = _kv_cache_t(cache_cmp_kv[j])
            pool_cmp_t = pool_cmp_t.reshape(pool_cmp_t.shape[0], KV_HEADS, 2 * HEAD_DIM, PAGE)
            pool_slc_t = _kv_cache_t(cache_slc_kv[j])
            win_buf_t = _kv_cache_t(cache_win_kv[j])
            ckv = _cmp_tokens(pool_cmp_t, page_table, wp, cbias, w2b, True)
            q3 = q.reshape(bs, TS, NSA_WIDTH)
            ocmp, idx = _nsa_sample_select(q3, ckv, past_len)
            gates3 = gates.reshape(bs, TS, LANES)[:, :, :3 * NSA_HEADS].reshape(bs, TS, 3, NSA_HEADS)
            gates5 = _to_heads5(gates3.transpose(0, 1, 3, 2).reshape(bs, TS, NSA_HEADS * 3), t_dec)
            gates5 = jnp.pad(gates5, ((0, 0),) * 4 + ((0, LANES - 3),))
            o5 = _nsa_sample_attend(idx, page_table, _to_heads5(q3, t_dec), _to_heads5(ocmp, t_dec), gates5,
                                    ks3, win_buf_t, kw3, pool_slc_t, t_dec, past_len)
            o = o5[:, :, :, :GROUP].reshape(bs, t_dec, NSA_WIDTH)
            m1s = jnp.pad(o, ((0, 0), (0, TS - t_dec), (0, 0))).reshape(bs * TS, NSA_WIDTH)
            m2s = _mem_attn_t(qm.reshape(bs, TS, MEM_WIDTH), mkv_s_t).reshape(bs * TS, MEM_WIDTH)
            outs['cmp_s'].append(kc3[:, :t_dec].reshape((bs, t_dec) + kv_shape))
            outs['slc_s'].append(ks3[:, :t_dec].reshape((bs, t_dec) + kv_shape))
            kv_w = jnp.concatenate([cache_win_kv[j], kw3[:, :t_dec].reshape((bs, t_dec) + kv_shape)], axis=1)
            n_keep = min(WINDOW, past_len + t_dec)
            outs['win_s'].append(kv_w[:, kv_w.shape[1] - n_keep:])
        wo = w_out[i].astype(BF16)
        wg = w_ffn_in[i][:, :d_ff].astype(BF16)
        wu = w_ffn_in[i][:, d_ff:].astype(BF16)
        wd = w_ffn_out[i].astype(BF16)
        w1 = m1p.shape[1]
        args = (wo[:w1], wo[w1:], norm_mix_post[i], norm_ffn_pre[i], wg, wu, wd, norm_ffn_post[i])
        xp = _post(xp, m1p, m2p, *args)
        xs = _post(xs, m1s, m2s, *args)

    st = lambda k: jnp.stack(outs[k])
    y_sample = xs.reshape(bs, TS, d)[:, :t_dec]
    return (xp.reshape(bp, seq, d), y_sample, st('ssm_re_p'), st('ssm_im_p'), st('ssm_re_s'), st('ssm_im_s'),
            st('cmp_p'), st('slc_p'), st('win_p'), st('cmp_s'), st('slc_s'), st('win_s'), st('mem_p'))
```

```python
import functools
import math

import numpy as np
import jax
import jax.numpy as jnp
from jax import lax
from jax.experimental import pallas as pl
from jax.experimental.pallas import tpu as pltpu

F32 = jnp.float32
BF16 = jnp.bfloat16

D_MODEL = 1024
PAGE = 128
MEM_HEADS = 4
HEAD_DIM = 64
MEM_WIDTH = MEM_HEADS * HEAD_DIM
SSM_WIDTH = D_MODEL - MEM_WIDTH
SSM_GROUP = 16
SSM_GROUPS = SSM_WIDTH // SSM_GROUP
SSM_STATE = 64
NSA_HEADS = 12
KV_HEADS = 3
GROUP = NSA_HEADS // KV_HEADS
NSA_WIDTH = NSA_HEADS * HEAD_DIM
KV_WIDTH = KV_HEADS * 2 * HEAD_DIM
CMP_BLOCK = 32
CMP_STRIDE = 16
CMP_HIDDEN = 2 * HEAD_DIM
SEL_BLOCK = 64
SEL_TOP = 16
WINDOW = 512
RMS_EPS = 1e-6
NEG = -1e30
BIG = 1e30
SCALE = HEAD_DIM ** -0.5
LANES = 128
SUBLANES = 8
VMEM_LIMIT = 56 * 1024 * 1024
SLOPES = [2.0 ** (-8.0 * (h + 1) / NSA_HEADS) for h in range(NSA_HEADS)]
LOG2E = math.log2(math.e)


def _cparams(*sem):
    return pltpu.CompilerParams(dimension_semantics=sem, vmem_limit_bytes=VMEM_LIMIT)


def _rms(x, g):
    return x * lax.rsqrt(jnp.mean(x * x, axis=-1, keepdims=True) + RMS_EPS) * g


def _gelu(x):
    return 0.5 * x * (1.0 + jnp.tanh(math.sqrt(2.0 / math.pi) * (x + 0.044715 * (x * x * x))))


def _dot(a, b):
    return jnp.dot(a, b, preferred_element_type=F32)


def _dot_nt(a, b):
    return lax.dot_general(a, b, (((1,), (1,)), ((), ())), preferred_element_type=F32)


def _kv_cache_t(cache):
    n = cache.ndim
    return jnp.moveaxis(cache, n - 4, n - 1)


def _norm_proj_kernel(x_ref, g_ref, w_ref, *o_refs, splits, t_splits, do_norm):
    x = x_ref[...]
    if do_norm:
        x = _rms(x, g_ref[...])
    z = _dot(x.astype(BF16), w_ref[...])
    for (start, width), o in zip(splits, o_refs):
        o[...] = z[:, start:start + width]
    for (start, width), o in zip(t_splits, o_refs[len(splits):]):
        o[0] = z[:, start:start + width].T


def _norm_proj(x, g, w, splits, do_norm=True, tm=256, t_splits=(), seq=None):
    rows, d = x.shape
    n = w.shape[1]
    tm = min(tm, rows)
    kern = functools.partial(_norm_proj_kernel, splits=tuple(splits), t_splits=tuple(t_splits), do_norm=do_norm)
    out_specs = [pl.BlockSpec((tm, wd), lambda i: (i, 0)) for _, wd in splits]
    out_shape = [jax.ShapeDtypeStruct((rows, wd), F32) for _, wd in splits]
    if t_splits:
        per_seq = seq // tm
        out_specs += [pl.BlockSpec((1, wd, tm), lambda i: (i // per_seq, 0, i % per_seq)) for _, wd in t_splits]
        out_shape += [jax.ShapeDtypeStruct((rows // seq, wd, seq), F32) for _, wd in t_splits]
    return pl.pallas_call(
        kern,
        grid=(rows // tm,),
        in_specs=[pl.BlockSpec((tm, d), lambda i: (i, 0)),
                  pl.BlockSpec((1, d), lambda i: (0, 0)),
                  pl.BlockSpec((d, n), lambda i: (0, 0))],
        out_specs=out_specs,
        out_shape=out_shape,
        compiler_params=_cparams("parallel"),
        name="norm_proj",
    )(x, g.reshape(1, d), w)


def _softmax_rows(s):
    p = jnp.exp(s - jnp.max(s, axis=-1, keepdims=True))
    return p, jnp.sum(p, axis=-1, keepdims=True)


def _mem_attn_kernel(q_ref, kv_ref, o_ref):
    q = q_ref[0]
    kv = kv_ref[0]
    outs = []
    for h in range(MEM_HEADS):
        qh = (q[:, h * HEAD_DIM:(h + 1) * HEAD_DIM] * SCALE).astype(BF16)
        k = kv[:, h * 2 * HEAD_DIM:h * 2 * HEAD_DIM + HEAD_DIM].astype(BF16)
        v = kv[:, h * 2 * HEAD_DIM + HEAD_DIM:(h + 1) * 2 * HEAD_DIM].astype(BF16)
        p, l = _softmax_rows(_dot_nt(qh, k))
        outs.append(_dot(p.astype(BF16), v) / l)
    o_ref[0] = jnp.concatenate(outs, axis=-1)


def _mem_attn(qm, mem_kv, tm=256):
    b, t, _ = qm.shape
    tm = min(tm, t)
    n_mem = mem_kv.shape[1]
    return pl.pallas_call(
        _mem_attn_kernel,
        grid=(b, t // tm),
        in_specs=[pl.BlockSpec((1, tm, MEM_WIDTH), lambda i, j: (i, j, 0)),
                  pl.BlockSpec((1, n_mem, 2 * MEM_WIDTH), lambda i, j: (i, 0, 0))],
        out_specs=pl.BlockSpec((1, tm, MEM_WIDTH), lambda i, j: (i, j, 0)),
        out_shape=jax.ShapeDtypeStruct((b, t, MEM_WIDTH), F32),
        compiler_params=_cparams("parallel", "parallel"),
        name="mem_attn",
    )(qm, mem_kv)


def _mem_attn_t_kernel(q_ref, kv_ref, o_ref):
    q = q_ref[0]
    outs = []
    for h in range(MEM_HEADS):
        qh = (q[:, h * HEAD_DIM:(h + 1) * HEAD_DIM] * SCALE).astype(BF16)
        p, l = _softmax_rows(_dot(qh, kv_ref[0, h, 0].astype(BF16)))
        outs.append(_dot_nt(p.astype(BF16), kv_ref[0, h, 1].astype(BF16)) / l)
    o_ref[0] = jnp.concatenate(outs, axis=-1)


def _mem_attn_t(qm, mem_kv_t):
    b, t, _ = qm.shape
    n_mem = mem_kv_t.shape[-1]
    return pl.pallas_call(
        _mem_attn_t_kernel,
        grid=(b,),
        in_specs=[pl.BlockSpec((1, t, MEM_WIDTH), lambda i: (i, 0, 0)),
                  pl.BlockSpec((1, MEM_HEADS, 2, HEAD_DIM, n_mem), lambda i: (i, 0, 0, 0, 0))],
        out_specs=pl.BlockSpec((1, t, MEM_WIDTH), lambda i: (i, 0, 0)),
        out_shape=jax.ShapeDtypeStruct((b, t, MEM_WIDTH), F32),
        compiler_params=_cparams("parallel"),
        name="mem_attn_t",
    )(qm, mem_kv_t)


def _post_kernel(x_ref, m1_ref, m2_ref, wo1_ref, wo2_ref, g1_ref, g2_ref, wg_ref, wu_ref, wd_ref, g3_ref,
                 o_ref):
    a = _dot(m1_ref[...].astype(BF16), wo1_ref[...]) + _dot(m2_ref[...].astype(BF16), wo2_ref[...])
    x1 = x_ref[...] + _rms(a, g1_ref[...])
    h = _rms(x1, g2_ref[...]).astype(BF16)
    gate = _dot(h, wg_ref[...])
    up = _dot(h, wu_ref[...])
    act = (gate * jax.nn.sigmoid(gate) * up).astype(BF16)
    f = _dot(act, wd_ref[...])
    o_ref[...] = x1 + _rms(f, g3_ref[...])


def _post(x, m1, m2, wo1, wo2, g1, g2, wg, wu, wd, g3, tm=512):
    rows, d = x.shape
    tm = min(tm, rows)
    w1, w2 = m1.shape[1], m2.shape[1]
    dff = wg.shape[1]

    def const(shape):
        return pl.BlockSpec(shape, lambda i: (0, 0), pipeline_mode=pl.Buffered(1))

    return pl.pallas_call(
        _post_kernel,
        grid=(rows // tm,),
        in_specs=[pl.BlockSpec((tm, d), lambda i: (i, 0)),
                  pl.BlockSpec((tm, w1), lambda i: (i, 0)),
                  pl.BlockSpec((tm, w2), lambda i: (i, 0)),
                  const((w1, d)), const((w2, d)), const((1, d)), const((1, d)),
                  const((d, dff)), const((d, dff)), const((dff, d)), const((1, d))],
        out_specs=pl.BlockSpec((tm, d), lambda i: (i, 0)),
        out_shape=jax.ShapeDtypeStruct((rows, d), F32),
        compiler_params=_cparams("parallel"),
        name="post_ffn",
    )(x, m1, m2, wo1, wo2, g1.reshape(1, d), g2.reshape(1, d), wg, wu, wd, g3.reshape(1, d))


def _s5_operators(a_re, a_im, log_dt, b_re, b_im, c_re, c_im, q):
    hp = lax.Precision.HIGHEST
    a_re, a_im = a_re.astype(F32), a_im.astype(F32)
    dt = jnp.exp(log_dt.astype(F32))[:, None]
    mag = jnp.exp(a_re * dt)
    ab_r, ab_i = mag * jnp.cos(a_im * dt), mag * jnp.sin(a_im * dt)
    den = a_re * a_re + a_im * a_im
    nr, ni = ab_r - 1.0, ab_i
    f_r = (nr * a_re + ni * a_im) / den
    f_i = (ni * a_re - nr * a_im) / den
    bb_r = f_r[..., None] * b_re - f_i[..., None] * b_im
    bb_i = f_r[..., None] * b_im + f_i[..., None] * b_re
    pw_r, pw_i = [jnp.ones_like(ab_r)], [jnp.zeros_like(ab_i)]
    for _ in range(q):
        r, i = pw_r[-1], pw_i[-1]
        pw_r.append(r * ab_r - i * ab_i)
        pw_i.append(r * ab_i + i * ab_r)
    pw_r, pw_i = jnp.stack(pw_r), jnp.stack(pw_i)
    w_r = pw_r[:, :, :, None] * bb_r[None] - pw_i[:, :, :, None] * bb_i[None]
    w_i = pw_r[:, :, :, None] * bb_i[None] + pw_i[:, :, :, None] * bb_r[None]
    kk = (jnp.einsum('gip,tgpj->tgij', c_re, w_r, precision=hp)
          - jnp.einsum('gip,tgpj->tgij', c_im, w_i, precision=hp))
    tau = jnp.arange(q)[None, :] - jnp.arange(q)[:, None]
    kt = kk[jnp.clip(tau, 0, q)]
    kt = jnp.where((tau >= 0)[:, :, None, None, None], kt, 0.0)
    g = a_re.shape[0]
    toep = kt.transpose(2, 0, 4, 1, 3).reshape(g, q * SSM_GROUP, q * SSM_GROUP)
    rev = q - 1 - jnp.arange(q)
    bst_r = w_r[rev].transpose(1, 0, 3, 2).reshape(g, q * SSM_GROUP, SSM_STATE)
    bst_i = w_i[rev].transpose(1, 0, 3, 2).reshape(g, q * SSM_GROUP, SSM_STATE)
    ar, ai = pw_r[1:], pw_i[1:]
    co_r = c_re[None] * ar[:, :, None, :] - c_im[None] * ai[:, :, None, :]
    co_i = -c_re[None] * ai[:, :, None, :] - c_im[None] * ar[:, :, None, :]
    cout_r = co_r.transpose(1, 3, 0, 2).reshape(g, SSM_STATE, q * SSM_GROUP)
    cout_i = co_i.transpose(1, 3, 0, 2).reshape(g, SSM_STATE, q * SSM_GROUP)
    return (toep.astype(BF16), bst_r.astype(BF16), bst_i.astype(BF16), cout_r.astype(BF16),
            cout_i.astype(BF16), pw_r[q][:, None, :], pw_i[q][:, None, :])


def _s5_scan_kernel(u_ref, toep_ref, bsr_ref, bsi_ref, cor_ref, coi_ref, aqr_ref, aqi_ref, h0r_ref, h0i_ref,
                    y_ref, hlr_ref, hli_ref, sr_ref, si_ref, hpr_ref, hpi_ref, *, gs, nc, nb):
    for g in range(gs):
        u = u_ref[g].astype(BF16)
        sr_ref[g] = _dot(u, bsr_ref[g])
        si_ref[g] = _dot(u, bsi_ref[g])

    def step(c, carry):
        rows = pl.ds(pl.multiple_of(c * nb, SUBLANES), nb)
        new = []
        for g in range(gs):
            hr, hi = carry[2 * g], carry[2 * g + 1]
            hpr_ref[g, rows, :] = hr
            hpi_ref[g, rows, :] = hi
            ar, ai = aqr_ref[g], aqi_ref[g]
            new.append(ar * hr - ai * hi + sr_ref[g, rows, :])
            new.append(ar * hi + ai * hr + si_ref[g, rows, :])
        return tuple(new)

    init = []
    for g in range(gs):
        init += [h0r_ref[g], h0i_ref[g]]
    fin = lax.fori_loop(0, nc, step, tuple(init))
    for g in range(gs):
        hlr_ref[g] = fin[2 * g]
        hli_ref[g] = fin[2 * g + 1]
        u = u_ref[g].astype(BF16)
        y_ref[g] = (_dot(u, toep_ref[g]) + _dot(hpr_ref[g].astype(BF16), cor_ref[g])
                    + _dot(hpi_ref[g].astype(BF16), coi_ref[g]))


def _s5_scan(u, h0r, h0i, ops, q, gs=4):
    b, l, _ = u.shape
    nc = l // q
    qw = q * SSM_GROUP
    toep, bsr, bsi, cor, coi, aqr, aqi = ops
    ug = u.reshape(b, nc, q, SSM_GROUPS, SSM_GROUP).transpose(3, 1, 0, 2, 4).reshape(SSM_GROUPS, nc * b, qw)
    h0r_g = h0r.transpose(1, 0, 2)
    h0i_g = h0i.transpose(1, 0, 2)
    rows = nc * b

    def gspec(r, c):
        return pl.BlockSpec((gs, r, c), lambda i: (i, 0, 0))

    kern = functools.partial(_s5_scan_kernel, gs=gs, nc=nc, nb=b)
    y, hlr, hli = pl.pallas_call(
        kern,
        grid=(SSM_GROUPS // gs,),
        in_specs=[gspec(rows, qw), gspec(qw, qw), gspec(qw, SSM_STATE), gspec(qw, SSM_STATE),
                  gspec(SSM_STATE, qw), gspec(SSM_STATE, qw), gspec(1, SSM_STATE), gspec(1, SSM_STATE),
                  gspec(b, SSM_STATE), gspec(b, SSM_STATE)],
        out_specs=[gspec(rows, qw), gspec(b, SSM_STATE), gspec(b, SSM_STATE)],
        out_shape=[jax.ShapeDtypeStruct((SSM_GROUPS, rows, qw), F32),
                   jax.ShapeDtypeStruct((SSM_GROUPS, b, SSM_STATE), F32),
                   jax.ShapeDtypeStruct((SSM_GROUPS, b, SSM_STATE), F32)],
        scratch_shapes=[pltpu.VMEM((gs, rows, SSM_STATE), F32) for _ in range(4)],
        compiler_params=_cparams("parallel"),
        name="s5_scan_groups",
    )(ug, toep, bsr, bsi, cor, coi, aqr, aqi, h0r_g, h0i_g)
    y = y.reshape(SSM_GROUPS, nc, b, q, SSM_GROUP).transpose(2, 1, 3, 0, 4).reshape(b, l, SSM_WIDTH)
    return y, hlr.transpose(1, 0, 2), hli.transpose(1, 0, 2)


OCT = LANES // SSM_GROUP
N_OCT = SSM_GROUPS // OCT
OCT_STATE = OCT * SSM_STATE


def _s5_octet_operators(ops, q):
    toep, bsr, bsi, cor, coi, aqr, aqi = ops
    eye = jnp.eye(OCT, dtype=toep.dtype)
    qq = q * LANES
    lag = toep.reshape(N_OCT, OCT, q, SSM_GROUP, q, SSM_GROUP)[:, :, 0]
    k8 = jnp.einsum('ogjti,gh->otgjhi', lag, eye).reshape(N_OCT, q, LANES, LANES)

    def bst8(b):
        b8 = jnp.einsum('ogsjp,gh->osgjhp', b.reshape(N_OCT, OCT, q, SSM_GROUP, SSM_STATE), eye)
        return b8.reshape(N_OCT, qq, OCT_STATE)

    def cout8(c):
        c8 = jnp.einsum('ogpti,gh->ogpthi', c.reshape(N_OCT, OCT, SSM_STATE, q, SSM_GROUP), eye)
        return c8.reshape(N_OCT, OCT_STATE, qq)

    return (k8, bst8(bsr), bst8(bsi), cout8(cor), cout8(coi),
            aqr.reshape(N_OCT, 1, OCT_STATE), aqi.reshape(N_OCT, 1, OCT_STATE))


def _s5_octet_kernel(x_ref, lag_ref, bsr_ref, bsi_ref, cor_ref, coi_ref, aqr_ref, aqi_ref, h0r_ref, h0i_ref,
                     y_ref, hlr_ref, hli_ref, sr_ref, si_ref, hpr_ref, hpi_ref, hr_ref, hi_ref, toep_ref,
                     *, ncb, nb, q):
    @pl.when(pl.program_id(1) == 0)
    def _():
        hr_ref[...] = h0r_ref[0]
        hi_ref[...] = h0i_ref[0]
        zero = jnp.zeros((LANES, LANES), BF16)
        for s in range(q):
            for t in range(q):
                toep_ref[s * LANES:(s + 1) * LANES, t * LANES:(t + 1) * LANES] = (
                    lag_ref[0, t - s] if t >= s else zero)

    x = x_ref[0]
    sr_ref[...] = _dot(x, bsr_ref[0])
    si_ref[...] = _dot(x, bsi_ref[0])
    ar, ai = aqr_ref[0], aqi_ref[0]

    def step(c, carry):
        hr, hi = carry
        rows = pl.ds(pl.multiple_of(c * nb, SUBLANES), nb)
        hpr_ref[rows, :] = hr
        hpi_ref[rows, :] = hi
        return ar * hr - ai * hi + sr_ref[rows, :], ar * hi + ai * hr + si_ref[rows, :]

    hr, hi = lax.fori_loop(0, ncb, step, (hr_ref[...], hi_ref[...]))
    hr_ref[...] = hr
    hi_ref[...] = hi
    hlr_ref[0] = hr
    hli_ref[0] = hi
    y_ref[0] = (_dot(x, toep_ref[...]) + _dot(hpr_ref[...].astype(BF16), cor_ref[0])
                + _dot(hpi_ref[...].astype(BF16), coi_ref[0]))


def _s5_scan_octets(u, h0r, h0i, ops, q, row_blocks=4):
    b, l, _ = u.shape
    nc = l // q
    qq = q * LANES
    rows = nc * b
    rb = rows // row_blocks
    ncb = nc // row_blocks
    ops8 = _s5_octet_operators(ops, q)
    x = u.reshape(b, nc, q, N_OCT, LANES).transpose(3, 1, 0, 2, 4).reshape(N_OCT, rows, qq).astype(BF16)
    h0r8 = h0r.reshape(b, N_OCT, OCT_STATE).transpose(1, 0, 2)
    h0i8 = h0i.reshape(b, N_OCT, OCT_STATE).transpose(1, 0, 2)

    def wspec(r, c):
        return pl.BlockSpec((1, r, c), lambda o, i: (o, 0, 0))

    kern = functools.partial(_s5_octet_kernel, ncb=ncb, nb=b, q=q)
    y, hlr, hli = pl.pallas_call(
        kern,
        grid=(N_OCT, row_blocks),
        in_specs=[pl.BlockSpec((1, rb, qq), lambda o, i: (o, i, 0)),
                  pl.BlockSpec((1, q, LANES, LANES), lambda o, i: (o, 0, 0, 0)),
                  wspec(qq, OCT_STATE), wspec(qq, OCT_STATE), wspec(OCT_STATE, qq),
                  wspec(OCT_STATE, qq), wspec(1, OCT_STATE), wspec(1, OCT_STATE),
                  wspec(b, OCT_STATE), wspec(b, OCT_STATE)],
        out_specs=[pl.BlockSpec((1, rb, qq), lambda o, i: (o, i, 0)), wspec(b, OCT_STATE), wspec(b, OCT_STATE)],
        out_shape=[jax.ShapeDtypeStruct((N_OCT, rows, qq), F32),
                   jax.ShapeDtypeStruct((N_OCT, b, OCT_STATE), F32),
                   jax.ShapeDtypeStruct((N_OCT, b, OCT_STATE), F32)],
        scratch_shapes=[pltpu.VMEM((rb, OCT_STATE), F32) for _ in range(4)]
        + [pltpu.VMEM((b, OCT_STATE), F32) for _ in range(2)] + [pltpu.VMEM((qq, qq), BF16)],
        compiler_params=_cparams("parallel", "arbitrary"),
        name="s5_scan_octets",
    )(x, *ops8, h0r8, h0i8)
    y = y.reshape(N_OCT, nc, b, q, LANES).transpose(2, 1, 3, 0, 4).reshape(b, l, SSM_WIDTH)
    unpack = lambda h: h.transpose(1, 0, 2).reshape(b, SSM_GROUPS, SSM_STATE)
    return y, unpack(hlr), unpack(hli)


def _s5_glu_kernel(y_ref, u_ref, d_ref, w_ref, b_ref, o_ref):
    v = _gelu(y_ref[...] + d_ref[...] * u_ref[...])
    o_ref[...] = v * jax.nn.sigmoid(_dot(v.astype(BF16), w_ref[...]) + b_ref[...])


def _s5_glu(y, u, d, w, bias, tm=256):
    rows, n = y.shape
    tm = min(tm, rows)
    return pl.pallas_call(
        _s5_glu_kernel,
        grid=(rows // tm,),
        in_specs=[pl.BlockSpec((tm, n), lambda i: (i, 0)), pl.BlockSpec((tm, n), lambda i: (i, 0)),
                  pl.BlockSpec((1, n), lambda i: (0, 0)), pl.BlockSpec((n, n), lambda i: (0, 0)),
                  pl.BlockSpec((1, n), lambda i: (0, 0))],
        out_specs=pl.BlockSpec((tm, n), lambda i: (i, 0)),
        out_shape=jax.ShapeDtypeStruct((rows, n), F32),
        compiler_params=_cparams("parallel"),
        name="s5_glu",
    )(y, u, d.reshape(1, n), w, bias.reshape(1, n))


CMP_PAGES = 16
CMP_GROUPS = 2
SUBS = PAGE // CMP_STRIDE


def _cmp_weights(pe, w1, b1, w2):
    w = w1.reshape(2, CMP_STRIDE // 2, 2, 2, HEAD_DIM, CMP_HIDDEN)
    eye = jnp.eye(2, dtype=w1.dtype)
    wp = jnp.einsum('apjcef,cd->pjcedaf', w, eye)
    wp = wp.reshape(CMP_STRIDE // 2, 2 * 2 * HEAD_DIM, 2 * 2 * CMP_HIDDEN)
    bias = (jnp.einsum('jce,jcef->cf', pe, w1, precision=lax.Precision.HIGHEST) + b1).reshape(1, 2 * CMP_HIDDEN)
    w2b = jnp.einsum('cfe,cd->cfde', w2, eye).reshape(2 * CMP_HIDDEN, 2 * HEAD_DIM)
    return wp.astype(BF16), bias.astype(F32), w2b.astype(BF16)


def _cmp_tokens_kernel(pt_ref, *refs, transposed):
    n = CMP_PAGES * SUBS
    if transposed:
        nblk = CMP_PAGES
        page_refs = refs[:nblk]
        wp_ref, bias_ref, w2_ref, o_ref, carry_ref = refs[nblk:nblk + 5]
        xs_refs = refs[nblk + 5:]
        per_group = CMP_PAGES * KV_HEADS // CMP_GROUPS

        def stage(pages):
            for k in pages:
                for h in range(KV_HEADS):
                    i = k * KV_HEADS + h
                    xs_refs[i // per_group][i % per_group] = page_refs[k][0, h].T

        def rows_of(i, j):
            return xs_refs[i // per_group][i % per_group, pl.ds(j, SUBS, stride=CMP_STRIDE), :]
    else:
        nblk = CMP_PAGES * KV_HEADS
        page_refs = refs[:nblk]
        wp_ref, bias_ref, w2_ref, o_ref, carry_ref = refs[nblk:]

        def rows_of(i, j):
            return page_refs[i][0, pl.ds(j, SUBS, stride=CMP_STRIDE), :]

    @pl.when(pl.program_id(1) == 0)
    def _():
        carry_ref[...] = jnp.zeros_like(carry_ref)

    groups = CMP_GROUPS if transposed else 1
    pages_per = CMP_PAGES // groups
    ng = pages_per * SUBS
    accs = []
    for gi in range(groups):
        pages = range(gi * pages_per, (gi + 1) * pages_per)
        if transposed:
            stage(pages)
        acc = None
        for jp in range(CMP_STRIDE // 2):
            halves = []
            for jj in range(2):
                j = 2 * jp + jj
                rows = [rows_of(k * KV_HEADS + h, j) for h in range(KV_HEADS) for k in pages]
                halves.append(jnp.concatenate(rows, axis=0))
            lhs = jnp.concatenate(halves, axis=1).astype(BF16)
            part = _dot(lhs, wp_ref[jp])
            acc = part if acc is None else acc + part
        accs.append(acc)
    row = lax.broadcasted_iota(jnp.int32, (n, 1), 0)
    outs = []
    for h in range(KV_HEADS):
        p = jnp.concatenate([a[h * ng:(h + 1) * ng] for a in accs], axis=0)
        first = jnp.concatenate([p[:, 0:CMP_HIDDEN], p[:, 2 * CMP_HIDDEN:3 * CMP_HIDDEN]], axis=1)
        second = jnp.concatenate([p[:, CMP_HIDDEN:2 * CMP_HIDDEN], p[:, 3 * CMP_HIDDEN:]], axis=1)
        prev = jnp.where(row == 0, carry_ref[h:h + 1, :], pltpu.roll(first, 1, axis=0))
        carry_ref[h:h + 1, :] = first[n - 1:n, :]
        hid = _gelu(prev + second + bias_ref[...])
        outs.append(_dot(hid.astype(BF16), w2_ref[...]))
    o_ref[0] = jnp.concatenate(outs, axis=1)


def _cmp_tokens(pool, page_table, wp, bias, w2b, transposed):
    b, n_pages = page_table.shape
    steps = n_pages // CMP_PAGES
    n = CMP_PAGES * SUBS

    def page_spec(k, h):
        return pl.BlockSpec((1, PAGE, 2 * HEAD_DIM), lambda i, s, pt: (pt[i, s * CMP_PAGES + k], 0, h))

    def page_spec_t(k):
        return pl.BlockSpec((1, KV_HEADS, 2 * HEAD_DIM, PAGE), lambda i, s, pt: (pt[i, s * CMP_PAGES + k], 0, 0, 0))

    scratch = [pltpu.VMEM((SUBLANES, 2 * CMP_HIDDEN), F32)]
    if transposed:
        scratch += [pltpu.VMEM((CMP_PAGES * KV_HEADS // CMP_GROUPS, PAGE, 2 * HEAD_DIM), F32)
                    for _ in range(CMP_GROUPS)]
        page_specs = [page_spec_t(k) for k in range(CMP_PAGES)]
    else:
        page_specs = [page_spec(k, h) for k in range(CMP_PAGES) for h in range(KV_HEADS)]
    grid_spec = pltpu.PrefetchScalarGridSpec(
        num_scalar_prefetch=1,
        grid=(b, steps),
        in_specs=page_specs + [
            pl.BlockSpec(wp.shape, lambda i, s, pt: (0, 0, 0)),
            pl.BlockSpec(bias.shape, lambda i, s, pt: (0, 0)),
            pl.BlockSpec(w2b.shape, lambda i, s, pt: (0, 0))],
        out_specs=pl.BlockSpec((1, n, KV_WIDTH), lambda i, s, pt: (i, s, 0)),
        scratch_shapes=scratch,
    )
    return pl.pallas_call(
        functools.partial(_cmp_tokens_kernel, transposed=transposed),
        grid_spec=grid_spec,
        out_shape=jax.ShapeDtypeStruct((b, steps * n, KV_WIDTH), F32),
        compiler_params=_cparams("parallel", "arbitrary"),
        name="cmp_tokens_t" if transposed else "cmp_tokens",
    )(page_table, *([pool] * len(page_specs)), wp, bias, w2b)


def _cover_matrix(n_tok, n_sel):
    i = np.arange(n_tok)[:, None]
    start = (i - 1) * CMP_STRIDE
    sj = np.arange(n_sel)[None, :] * SEL_BLOCK
    cov = (start < sj + SEL_BLOCK) & (start + CMP_BLOCK > sj) & (i >= 1)
    return cov.astype(np.float32)


KT_UNROLL = 2


def _split_hi_lo(x):
    hi = x.astype(BF16)
    lo = (x - hi.astype(F32)).astype(BF16)
    return hi, lo


def _nsa_prompt_kernel(q_ref, ckv_ref, kslc_ref, kwin_ref, gates_ref, cover_ref, o_ref, sel_ref, acc_ref,
                       s_ref, p_ref, *, tq, tk):
    q0 = pl.program_id(1) * tq
    qt = (q_ref[0] * SCALE).T
    gt = jax.nn.sigmoid(gates_ref[0]).T
    qpos = q0 + lax.broadcasted_iota(jnp.int32, (1, tq), 1)
    lane = lax.broadcasted_iota(jnp.int32, (1, GROUP * tq), 1)
    n_cmp = ckv_ref.shape[1]
    n_sel = cover_ref.shape[0]
    per_tile = tk // SEL_BLOCK
    wide = GROUP * tq

    def tile4(x):
        return jnp.concatenate([x] * GROUP, axis=1)

    ik = lax.broadcasted_iota(jnp.int32, (tk, 1), 0)
    rel = ik - lax.broadcasted_iota(jnp.int32, (1, tq), 1)
    row64 = lax.broadcasted_iota(jnp.int32, (HEAD_DIM, 1), 0)
    lane128 = lax.broadcasted_iota(jnp.int32, (1, 2 * HEAD_DIM), 1)
    kfeat = jnp.where((lane128 == HEAD_DIM) | (lane128 == HEAD_DIM + 1), ik.astype(F32), 0.0)
    q4, slope4, slope2, qaug = [], [], [], []
    for h in range(KV_HEADS):
        qf = jnp.concatenate([qt[(h * GROUP + g) * HEAD_DIM:(h * GROUP + g + 1) * HEAD_DIM, :]
                              for g in range(GROUP)], axis=1)
        q4.append(qf.astype(BF16))
        sl = jnp.zeros((1, wide), F32)
        for g in range(GROUP):
            sl = jnp.where((lane >= g * tq) & (lane < (g + 1) * tq), SLOPES[h * GROUP + g], sl)
        slope4.append(sl)
        sl2 = sl * LOG2E
        slope2.append(sl2)
        sl_hi = sl2.astype(BF16).astype(F32)
        extra = jnp.where(row64 == 0, sl_hi, jnp.where(row64 == 1, sl2 - sl_hi, 0.0))
        qaug.append(jnp.concatenate([qf * LOG2E, extra], axis=0).astype(BF16))

    o_cmp = []
    slot = lax.broadcasted_iota(jnp.int32, (n_cmp, 1), 0)
    dist = qpos - ((slot + 1) * CMP_STRIDE - 1)
    cmask = tile4(((dist >= 0) & (slot >= 1)).astype(F32))
    cdist4 = tile4(dist.astype(F32))
    cov = cover_ref[...].astype(BF16)
    jj = lax.broadcasted_iota(jnp.int32, (n_sel, 1), 0)
    cur = qpos // SEL_BLOCK
    forced = (jj == 0) | (jj == cur) | (jj == cur - 1)
    valid = jj * SEL_BLOCK <= qpos
    for h in range(KV_HEADS):
        ckv = ckv_ref[0, :, h * 2 * HEAD_DIM:(h + 1) * 2 * HEAD_DIM]
        ck = ckv[:, :HEAD_DIM].astype(BF16)
        cvt = ckv.T[HEAD_DIM:, :].astype(BF16)
        s = _dot(ck, q4[h]) - slope4[h] * cdist4
        s = jnp.where(cmask > 0, s, NEG)
        p = jnp.exp(s - jnp.max(s, axis=0, keepdims=True)) * cmask
        p = p / jnp.maximum(jnp.sum(p, axis=0, keepdims=True), 1e-30)
        o_cmp.append(_dot(cvt, p.astype(BF16)))
        psum = p[:, 0:tq]
        for g in range(1, GROUP):
            psum = psum + p[:, g * tq:(g + 1) * tq]
        p_hi, p_lo = _split_hi_lo(psum)
        imp = _dot(cov, p_hi) + _dot(cov, p_lo)
        imp = jnp.where(forced, BIG, jnp.where(valid, imp, NEG))
        cnt = jnp.zeros((n_sel, tq), F32)
        for j2 in range(n_sel):
            r = imp[j2:j2 + 1, :]
            cnt = cnt + jnp.where(r > imp, 1.0, jnp.where((r == imp) & (j2 < jj), 1.0, 0.0))
        sel = jnp.where(cnt < SEL_TOP, BIG, NEG)
        for kk in range(n_sel // per_tile):
            sel_ref[h, kk] = sel[kk * per_tile:(kk + 1) * per_tile, :]

    half = ik // SEL_BLOCK
    ones_rows = jnp.ones((SUBLANES, tk), F32)

    def attend(kv_ref, lo, hi, cap_fn):
        for h in range(KV_HEADS):
            acc_ref[h] = jnp.zeros((HEAD_DIM + SUBLANES, wide), F32)

        def body(i, ms):
            ms = list(ms)
            tiles = []
            for sub in range(KT_UNROLL):
                kt_raw = lo + KT_UNROLL * i + sub
                kt = jnp.minimum(kt_raw, hi - 1)
                k0 = pl.multiple_of(kt * tk, tk)
                d = jnp.where(kt_raw < hi, (q0 - k0) - rel, -1)
                tiles.append((kt, k0, d))
                for h in range(KV_HEADS):
                    kv = kv_ref[0, pl.ds(k0, tk), h * 2 * HEAD_DIM:(h + 1) * 2 * HEAD_DIM]
                    k_aug = jnp.where(lane128 < HEAD_DIM, kv, kfeat).astype(BF16)
                    s_ref[sub, h] = _dot(k_aug, qaug[h])
            for sub, (kt, k0, d) in enumerate(tiles):
                k0f = k0.astype(F32)
                alphas = []
                for h in range(KV_HEADS):
                    sc = jnp.minimum(s_ref[sub, h], tile4(cap_fn(h, kt, d)))
                    c = slope2[h] * k0f
                    m_new = jnp.maximum(ms[h], jnp.max(sc, axis=0, keepdims=True) + c)
                    p_ref[sub, h] = jnp.exp2(sc - (m_new - c)).astype(BF16)
                    alphas.append(jnp.exp2(ms[h] - m_new))
                    ms[h] = m_new
                for h in range(KV_HEADS):
                    kv = kv_ref[0, pl.ds(k0, tk), h * 2 * HEAD_DIM:(h + 1) * 2 * HEAD_DIM]
                    vt = jnp.concatenate([kv.T[HEAD_DIM:, :], ones_rows], axis=0).astype(BF16)
                    acc_ref[h] = alphas[h] * acc_ref[h] + _dot(vt, p_ref[sub, h])
            return tuple(ms)

        trips = (hi - lo + KT_UNROLL - 1) // KT_UNROLL
        lax.fori_loop(0, trips, body, tuple(jnp.full((1, wide), NEG, F32) for _ in range(KV_HEADS)))
        return [acc_ref[h, :HEAD_DIM, :] / jnp.maximum(acc_ref[h, HEAD_DIM:HEAD_DIM + 1, :], 1e-30)
                for h in range(KV_HEADS)]

    def slc_cap(h, kt, d):
        rows = sel_ref[h, kt]
        cap = rows[0:1, :]
        for r in range(1, per_tile):
            cap = jnp.where(half == r, rows[r:r + 1, :], cap)
        return jnp.where(d >= 0, cap, NEG)

    def win_cap(h, kt, d):
        return jnp.where(d >= 0, jnp.where(d < WINDOW, BIG, NEG), NEG)

    hi = (q0 + tq + tk - 1) // tk
    o_slc = attend(kslc_ref, 0, hi, slc_cap)
    o_win = attend(kwin_ref, jnp.maximum(q0 - WINDOW, 0) // tk, hi, win_cap)

    out_rows = []
    for h in range(KV_HEADS):
        for g in range(GROUP):
            hh = h * GROUP + g
            sl = slice(g * tq, (g + 1) * tq)
            out_rows.append(gt[hh:hh + 1, :] * o_cmp[h][:, sl]
                            + gt[NSA_HEADS + hh:NSA_HEADS + hh + 1, :] * o_slc[h][:, sl]
                            + gt[2 * NSA_HEADS + hh:2 * NSA_HEADS + hh + 1, :] * o_win[h][:, sl])
    o_ref[0] = jnp.concatenate(out_rows, axis=0).T


def _nsa_prompt_attn(q, ckv, kv_slc, kv_win, gates, tq=128, tk=128):
    b, t, _ = q.shape
    n_cmp = ckv.shape[1]
    n_sel = t // SEL_BLOCK
    cover_t = jnp.asarray(_cover_matrix(n_cmp, n_sel).T)
    kern = functools.partial(_nsa_prompt_kernel, tq=tq, tk=tk)
    return pl.pallas_call(
        kern,
        grid=(b, t // tq),
        in_specs=[pl.BlockSpec((1, tq, NSA_WIDTH), lambda i, j: (i, j, 0)),
                  pl.BlockSpec((1, n_cmp, KV_WIDTH), lambda i, j: (i, 0, 0)),
                  pl.BlockSpec((1, t, KV_WIDTH), lambda i, j: (i, 0, 0)),
                  pl.BlockSpec((1, t, KV_WIDTH), lambda i, j: (i, 0, 0)),
                  pl.BlockSpec((1, tq, LANES), lambda i, j: (i, j, 0)),
                  pl.BlockSpec((n_sel, n_cmp), lambda i, j: (0, 0))],
        out_specs=pl.BlockSpec((1, tq, NSA_WIDTH), lambda i, j: (i, j, 0)),
        out_shape=jax.ShapeDtypeStruct((b, t, NSA_WIDTH), F32),
        scratch_shapes=[pltpu.VMEM((KV_HEADS, n_sel * SEL_BLOCK // tk, tk // SEL_BLOCK, tq), F32),
                        pltpu.VMEM((KV_HEADS, HEAD_DIM + SUBLANES, GROUP * tq), F32),
                        pltpu.VMEM((KT_UNROLL, KV_HEADS, tk, GROUP * tq), F32),
                        pltpu.VMEM((KT_UNROLL, KV_HEADS, tk, GROUP * tq), BF16)],
        compiler_params=_cparams("parallel", "parallel"),
        name="nsa_prompt_attn",
    )(q, ckv, kv_slc, kv_win, gates, cover_t)


TS = SUBLANES
SEL_LANES = 384


def _nsa_sample_select_kernel(q_ref, ckv_ref, cover_ref, ocmp_ref, idx_ref, *, nbatch, past_len, n_sel):
    n_cmp = ckv_ref.shape[1]
    rowi = lax.broadcasted_iota(jnp.int32, (GROUP * TS, 1), 0)
    tpos = past_len + rowi % TS
    slot = lax.broadcasted_iota(jnp.int32, (1, n_cmp), 1)
    dist = tpos - ((slot + 1) * CMP_STRIDE - 1)
    mask = ((dist >= 0) & (slot >= 1)).astype(F32)
    jj = lax.broadcasted_iota(jnp.int32, (1, SEL_LANES), 1)
    jjf = jj.astype(F32)
    qp8 = past_len + lax.broadcasted_iota(jnp.int32, (TS, 1), 0)
    cur = qp8 // SEL_BLOCK
    forced = (jj == 0) | (jj == cur) | (jj == cur - 1)
    valid = jj * SEL_BLOCK <= qp8
    lane = lax.broadcasted_iota(jnp.int32, (1, LANES), 1)
    cov = cover_ref[...].astype(BF16)
    for bb in range(nbatch):
        q = q_ref[bb] * SCALE
        idx_out = jnp.zeros((TS, LANES), jnp.int32)
        heads = []
        for h in range(KV_HEADS):
            qh = jnp.concatenate([q[:, (h * GROUP + g) * HEAD_DIM:(h * GROUP + g + 1) * HEAD_DIM]
                                  for g in range(GROUP)], axis=0).astype(BF16)
            slope = jnp.zeros((GROUP * TS, 1), F32)
            for g in range(GROUP):
                slope = jnp.where(rowi // TS == g, SLOPES[h * GROUP + g], slope)
            ckv = ckv_ref[bb, :, h * 2 * HEAD_DIM:(h + 1) * 2 * HEAD_DIM]
            s = _dot_nt(qh, ckv[:, :HEAD_DIM].astype(BF16)) - slope * dist.astype(F32)
            s = jnp.where(mask > 0, s, NEG)
            p = jnp.exp(s - jnp.max(s, axis=-1, keepdims=True)) * mask
            p = p / jnp.maximum(jnp.sum(p, axis=-1, keepdims=True), 1e-30)
            o = _dot(p.astype(BF16), ckv[:, HEAD_DIM:].astype(BF16))
            heads += [o[g * TS:(g + 1) * TS, :] for g in range(GROUP)]
            psum = p[0:TS]
            for g in range(1, GROUP):
                psum = psum + p[g * TS:(g + 1) * TS]
            p_hi, p_lo = _split_hi_lo(psum)
            imp = _dot(p_hi, cov) + _dot(p_lo, cov)
            imp = jnp.where(forced, BIG, jnp.where(valid, imp, NEG))
            imp = jnp.where(jj < n_sel, imp, -3e38)
            for k in range(SEL_TOP):
                best = jnp.max(imp, axis=-1, keepdims=True)
                pick = jnp.min(jnp.where(imp == best, jjf, float(SEL_LANES)), axis=-1, keepdims=True)
                idx_out = jnp.where(lane == h * SEL_TOP + k, pick.astype(jnp.int32), idx_out)
                imp = jnp.where(jjf == pick, -3e38, imp)
        ocmp_ref[bb] = jnp.concatenate(heads, axis=-1)
        idx_ref[bb] = idx_out


def _nsa_sample_select(q, ckv, past_len, nbatch=4):
    b = q.shape[0]
    n_cmp = ckv.shape[1]
    n_sel = -(-(past_len + 4) // SEL_BLOCK)
    cov = np.zeros((n_cmp, SEL_LANES), np.float32)
    cov[:, :n_sel] = _cover_matrix(n_cmp, n_sel)
    kern = functools.partial(_nsa_sample_select_kernel, nbatch=nbatch, past_len=past_len, n_sel=n_sel)
    return pl.pallas_call(
        kern,
        grid=(b // nbatch,),
        in_specs=[pl.BlockSpec((nbatch, TS, NSA_WIDTH), lambda i: (i, 0, 0)),
                  pl.BlockSpec((nbatch, n_cmp, KV_WIDTH), lambda i: (i, 0, 0)),
                  pl.BlockSpec((n_cmp, SEL_LANES), lambda i: (0, 0))],
        out_specs=[pl.BlockSpec((nbatch, TS, NSA_WIDTH), lambda i: (i, 0, 0)),
                   pl.BlockSpec((nbatch, TS, LANES), lambda i: (i, 0, 0))],
        out_shape=[jax.ShapeDtypeStruct((b, TS, NSA_WIDTH), F32),
                   jax.ShapeDtypeStruct((b, TS, LANES), jnp.int32)],
        compiler_params=_cparams("parallel"),
        name="nsa_sample_select",
    )(q, ckv, jnp.asarray(cov))


def _nsa_sample_attend_kernel(idx_ref, pt_ref, q_ref, ocmp_ref, gates_ref, knew_ref, wbuf_ref, wnew_ref,
                              pool_ref, o_ref, buf_ref, sem, *, t_dec, past_len):
    b = pl.program_id(0)
    last_blk = past_len // SEL_BLOCK
    per_page = PAGE // SEL_BLOCK

    nb = pl.num_programs(0)
    slot = b % 2

    def block_of(t, h, k, bb=None):
        bb = b if bb is None else bb
        return idx_ref[((bb * t_dec + t) * KV_HEADS + h) * SEL_TOP + k]

    def page_copies(bb, sl):
        out = []
        for t in range(t_dec):
            for h in range(KV_HEADS):
                for k in range(SEL_TOP):
                    page = pt_ref[bb, jnp.minimum(block_of(t, h, k, bb), last_blk - 1) // per_page]
                    out.append(pltpu.make_async_copy(pool_ref.at[page, h], buf_ref.at[sl, t, h, k], sem.at[sl]))
        return out

    @pl.when(b == 0)
    def _():
        for cp in page_copies(b, slot):
            cp.start()

    @pl.when(b + 1 < nb)
    def _():
        for cp in page_copies(b + 1, 1 - slot):
            cp.start()

    for cp in page_copies(b, slot):
        cp.wait()

    def rows_to_t(x):
        return jnp.concatenate([x, jnp.zeros((PAGE - TS, 2 * HEAD_DIM), F32)], axis=0).T

    rowg = lax.broadcasted_iota(jnp.int32, (SUBLANES, 1), 0)
    slope_col = [jnp.zeros((SUBLANES, 1), F32) for _ in range(KV_HEADS)]
    for h in range(KV_HEADS):
        for g in range(GROUP):
            slope_col[h] = jnp.where(rowg == g, SLOPES[h * GROUP + g], slope_col[h])
    n_tok = SEL_TOP * PAGE
    lane = lax.broadcasted_iota(jnp.int32, (1, n_tok), 1)
    row = lane % PAGE
    row_half = row // SEL_BLOCK
    slot_of = lane // PAGE
    n_win = wbuf_ref.shape[-1]
    wlane = lax.broadcasted_iota(jnp.int32, (1, n_win + PAGE), 1)
    wpos = jnp.where(wlane < n_win, past_len - n_win + wlane, past_len + wlane - n_win)

    for h in range(KV_HEADS):
        hs = slice(h * 2 * HEAD_DIM, (h + 1) * 2 * HEAD_DIM)
        new_t = rows_to_t(knew_ref[0, :, hs])
        wnew_t = rows_to_t(wnew_ref[0, :, hs])
        kw_t = jnp.concatenate([wbuf_ref[0, h, 0], wnew_t[:HEAD_DIM]], axis=1).astype(BF16)
        vw_t = jnp.concatenate([wbuf_ref[0, h, 1], wnew_t[HEAD_DIM:]], axis=1).astype(BF16)
        for t in range(t_dec):
            qpos = past_len + t
            qh = (q_ref[0, t, h] * SCALE).astype(BF16)
            k_tiles, v_tiles = [], []
            tok = jnp.zeros((1, n_tok), jnp.int32)
            want_half = jnp.zeros((1, n_tok), jnp.int32)
            for k in range(SEL_TOP):
                blk = block_of(t, h, k)
                is_new = blk >= last_blk
                k_tiles.append(jnp.where(is_new, new_t[:HEAD_DIM], buf_ref[slot, t, h, k, 0]))
                v_tiles.append(jnp.where(is_new, new_t[HEAD_DIM:], buf_ref[slot, t, h, k, 1]))
                tok = jnp.where(slot_of == k, (blk // per_page) * PAGE + row, tok)
                want_half = jnp.where(slot_of == k, blk % per_page, want_half)
            k_t = jnp.concatenate(k_tiles, axis=1).astype(BF16)
            v_t = jnp.concatenate(v_tiles, axis=1).astype(BF16)
            d = qpos - tok
            msk = jnp.where(d >= 0, jnp.where(row_half == want_half, 1.0, 0.0), 0.0)
            s = _dot(qh, k_t) - slope_col[h] * d.astype(F32)
            s = jnp.where(msk > 0, s, NEG)
            p = jnp.exp(s - jnp.max(s, axis=-1, keepdims=True)) * msk
            p = p / jnp.maximum(jnp.sum(p, axis=-1, keepdims=True), 1e-30)
            o_slc = _dot_nt(p.astype(BF16), v_t)
            dw = qpos - wpos
            mw = jnp.where(dw >= 0, jnp.where(dw < WINDOW, 1.0, 0.0), 0.0)
            sw = _dot(qh, kw_t) - slope_col[h] * dw.astype(F32)
            sw = jnp.where(mw > 0, sw, NEG)
            pw = jnp.exp(sw - jnp.max(sw, axis=-1, keepdims=True)) * mw
            pw = pw / jnp.maximum(jnp.sum(pw, axis=-1, keepdims=True), 1e-30)
            o_win = _dot_nt(pw.astype(BF16), vw_t)
            gts = jax.nn.sigmoid(gates_ref[0, t, h])
            o_ref[0, t, h] = (gts[:, 0:1] * ocmp_ref[0, t, h] + gts[:, 1:2] * o_slc + gts[:, 2:3] * o_win)


def _nsa_sample_attend(idx, page_table, q5, ocmp5, gates5, kv_slc_new, win_buf_t, kv_win_new, pool_slc_t,
                       t_dec, past_len):
    b = q5.shape[0]
    n_win = win_buf_t.shape[-1]
    blk5 = (1, t_dec, KV_HEADS, SUBLANES, HEAD_DIM)
    grid_spec = pltpu.PrefetchScalarGridSpec(
        num_scalar_prefetch=2,
        grid=(b,),
        in_specs=[pl.BlockSpec(blk5, lambda i, *_: (i, 0, 0, 0, 0)),
                  pl.BlockSpec(blk5, lambda i, *_: (i, 0, 0, 0, 0)),
                  pl.BlockSpec((1, t_dec, KV_HEADS, SUBLANES, LANES), lambda i, *_: (i, 0, 0, 0, 0)),
                  pl.BlockSpec((1, TS, KV_WIDTH), lambda i, *_: (i, 0, 0)),
                  pl.BlockSpec((1, KV_HEADS, 2, HEAD_DIM, n_win), lambda i, *_: (i, 0, 0, 0, 0)),
                  pl.BlockSpec((1, TS, KV_WIDTH), lambda i, *_: (i, 0, 0)),
                  pl.BlockSpec(memory_space=pl.ANY)],
        out_specs=pl.BlockSpec(blk5, lambda i, *_: (i, 0, 0, 0, 0)),
        scratch_shapes=[pltpu.VMEM((2, t_dec, KV_HEADS, SEL_TOP, 2, HEAD_DIM, PAGE), F32),
                        pltpu.SemaphoreType.DMA((2,))],
    )
    kern = functools.partial(_nsa_sample_attend_kernel, t_dec=t_dec, past_len=past_len)
    return pl.pallas_call(
        kern,
        grid_spec=grid_spec,
        out_shape=jax.ShapeDtypeStruct((b, t_dec, KV_HEADS, SUBLANES, HEAD_DIM), F32),
        compiler_params=_cparams("arbitrary"),
        name="nsa_sample_attend",
    )(idx[:, :t_dec, :KV_HEADS * SEL_TOP].reshape(-1), page_table, q5, ocmp5, gates5, kv_slc_new, win_buf_t,
      kv_win_new, pool_slc_t)


NSA_SPLITS = ((0, NSA_WIDTH), (NSA_WIDTH, KV_WIDTH), (NSA_WIDTH + KV_WIDTH, KV_WIDTH),
              (NSA_WIDTH + 2 * KV_WIDTH, KV_WIDTH), (NSA_WIDTH + 3 * KV_WIDTH, MEM_WIDTH),
              (NSA_WIDTH + 3 * KV_WIDTH + MEM_WIDTH, LANES))


def _nsa_in_weight(w_in):
    o = NSA_WIDTH + 3 * KV_WIDTH
    n_gate = 3 * NSA_HEADS
    gates = jnp.pad(w_in[:, o:o + n_gate], ((0, 0), (0, LANES - n_gate)))
    return jnp.concatenate([w_in[:, :o], w_in[:, o + n_gate:], gates], axis=1).astype(BF16)


def _to_heads5(x, t_dec):
    b = x.shape[0]
    w = x.shape[-1] // NSA_HEADS
    x = x[:, :t_dec].reshape(b, t_dec, KV_HEADS, GROUP, w)
    return jnp.pad(x, ((0, 0), (0, 0), (0, 0), (0, SUBLANES - GROUP), (0, 0)))


def kernel(x_prompt, x_sample, mem_prompt, state_ssm_re, state_ssm_im, cache_cmp_kv, cache_slc_kv, cache_win_kv, cache_mem_kv, page_table, norm_mix_pre, norm_mix_post, norm_ffn_pre, norm_ffn_post, w_out, w_mem_kv, w_ffn_in, w_ffn_out, ssm_w_in, ssm_a_re, ssm_a_im, ssm_log_dt, ssm_b_re, ssm_b_im, ssm_c_re, ssm_c_im, ssm_d, ssm_w_glu, ssm_b_glu, nsa_w_in, nsa_cmp_pe, nsa_cmp_w1, nsa_cmp_b1, nsa_cmp_w2):
    bp, seq, d = x_prompt.shape
    bs, t_dec, _ = x_sample.shape
    n_mem = mem_prompt.shape[1]
    depth = w_out.shape[0]
    past_len = page_table.shape[1] * PAGE
    d_ff = w_ffn_out.shape[1]
    chunk = 16

    xp = x_prompt.reshape(bp * seq, d)
    xs = jnp.pad(x_sample, ((0, 0), (0, TS - t_dec), (0, 0))).reshape(bs * TS, d)
    mem2d = mem_prompt.reshape(bp * n_mem, d)

    outs = {k: [] for k in ('ssm_re_p', 'ssm_im_p', 'ssm_re_s', 'ssm_im_s', 'cmp_p', 'slc_p', 'win_p',
                            'cmp_s', 'slc_s', 'win_s', 'mem_p')}
    for i in range(depth):
        j = i // 2
        mem_split = ((0, 2 * MEM_WIDTH),)
        mkv_p, mkv_p_t = _norm_proj(mem2d, norm_mix_pre[i], w_mem_kv[i].astype(BF16), mem_split, do_norm=False,
                                    t_splits=mem_split, seq=n_mem)
        mkv_p = mkv_p.reshape(bp, n_mem, 2 * MEM_WIDTH)
        outs['mem_p'].append(jnp.moveaxis(mkv_p_t.reshape(bp, MEM_HEADS, 2, HEAD_DIM, n_mem), -1, 1))
        mkv_s_t = _kv_cache_t(cache_mem_kv[i])
        if i % 2 == 0:
            w_in = ssm_w_in[j].astype(BF16)
            wglu = ssm_w_glu[j].astype(BF16)
            ssm_w = (ssm_a_re[j], ssm_a_im[j], ssm_log_dt[j], ssm_b_re[j], ssm_b_im[j], ssm_c_re[j], ssm_c_im[j])
            ssm_split = ((0, SSM_WIDTH), (SSM_WIDTH, MEM_WIDTH))
            u, qm = _norm_proj(xp, norm_mix_pre[i], w_in, ssm_split)
            zero = jnp.zeros((bp, SSM_GROUPS, SSM_STATE), F32)
            y, hlr, hli = _s5_scan_octets(u.reshape(bp, seq, SSM_WIDTH), zero, zero,
                                          _s5_operators(*ssm_w, chunk), chunk)
            m1p = _s5_glu(y.reshape(bp * seq, SSM_WIDTH), u, ssm_d[j], wglu, ssm_b_glu[j])
            m2p = _mem_attn(qm.reshape(bp, seq, MEM_WIDTH), mkv_p).reshape(bp * seq, MEM_WIDTH)
            outs['ssm_re_p'].append(hlr)
            outs['ssm_im_p'].append(hli)
            u, qm = _norm_proj(xs, norm_mix_pre[i], w_in, ssm_split)
            y, hlr, hli = _s5_scan(u.reshape(bs, TS, SSM_WIDTH)[:, :t_dec], state_ssm_re[j].astype(F32),
                                   state_ssm_im[j].astype(F32), _s5_operators(*ssm_w, t_dec), t_dec)
            y = jnp.pad(y, ((0, 0), (0, TS - t_dec), (0, 0)))
            m1s = _s5_glu(y.reshape(bs * TS, SSM_WIDTH), u, ssm_d[j], wglu, ssm_b_glu[j])
            m2s = _mem_attn_t(qm.reshape(bs, TS, MEM_WIDTH), mkv_s_t).reshape(bs * TS, MEM_WIDTH)
            outs['ssm_re_s'].append(hlr)
            outs['ssm_im_s'].append(hli)
        else:
            w_in = _nsa_in_weight(nsa_w_in[j])
            wp, cbias, w2b = _cmp_weights(nsa_cmp_pe[j], nsa_cmp_w1[j], nsa_cmp_b1[j], nsa_cmp_w2[j])
            kv_shape = (KV_HEADS, 2, HEAD_DIM)
            q, kc, ks, kw, qm, gates, kc_t, ks_t, kw_t = _norm_proj(
                xp, norm_mix_pre[i], w_in, NSA_SPLITS, t_splits=NSA_SPLITS[1:4], seq=seq)
            kc3, ks3, kw3 = (a.reshape(bp, seq, KV_WIDTH) for a in (kc, ks, kw))

            def rows_major(a_t):
                return jnp.moveaxis(a_t.reshape((bp,) + kv_shape + (a_t.shape[-1],)), -1, 1)
            ident = jnp.arange(bp * (seq // PAGE), dtype=jnp.int32).reshape(bp, seq // PAGE)
            ckv = _cmp_tokens(kc.reshape(bp * seq // PAGE, PAGE, KV_WIDTH), ident, wp, cbias, w2b, False)
            m1p = _nsa_prompt_attn(q.reshape(bp, seq, NSA_WIDTH), ckv, ks3, kw3,
                                   gates.reshape(bp, seq, LANES)).reshape(bp * seq, NSA_WIDTH)
            m2p = _mem_attn(qm.reshape(bp, seq, MEM_WIDTH), mkv_p).reshape(bp * seq, MEM_WIDTH)
            outs['cmp_p'].append(rows_major(kc_t))
            outs['slc_p'].append(rows_major(ks_t))
            outs['win_p'].append(rows_major(kw_t[:, :, seq - min(WINDOW, seq):]))
            q, kc, ks, kw, qm, gates = _norm_proj(xs, norm_mix_pre[i], w_in, NSA_SPLITS)
            kc3, ks3, kw3 = (a.reshape(bs, TS, KV_WIDTH) for a in (kc, ks, kw))
            pool_cmp_t = _kv_cache_t(cache_cmp_kv[j])
            pool_cmp_t = pool_cmp_t.reshape(pool_cmp_t.shape[0], KV_HEADS, 2 * HEAD_DIM, PAGE)
            pool_slc_t = _kv_cache_t(cache_slc_kv[j])
            win_buf_t = _kv_cache_t(cache_win_kv[j])
            ckv = _cmp_tokens(pool_cmp_t, page_table, wp, cbias, w2b, True)
            q3 = q.reshape(bs, TS, NSA_WIDTH)
            ocmp, idx = _nsa_sample_select(q3, ckv, past_len)
            gates3 = gates.reshape(bs, TS, LANES)[:, :, :3 * NSA_HEADS].reshape(bs, TS, 3, NSA_HEADS)
            gates5 = _to_heads5(gates3.transpose(0, 1, 3, 2).reshape(bs, TS, NSA_HEADS * 3), t_dec)
            gates5 = jnp.pad(gates5, ((0, 0),) * 4 + ((0, LANES - 3),))
            o5 = _nsa_sample_attend(idx, page_table, _to_heads5(q3, t_dec), _to_heads5(ocmp, t_dec), gates5,
                                    ks3, win_buf_t, kw3, pool_slc_t, t_dec, past_len)
            o = o5[:, :, :, :GROUP].reshape(bs, t_dec, NSA_WIDTH)
            m1s = jnp.pad(o, ((0, 0), (0, TS - t_dec), (0, 0))).reshape(bs * TS, NSA_WIDTH)
            m2s = _mem_attn_t(qm.reshape(bs, TS, MEM_WIDTH), mkv_s_t).reshape(bs * TS, MEM_WIDTH)
            outs['cmp_s'].append(kc3[:, :t_dec].reshape((bs, t_dec) + kv_shape))
            outs['slc_s'].append(ks3[:, :t_dec].reshape((bs, t_dec) + kv_shape))
            kv_w = jnp.concatenate([cache_win_kv[j], kw3[:, :t_dec].reshape((bs, t_dec) + kv_shape)], axis=1)
            n_keep = min(WINDOW, past_len + t_dec)
            outs['win_s'].append(kv_w[:, kv_w.shape[1] - n_keep:])
        wo = w_out[i].astype(BF16)
        wg = w_ffn_in[i][:, :d_ff].astype(BF16)
        wu = w_ffn_in[i][:, d_ff:].astype(BF16)
        wd = w_ffn_out[i].astype(BF16)
        w1 = m1p.shape[1]
        args = (wo[:w1], wo[w1:], norm_mix_post[i], norm_ffn_pre[i], wg, wu, wd, norm_ffn_post[i])
        xp = _post(xp, m1p, m2p, *args)
        xs = _post(xs, m1s, m2s, *args)

    st = lambda k: jnp.stack(outs[k])
    y_sample = xs.reshape(bs, TS, d)[:, :t_dec]
    return (xp.reshape(bp, seq, d), y_sample, st('ssm_re_p'), st('ssm_im_p'), st('ssm_re_s'), st('ssm_im_s'),
            st('cmp_p'), st('slc_p'), st('win_p'), st('cmp_s'), st('slc_s'), st('win_s'), st('mem_p'))
```

```python
import functools
import math

import numpy as np
import jax
import jax.numpy as jnp
from jax import lax
from jax.experimental import pallas as pl
from jax.experimental.pallas import tpu as pltpu

F32 = jnp.float32
BF16 = jnp.bfloat16

D_MODEL = 1024
PAGE = 128
MEM_HEADS = 4
HEAD_DIM = 64
MEM_WIDTH = MEM_HEADS * HEAD_DIM
SSM_WIDTH = D_MODEL - MEM_WIDTH
SSM_GROUP = 16
SSM_GROUPS = SSM_WIDTH // SSM_GROUP
SSM_STATE = 64
NSA_HEADS = 12
KV_HEADS = 3
GROUP = NSA_HEADS // KV_HEADS
NSA_WIDTH = NSA_HEADS * HEAD_DIM
KV_WIDTH = KV_HEADS * 2 * HEAD_DIM
CMP_BLOCK = 32
CMP_STRIDE = 16
CMP_HIDDEN = 2 * HEAD_DIM
SEL_BLOCK = 64
SEL_TOP = 16
WINDOW = 512
RMS_EPS = 1e-6
NEG = -1e30
BIG = 1e30
SCALE = HEAD_DIM ** -0.5
LANES = 128
SUBLANES = 8
VMEM_LIMIT = 56 * 1024 * 1024
SLOPES = [2.0 ** (-8.0 * (h + 1) / NSA_HEADS) for h in range(NSA_HEADS)]
LOG2E = math.log2(math.e)


def _cparams(*sem):
    return pltpu.CompilerParams(dimension_semantics=sem, vmem_limit_bytes=VMEM_LIMIT)


def _rms(x, g):
    return x * lax.rsqrt(jnp.mean(x * x, axis=-1, keepdims=True) + RMS_EPS) * g


def _gelu(x):
    return 0.5 * x * (1.0 + jnp.tanh(math.sqrt(2.0 / math.pi) * (x + 0.044715 * (x * x * x))))


def _dot(a, b):
    return jnp.dot(a, b, preferred_element_type=F32)


def _dot_nt(a, b):
    return lax.dot_general(a, b, (((1,), (1,)), ((), ())), preferred_element_type=F32)


def _kv_cache_t(cache):
    n = cache.ndim
    return jnp.moveaxis(cache, n - 4, n - 1)


def _norm_proj_kernel(x_ref, g_ref, w_ref, *o_refs, splits, t_splits, do_norm):
    x = x_ref[...]
    if do_norm:
        x = _rms(x, g_ref[...])
    z = _dot(x.astype(BF16), w_ref[...])
    for (start, width), o in zip(splits, o_refs):
        o[...] = z[:, start:start + width]
    for (start, width), o in zip(t_splits, o_refs[len(splits):]):
        o[0] = z[:, start:start + width].T


def _norm_proj(x, g, w, splits, do_norm=True, tm=512, t_splits=(), seq=None):
    rows, d = x.shape
    n = w.shape[1]
    tm = min(tm, rows, seq) if t_splits else min(tm, rows)
    kern = functools.partial(_norm_proj_kernel, splits=tuple(splits), t_splits=tuple(t_splits), do_norm=do_norm)
    out_specs = [pl.BlockSpec((tm, wd), lambda i: (i, 0)) for _, wd in splits]
    out_shape = [jax.ShapeDtypeStruct((rows, wd), F32) for _, wd in splits]
    if t_splits:
        per_seq = seq // tm
        out_specs += [pl.BlockSpec((1, wd, tm), lambda i: (i // per_seq, 0, i % per_seq)) for _, wd in t_splits]
        out_shape += [jax.ShapeDtypeStruct((rows // seq, wd, seq), F32) for _, wd in t_splits]
    return pl.pallas_call(
        kern,
        grid=(rows // tm,),
        in_specs=[pl.BlockSpec((tm, d), lambda i: (i, 0)),
                  pl.BlockSpec((1, d), lambda i: (0, 0)),
                  pl.BlockSpec((d, n), lambda i: (0, 0))],
        out_specs=out_specs,
        out_shape=out_shape,
        compiler_params=_cparams("parallel"),
        name="norm_proj",
    )(x, g.reshape(1, d), w)


def _softmax_rows(s):
    p = jnp.exp(s - jnp.max(s, axis=-1, keepdims=True))
    return p, jnp.sum(p, axis=-1, keepdims=True)


def _mem_attn_kernel(q_ref, kv_ref, o_ref):
    q = q_ref[0]
    kv = kv_ref[0]
    outs = []
    for h in range(MEM_HEADS):
        qh = (q[:, h * HEAD_DIM:(h + 1) * HEAD_DIM] * SCALE).astype(BF16)
        k = kv[:, h * 2 * HEAD_DIM:h * 2 * HEAD_DIM + HEAD_DIM].astype(BF16)
        v = kv[:, h * 2 * HEAD_DIM + HEAD_DIM:(h + 1) * 2 * HEAD_DIM].astype(BF16)
        p, l = _softmax_rows(_dot_nt(qh, k))
        outs.append(_dot(p.astype(BF16), v) / l)
    o_ref[0] = jnp.concatenate(outs, axis=-1)


def _mem_attn(qm, mem_kv, tm=512):
    b, t, _ = qm.shape
    tm = min(tm, t)
    n_mem = mem_kv.shape[1]
    return pl.pallas_call(
        _mem_attn_kernel,
        grid=(b, t // tm),
        in_specs=[pl.BlockSpec((1, tm, MEM_WIDTH), lambda i, j: (i, j, 0)),
                  pl.BlockSpec((1, n_mem, 2 * MEM_WIDTH), lambda i, j: (i, 0, 0))],
        out_specs=pl.BlockSpec((1, tm, MEM_WIDTH), lambda i, j: (i, j, 0)),
        out_shape=jax.ShapeDtypeStruct((b, t, MEM_WIDTH), F32),
        compiler_params=_cparams("parallel", "parallel"),
        name="mem_attn",
    )(qm, mem_kv)


def _mem_attn_t_kernel(q_ref, kv_ref, o_ref):
    q = q_ref[0]
    outs = []
    for h in range(MEM_HEADS):
        qh = (q[:, h * HEAD_DIM:(h + 1) * HEAD_DIM] * SCALE).astype(BF16)
        p, l = _softmax_rows(_dot(qh, kv_ref[0, h, 0].astype(BF16)))
        outs.append(_dot_nt(p.astype(BF16), kv_ref[0, h, 1].astype(BF16)) / l)
    o_ref[0] = jnp.concatenate(outs, axis=-1)


def _mem_attn_t(qm, mem_kv_t):
    b, t, _ = qm.shape
    n_mem = mem_kv_t.shape[-1]
    return pl.pallas_call(
        _mem_attn_t_kernel,
        grid=(b,),
        in_specs=[pl.BlockSpec((1, t, MEM_WIDTH), lambda i: (i, 0, 0)),
                  pl.BlockSpec((1, MEM_HEADS, 2, HEAD_DIM, n_mem), lambda i: (i, 0, 0, 0, 0))],
        out_specs=pl.BlockSpec((1, t, MEM_WIDTH), lambda i: (i, 0, 0)),
        out_shape=jax.ShapeDtypeStruct((b, t, MEM_WIDTH), F32),
        compiler_params=_cparams("parallel"),
        name="mem_attn_t",
    )(qm, mem_kv_t)


def _post_kernel(x_ref, m1_ref, m2_ref, wo1_ref, wo2_ref, g1_ref, g2_ref, wg_ref, wu_ref, wd_ref, g3_ref,
                 o_ref):
    a = _dot(m1_ref[...].astype(BF16), wo1_ref[...]) + _dot(m2_ref[...].astype(BF16), wo2_ref[...])
    x1 = x_ref[...] + _rms(a, g1_ref[...])
    h = _rms(x1, g2_ref[...]).astype(BF16)
    gate = _dot(h, wg_ref[...])
    up = _dot(h, wu_ref[...])
    act = (gate * jax.nn.sigmoid(gate) * up).astype(BF16)
    f = _dot(act, wd_ref[...])
    o_ref[...] = x1 + _rms(f, g3_ref[...])


def _post(x, m1, m2, wo1, wo2, g1, g2, wg, wu, wd, g3, tm=512):
    rows, d = x.shape
    tm = min(tm, rows)
    w1, w2 = m1.shape[1], m2.shape[1]
    dff = wg.shape[1]

    def const(shape):
        return pl.BlockSpec(shape, lambda i: (0, 0), pipeline_mode=pl.Buffered(1))

    return pl.pallas_call(
        _post_kernel,
        grid=(rows // tm,),
        in_specs=[pl.BlockSpec((tm, d), lambda i: (i, 0)),
                  pl.BlockSpec((tm, w1), lambda i: (i, 0)),
                  pl.BlockSpec((tm, w2), lambda i: (i, 0)),
                  const((w1, d)), const((w2, d)), const((1, d)), const((1, d)),
                  const((d, dff)), const((d, dff)), const((dff, d)), const((1, d))],
        out_specs=pl.BlockSpec((tm, d), lambda i: (i, 0)),
        out_shape=jax.ShapeDtypeStruct((rows, d), F32),
        compiler_params=_cparams("parallel"),
        name="post_ffn",
    )(x, m1, m2, wo1, wo2, g1.reshape(1, d), g2.reshape(1, d), wg, wu, wd, g3.reshape(1, d))


def _s5_operators(a_re, a_im, log_dt, b_re, b_im, c_re, c_im, q):
    hp = lax.Precision.HIGHEST
    a_re, a_im = a_re.astype(F32), a_im.astype(F32)
    dt = jnp.exp(log_dt.astype(F32))[:, None]
    mag = jnp.exp(a_re * dt)
    ab_r, ab_i = mag * jnp.cos(a_im * dt), mag * jnp.sin(a_im * dt)
    den = a_re * a_re + a_im * a_im
    nr, ni = ab_r - 1.0, ab_i
    f_r = (nr * a_re + ni * a_im) / den
    f_i = (ni * a_re - nr * a_im) / den
    bb_r = f_r[..., None] * b_re - f_i[..., None] * b_im
    bb_i = f_r[..., None] * b_im + f_i[..., None] * b_re
    pw_r, pw_i = [jnp.ones_like(ab_r)], [jnp.zeros_like(ab_i)]
    for _ in range(q):
        r, i = pw_r[-1], pw_i[-1]
        pw_r.append(r * ab_r - i * ab_i)
        pw_i.append(r * ab_i + i * ab_r)
    pw_r, pw_i = jnp.stack(pw_r), jnp.stack(pw_i)
    w_r = pw_r[:, :, :, None] * bb_r[None] - pw_i[:, :, :, None] * bb_i[None]
    w_i = pw_r[:, :, :, None] * bb_i[None] + pw_i[:, :, :, None] * bb_r[None]
    kk = (jnp.einsum('gip,tgpj->tgij', c_re, w_r, precision=hp)
          - jnp.einsum('gip,tgpj->tgij', c_im, w_i, precision=hp))
    tau = jnp.arange(q)[None, :] - jnp.arange(q)[:, None]
    kt = kk[jnp.clip(tau, 0, q)]
    kt = jnp.where((tau >= 0)[:, :, None, None, None], kt, 0.0)
    g = a_re.shape[0]
    toep = kt.transpose(2, 0, 4, 1, 3).reshape(g, q * SSM_GROUP, q * SSM_GROUP)
    rev = q - 1 - jnp.arange(q)
    bst_r = w_r[rev].transpose(1, 0, 3, 2).reshape(g, q * SSM_GROUP, SSM_STATE)
    bst_i = w_i[rev].transpose(1, 0, 3, 2).reshape(g, q * SSM_GROUP, SSM_STATE)
    ar, ai = pw_r[1:], pw_i[1:]
    co_r = c_re[None] * ar[:, :, None, :] - c_im[None] * ai[:, :, None, :]
    co_i = -c_re[None] * ai[:, :, None, :] - c_im[None] * ar[:, :, None, :]
    cout_r = co_r.transpose(1, 3, 0, 2).reshape(g, SSM_STATE, q * SSM_GROUP)
    cout_i = co_i.transpose(1, 3, 0, 2).reshape(g, SSM_STATE, q * SSM_GROUP)
    return (toep.astype(BF16), bst_r.astype(BF16), bst_i.astype(BF16), cout_r.astype(BF16),
            cout_i.astype(BF16), pw_r[q][:, None, :], pw_i[q][:, None, :])


def _s5_scan_kernel(u_ref, toep_ref, bsr_ref, bsi_ref, cor_ref, coi_ref, aqr_ref, aqi_ref, h0r_ref, h0i_ref,
                    y_ref, hlr_ref, hli_ref, sr_ref, si_ref, hpr_ref, hpi_ref, *, gs, nc, nb):
    for g in range(gs):
        u = u_ref[g].astype(BF16)
        sr_ref[g] = _dot(u, bsr_ref[g])
        si_ref[g] = _dot(u, bsi_ref[g])

    def step(c, carry):
        rows = pl.ds(pl.multiple_of(c * nb, SUBLANES), nb)
        new = []
        for g in range(gs):
            hr, hi = carry[2 * g], carry[2 * g + 1]
            hpr_ref[g, rows, :] = hr
            hpi_ref[g, rows, :] = hi
            ar, ai = aqr_ref[g], aqi_ref[g]
            new.append(ar * hr - ai * hi + sr_ref[g, rows, :])
            new.append(ar * hi + ai * hr + si_ref[g, rows, :])
        return tuple(new)

    init = []
    for g in range(gs):
        init += [h0r_ref[g], h0i_ref[g]]
    fin = lax.fori_loop(0, nc, step, tuple(init))
    for g in range(gs):
        hlr_ref[g] = fin[2 * g]
        hli_ref[g] = fin[2 * g + 1]
        u = u_ref[g].astype(BF16)
        y_ref[g] = (_dot(u, toep_ref[g]) + _dot(hpr_ref[g].astype(BF16), cor_ref[g])
                    + _dot(hpi_ref[g].astype(BF16), coi_ref[g]))


def _s5_scan(u, h0r, h0i, ops, q, gs=4):
    b, l, _ = u.shape
    nc = l // q
    qw = q * SSM_GROUP
    toep, bsr, bsi, cor, coi, aqr, aqi = ops
    ug = u.reshape(b, nc, q, SSM_GROUPS, SSM_GROUP).transpose(3, 1, 0, 2, 4).reshape(SSM_GROUPS, nc * b, qw)
    h0r_g = h0r.transpose(1, 0, 2)
    h0i_g = h0i.transpose(1, 0, 2)
    rows = nc * b

    def gspec(r, c):
        return pl.BlockSpec((gs, r, c), lambda i: (i, 0, 0))

    kern = functools.partial(_s5_scan_kernel, gs=gs, nc=nc, nb=b)
    y, hlr, hli = pl.pallas_call(
        kern,
        grid=(SSM_GROUPS // gs,),
        in_specs=[gspec(rows, qw), gspec(qw, qw), gspec(qw, SSM_STATE), gspec(qw, SSM_STATE),
                  gspec(SSM_STATE, qw), gspec(SSM_STATE, qw), gspec(1, SSM_STATE), gspec(1, SSM_STATE),
                  gspec(b, SSM_STATE), gspec(b, SSM_STATE)],
        out_specs=[gspec(rows, qw), gspec(b, SSM_STATE), gspec(b, SSM_STATE)],
        out_shape=[jax.ShapeDtypeStruct((SSM_GROUPS, rows, qw), F32),
                   jax.ShapeDtypeStruct((SSM_GROUPS, b, SSM_STATE), F32),
                   jax.ShapeDtypeStruct((SSM_GROUPS, b, SSM_STATE), F32)],
        scratch_shapes=[pltpu.VMEM((gs, rows, SSM_STATE), F32) for _ in range(4)],
        compiler_params=_cparams("parallel"),
        name="s5_scan_groups",
    )(ug, toep, bsr, bsi, cor, coi, aqr, aqi, h0r_g, h0i_g)
    y = y.reshape(SSM_GROUPS, nc, b, q, SSM_GROUP).transpose(2, 1, 3, 0, 4).reshape(b, l, SSM_WIDTH)
    return y, hlr.transpose(1, 0, 2), hli.transpose(1, 0, 2)


OCT = LANES // SSM_GROUP
N_OCT = SSM_GROUPS // OCT
OCT_STATE = OCT * SSM_STATE


def _s5_octet_operators(ops, q):
    toep, bsr, bsi, cor, coi, aqr, aqi = ops
    qq = q * LANES
    dt = toep.dtype

    def spread(x, tile_np, keep_np):
        y = jnp.einsum('...k,kn->...n', x, jnp.asarray(tile_np, dt), preferred_element_type=F32)
        return (y * jnp.asarray(keep_np, F32)).astype(dt)

    grp_of_gj = np.arange(LANES) // SSM_GROUP
    grp_of_gp = np.arange(OCT_STATE) // SSM_STATE
    lag = toep.reshape(N_OCT, OCT, q, SSM_GROUP, q, SSM_GROUP)[:, :, 0]
    lag = lag.transpose(0, 3, 1, 2, 4).reshape(N_OCT, q, LANES, SSM_GROUP)
    k8 = spread(lag, np.tile(np.eye(SSM_GROUP), (1, OCT)), grp_of_gj[:, None] == grp_of_gj[None, :])

    def bst8(b):
        b = b.reshape(N_OCT, OCT, q, SSM_GROUP, SSM_STATE).transpose(0, 2, 1, 3, 4).reshape(N_OCT, qq, SSM_STATE)
        keep = np.tile(grp_of_gj, q)[:, None] == grp_of_gp[None, :]
        return spread(b, np.tile(np.eye(SSM_STATE), (1, OCT)), keep)

    def cout8(c):
        c = c.reshape(N_OCT, OCT_STATE, q * SSM_GROUP)
        t_of, i_of = np.arange(q * SSM_GROUP) // SSM_GROUP, np.arange(q * SSM_GROUP) % SSM_GROUP
        col_t, col_i = np.arange(qq) // LANES, np.arange(qq) % SSM_GROUP
        tile = (t_of[:, None] == col_t[None, :]) & (i_of[:, None] == col_i[None, :])
        keep = grp_of_gp[:, None] == np.tile(grp_of_gj, q)[None, :]
        return spread(c, tile.astype(np.float32), keep)

    return (k8, bst8(bsr), bst8(bsi), cout8(cor), cout8(coi),
            aqr.reshape(N_OCT, 1, OCT_STATE), aqi.reshape(N_OCT, 1, OCT_STATE))


def _s5_octet_kernel(x_ref, lag_ref, bsr_ref, bsi_ref, cor_ref, coi_ref, aqr_ref, aqi_ref, h0r_ref, h0i_ref,
                     y_ref, hlr_ref, hli_ref, sr_ref, si_ref, hpr_ref, hpi_ref, hr_ref, hi_ref, toep_ref,
                     *, ncb, nb, q):
    @pl.when(pl.program_id(1) == 0)
    def _():
        hr_ref[...] = h0r_ref[0]
        hi_ref[...] = h0i_ref[0]
        zero = jnp.zeros((LANES, LANES), BF16)
        for s in range(q):
            for t in range(q):
                toep_ref[s * LANES:(s + 1) * LANES, t * LANES:(t + 1) * LANES] = (
                    lag_ref[0, t - s] if t >= s else zero)

    x = x_ref[0]
    sr_ref[...] = _dot(x, bsr_ref[0])
    si_ref[...] = _dot(x, bsi_ref[0])
    ar, ai = aqr_ref[0], aqi_ref[0]

    def step(c, carry):
        hr, hi = carry
        rows = pl.ds(pl.multiple_of(c * nb, SUBLANES), nb)
        hpr_ref[rows, :] = hr
        hpi_ref[rows, :] = hi
        return ar * hr - ai * hi + sr_ref[rows, :], ar * hi + ai * hr + si_ref[rows, :]

    hr, hi = lax.fori_loop(0, ncb, step, (hr_ref[...], hi_ref[...]))
    hr_ref[...] = hr
    hi_ref[...] = hi
    hlr_ref[0] = hr
    hli_ref[0] = hi
    y_ref[0] = (_dot(x, toep_ref[...]) + _dot(hpr_ref[...].astype(BF16), cor_ref[0])
                + _dot(hpi_ref[...].astype(BF16), coi_ref[0]))


def _s5_scan_octets(u, h0r, h0i, ops, q, row_blocks=4):
    b, l, _ = u.shape
    nc = l // q
    qq = q * LANES
    rows = nc * b
    rb = rows // row_blocks
    ncb = nc // row_blocks
    ops8 = _s5_octet_operators(ops, q)
    x = u.reshape(b, nc, q, N_OCT, LANES).transpose(3, 1, 0, 2, 4).reshape(N_OCT, rows, qq).astype(BF16)
    h0r8 = h0r.reshape(b, N_OCT, OCT_STATE).transpose(1, 0, 2)
    h0i8 = h0i.reshape(b, N_OCT, OCT_STATE).transpose(1, 0, 2)

    def wspec(r, c):
        return pl.BlockSpec((1, r, c), lambda o, i: (o, 0, 0))

    kern = functools.partial(_s5_octet_kernel, ncb=ncb, nb=b, q=q)
    y, hlr, hli = pl.pallas_call(
        kern,
        grid=(N_OCT, row_blocks),
        in_specs=[pl.BlockSpec((1, rb, qq), lambda o, i: (o, i, 0)),
                  pl.BlockSpec((1, q, LANES, LANES), lambda o, i: (o, 0, 0, 0)),
                  wspec(qq, OCT_STATE), wspec(qq, OCT_STATE), wspec(OCT_STATE, qq),
                  wspec(OCT_STATE, qq), wspec(1, OCT_STATE), wspec(1, OCT_STATE),
                  wspec(b, OCT_STATE), wspec(b, OCT_STATE)],
        out_specs=[pl.BlockSpec((1, rb, qq), lambda o, i: (o, i, 0)), wspec(b, OCT_STATE), wspec(b, OCT_STATE)],
        out_shape=[jax.ShapeDtypeStruct((N_OCT, rows, qq), F32),
                   jax.ShapeDtypeStruct((N_OCT, b, OCT_STATE), F32),
                   jax.ShapeDtypeStruct((N_OCT, b, OCT_STATE), F32)],
        scratch_shapes=[pltpu.VMEM((rb, OCT_STATE), F32) for _ in range(4)]
        + [pltpu.VMEM((b, OCT_STATE), F32) for _ in range(2)] + [pltpu.VMEM((qq, qq), BF16)],
        compiler_params=_cparams("parallel", "arbitrary"),
        name="s5_scan_octets",
    )(x, *ops8, h0r8, h0i8)
    y = y.reshape(N_OCT, nc, b, q, LANES).transpose(2, 1, 3, 0, 4).reshape(b, l, SSM_WIDTH)
    unpack = lambda h: h.transpose(1, 0, 2).reshape(b, SSM_GROUPS, SSM_STATE)
    return y, unpack(hlr), unpack(hli)


def _s5_glu_kernel(y_ref, u_ref, d_ref, w_ref, b_ref, o_ref):
    v = _gelu(y_ref[...] + d_ref[...] * u_ref[...])
    o_ref[...] = v * jax.nn.sigmoid(_dot(v.astype(BF16), w_ref[...]) + b_ref[...])


def _s5_glu(y, u, d, w, bias, tm=512):
    rows, n = y.shape
    tm = min(tm, rows)
    return pl.pallas_call(
        _s5_glu_kernel,
        grid=(rows // tm,),
        in_specs=[pl.BlockSpec((tm, n), lambda i: (i, 0)), pl.BlockSpec((tm, n), lambda i: (i, 0)),
                  pl.BlockSpec((1, n), lambda i: (0, 0)), pl.BlockSpec((n, n), lambda i: (0, 0)),
                  pl.BlockSpec((1, n), lambda i: (0, 0))],
        out_specs=pl.BlockSpec((tm, n), lambda i: (i, 0)),
        out_shape=jax.ShapeDtypeStruct((rows, n), F32),
        compiler_params=_cparams("parallel"),
        name="s5_glu",
    )(y, u, d.reshape(1, n), w, bias.reshape(1, n))


CMP_PAGES = 16
CMP_PITCH = PAGE // CMP_STRIDE + 1
CMP_GROUP_SIZES = (4, 6, 6)


def _cmp_group_pages():
    out, start = [], 0
    for size in CMP_GROUP_SIZES:
        out.append(range(start, start + size))
        start += size
    return out
SUBS = PAGE // CMP_STRIDE


def _cmp_weights(pe, w1, b1, w2):
    w = w1.reshape(2, CMP_STRIDE // 2, 2, 2, HEAD_DIM, CMP_HIDDEN)
    eye = jnp.eye(2, dtype=w1.dtype)
    wp = jnp.einsum('apjcef,cd->pjcedaf', w, eye)
    wp = wp.reshape(CMP_STRIDE // 2, 2 * 2 * HEAD_DIM, 2 * 2 * CMP_HIDDEN)
    bias = (jnp.einsum('jce,jcef->cf', pe, w1, precision=lax.Precision.HIGHEST) + b1).reshape(1, 2 * CMP_HIDDEN)
    w2b = jnp.einsum('cfe,cd->cfde', w2, eye).reshape(2 * CMP_HIDDEN, 2 * HEAD_DIM)
    return wp.astype(BF16), bias.astype(F32), w2b.astype(BF16)


def _cmp_tokens_kernel(pt_ref, *refs, transposed):
    n = CMP_PAGES * SUBS
    if transposed:
        nblk = CMP_PAGES
        page_refs = refs[:nblk]
        wp_ref, bias_ref, w2_ref, o_ref, carry_ref = refs[nblk:nblk + 5]
        xs_refs = refs[nblk + 5:]
        group_pages = _cmp_group_pages()
        where = {k: (gi, (k - pages[0]) * KV_HEADS) for gi, pages in enumerate(group_pages) for k in pages}

        def stage(pages):
            for k in pages:
                gi, base = where[k]
                for h in range(KV_HEADS):
                    x = page_refs[k][0, h].T
                    for v in range(PAGE // SUBLANES):
                        n, j0 = divmod(v * SUBLANES, CMP_STRIDE)
                        xs_refs[gi][base + h, pl.ds(j0 * CMP_PITCH + n, SUBLANES, stride=CMP_PITCH), :] = (
                            x[v * SUBLANES:(v + 1) * SUBLANES, :])

        def rows_of(i, j):
            gi, base = where[i // KV_HEADS]
            return xs_refs[gi][base + i % KV_HEADS, pl.ds(j * CMP_PITCH, SUBS), :]
    else:
        nblk = CMP_PAGES * KV_HEADS
        page_refs = refs[:nblk]
        wp_ref, bias_ref, w2_ref, o_ref, carry_ref = refs[nblk:]

        def rows_of(i, j):
            return page_refs[i][0, pl.ds(j, SUBS, stride=CMP_STRIDE), :]

    @pl.when(pl.program_id(1) == 0)
    def _():
        carry_ref[...] = jnp.zeros_like(carry_ref)

    accs = []
    for pages in (_cmp_group_pages() if transposed else [range(CMP_PAGES)]):
        ng = len(pages) * SUBS
        if transposed:
            stage(pages)
        acc = None
        for jp in range(CMP_STRIDE // 2):
            halves = []
            for jj in range(2):
                j = 2 * jp + jj
                rows = [rows_of(k * KV_HEADS + h, j) for h in range(KV_HEADS) for k in pages]
                halves.append(jnp.concatenate(rows, axis=0))
            lhs = jnp.concatenate(halves, axis=1).astype(BF16)
            part = _dot(lhs, wp_ref[jp])
            acc = part if acc is None else acc + part
        accs.append(acc)
    row = lax.broadcasted_iota(jnp.int32, (n, 1), 0)
    outs = []
    for h in range(KV_HEADS):
        p = jnp.concatenate([a[h * (a.shape[0] // KV_HEADS):(h + 1) * (a.shape[0] // KV_HEADS)] for a in accs],
                            axis=0)
        first = jnp.concatenate([p[:, 0:CMP_HIDDEN], p[:, 2 * CMP_HIDDEN:3 * CMP_HIDDEN]], axis=1)
        second = jnp.concatenate([p[:, CMP_HIDDEN:2 * CMP_HIDDEN], p[:, 3 * CMP_HIDDEN:]], axis=1)
        prev = jnp.where(row == 0, carry_ref[h:h + 1, :], pltpu.roll(first, 1, axis=0))
        carry_ref[h:h + 1, :] = first[n - 1:n, :]
        hid = _gelu(prev + second + bias_ref[...])
        outs.append(_dot(hid.astype(BF16), w2_ref[...]))
    o_ref[0] = jnp.concatenate(outs, axis=1)


def _cmp_tokens(pool, page_table, wp, bias, w2b, transposed):
    b, n_pages = page_table.shape
    steps = n_pages // CMP_PAGES
    n = CMP_PAGES * SUBS

    def page_spec(k, h):
        return pl.BlockSpec((1, PAGE, 2 * HEAD_DIM), lambda i, s, pt: (pt[i, s * CMP_PAGES + k], 0, h))

    def page_spec_t(k):
        return pl.BlockSpec((1, KV_HEADS, 2 * HEAD_DIM, PAGE), lambda i, s, pt: (pt[i, s * CMP_PAGES + k], 0, 0, 0))

    scratch = [pltpu.VMEM((SUBLANES, 2 * CMP_HIDDEN), F32)]
    if transposed:
        scratch += [pltpu.VMEM((len(pages) * KV_HEADS, CMP_STRIDE * CMP_PITCH, 2 * HEAD_DIM), F32)
                    for pages in _cmp_group_pages()]
        page_specs = [page_spec_t(k) for k in range(CMP_PAGES)]
    else:
        page_specs = [page_spec(k, h) for k in range(CMP_PAGES) for h in range(KV_HEADS)]
    grid_spec = pltpu.PrefetchScalarGridSpec(
        num_scalar_prefetch=1,
        grid=(b, steps),
        in_specs=page_specs + [
            pl.BlockSpec(wp.shape, lambda i, s, pt: (0, 0, 0)),
            pl.BlockSpec(bias.shape, lambda i, s, pt: (0, 0)),
            pl.BlockSpec(w2b.shape, lambda i, s, pt: (0, 0))],
        out_specs=pl.BlockSpec((1, n, KV_WIDTH), lambda i, s, pt: (i, s, 0)),
        scratch_shapes=scratch,
    )
    return pl.pallas_call(
        functools.partial(_cmp_tokens_kernel, transposed=transposed),
        grid_spec=grid_spec,
        out_shape=jax.ShapeDtypeStruct((b, steps * n, KV_WIDTH), F32),
        compiler_params=_cparams("parallel", "arbitrary"),
        name="cmp_tokens_t" if transposed else "cmp_tokens",
    )(page_table, *([pool] * len(page_specs)), wp, bias, w2b)


def _cover_matrix(n_tok, n_sel):
    i = np.arange(n_tok)[:, None]
    start = (i - 1) * CMP_STRIDE
    sj = np.arange(n_sel)[None, :] * SEL_BLOCK
    cov = (start < sj + SEL_BLOCK) & (start + CMP_BLOCK > sj) & (i >= 1)
    return cov.astype(np.float32)


KT_UNROLL = 2


def _split_hi_lo(x):
    hi = x.astype(BF16)
    lo = (x - hi.astype(F32)).astype(BF16)
    return hi, lo


def _nsa_prompt_kernel(q_ref, ckv_ref, kslc_ref, kwin_ref, gates_ref, cover_ref, o_ref, sel_ref, acc_ref,
                       s_ref, p_ref, *, tq, tk):
    q0 = pl.program_id(1) * tq
    qt = (q_ref[0] * SCALE).T
    gt = jax.nn.sigmoid(gates_ref[0]).T
    qpos = q0 + lax.broadcasted_iota(jnp.int32, (1, tq), 1)
    lane = lax.broadcasted_iota(jnp.int32, (1, GROUP * tq), 1)
    n_cmp = ckv_ref.shape[1]
    n_sel = cover_ref.shape[0]
    per_tile = tk // SEL_BLOCK
    wide = GROUP * tq

    def tile4(x):
        return jnp.concatenate([x] * GROUP, axis=1)

    ik = lax.broadcasted_iota(jnp.int32, (tk, 1), 0)
    rel = ik - lax.broadcasted_iota(jnp.int32, (1, tq), 1)
    row64 = lax.broadcasted_iota(jnp.int32, (HEAD_DIM, 1), 0)
    lane128 = lax.broadcasted_iota(jnp.int32, (1, 2 * HEAD_DIM), 1)
    kfeat = jnp.where((lane128 == HEAD_DIM) | (lane128 == HEAD_DIM + 1), ik.astype(F32), 0.0)
    q4, slope4, slope2, qaug = [], [], [], []
    for h in range(KV_HEADS):
        qf = jnp.concatenate([qt[(h * GROUP + g) * HEAD_DIM:(h * GROUP + g + 1) * HEAD_DIM, :]
                              for g in range(GROUP)], axis=1)
        q4.append(qf.astype(BF16))
        sl = jnp.zeros((1, wide), F32)
        for g in range(GROUP):
            sl = jnp.where((lane >= g * tq) & (lane < (g + 1) * tq), SLOPES[h * GROUP + g], sl)
        slope4.append(sl)
        sl2 = sl * LOG2E
        slope2.append(sl2)
        sl_hi = sl2.astype(BF16).astype(F32)
        extra = jnp.where(row64 == 0, sl_hi, jnp.where(row64 == 1, sl2 - sl_hi, 0.0))
        qaug.append(jnp.concatenate([qf * LOG2E, extra], axis=0).astype(BF16))

    o_cmp = []
    slot = lax.broadcasted_iota(jnp.int32, (n_cmp, 1), 0)
    dist = qpos - ((slot + 1) * CMP_STRIDE - 1)
    cmask = tile4(((dist >= 0) & (slot >= 1)).astype(F32))
    cdist4 = tile4(dist.astype(F32))
    cov = cover_ref[...].astype(BF16)
    jj = lax.broadcasted_iota(jnp.int32, (n_sel, 1), 0)
    cur = qpos // SEL_BLOCK
    forced = (jj == 0) | (jj == cur) | (jj == cur - 1)
    valid = jj * SEL_BLOCK <= qpos
    for h in range(KV_HEADS):
        ckv = ckv_ref[0, :, h * 2 * HEAD_DIM:(h + 1) * 2 * HEAD_DIM]
        ck = ckv[:, :HEAD_DIM].astype(BF16)
        cvt = ckv.T[HEAD_DIM:, :].astype(BF16)
        s = _dot(ck, q4[h]) - slope4[h] * cdist4
        s = jnp.where(cmask > 0, s, NEG)
        p = jnp.exp(s - jnp.max(s, axis=0, keepdims=True)) * cmask
        p = p / jnp.maximum(jnp.sum(p, axis=0, keepdims=True), 1e-30)
        o_cmp.append(_dot(cvt, p.astype(BF16)))
        psum = p[:, 0:tq]
        for g in range(1, GROUP):
            psum = psum + p[:, g * tq:(g + 1) * tq]
        p_hi, p_lo = _split_hi_lo(psum)
        imp = _dot(cov, p_hi) + _dot(cov, p_lo)
        imp = jnp.where(forced, BIG, jnp.where(valid, imp, NEG))
        cnt = jnp.zeros((n_sel, tq), F32)
        for j2 in range(n_sel):
            r = imp[j2:j2 + 1, :]
            cnt = cnt + jnp.where(r > imp, 1.0, jnp.where((r == imp) & (j2 < jj), 1.0, 0.0))
        sel = jnp.where(cnt < SEL_TOP, BIG, NEG)
        for kk in range(n_sel // per_tile):
            sel_ref[h, kk] = sel[kk * per_tile:(kk + 1) * per_tile, :]

    half = ik // SEL_BLOCK
    ones_rows = jnp.ones((SUBLANES, tk), F32)

    def attend(kv_ref, lo, hi, cap_fn):
        for h in range(KV_HEADS):
            acc_ref[h] = jnp.zeros((HEAD_DIM + SUBLANES, wide), F32)

        def body(i, ms):
            ms = list(ms)
            tiles = []
            for sub in range(KT_UNROLL):
                kt_raw = lo + KT_UNROLL * i + sub
                kt = jnp.minimum(kt_raw, hi - 1)
                k0 = pl.multiple_of(kt * tk, tk)
                d = jnp.where(kt_raw < hi, (q0 - k0) - rel, -1)
                tiles.append((kt, k0, d))
                for h in range(KV_HEADS):
                    kv = kv_ref[0, pl.ds(k0, tk), h * 2 * HEAD_DIM:(h + 1) * 2 * HEAD_DIM]
                    k_aug = jnp.where(lane128 < HEAD_DIM, kv, kfeat).astype(BF16)
                    s_ref[sub, h] = _dot(k_aug, qaug[h])
            for sub, (kt, k0, d) in enumerate(tiles):
                k0f = k0.astype(F32)
                alphas = []
                for h in range(KV_HEADS):
                    sc = jnp.minimum(s_ref[sub, h], tile4(cap_fn(h, kt, d)))
                    c = slope2[h] * k0f
                    m_new = jnp.maximum(ms[h], jnp.max(sc, axis=0, keepdims=True) + c)
                    p_ref[sub, h] = jnp.exp2(sc - (m_new - c)).astype(BF16)
                    alphas.append(jnp.exp2(ms[h] - m_new))
                    ms[h] = m_new
                for h in range(KV_HEADS):
                    kv = kv_ref[0, pl.ds(k0, tk), h * 2 * HEAD_DIM:(h + 1) * 2 * HEAD_DIM]
                    vt = jnp.concatenate([kv.T[HEAD_DIM:, :], ones_rows], axis=0).astype(BF16)
                    acc_ref[h] = alphas[h] * acc_ref[h] + _dot(vt, p_ref[sub, h])
            return tuple(ms)

        trips = (hi - lo + KT_UNROLL - 1) // KT_UNROLL
        lax.fori_loop(0, trips, body, tuple(jnp.full((1, wide), NEG, F32) for _ in range(KV_HEADS)))
        return [acc_ref[h, :HEAD_DIM, :] / jnp.maximum(acc_ref[h, HEAD_DIM:HEAD_DIM + 1, :], 1e-30)
                for h in range(KV_HEADS)]

    def slc_cap(h, kt, d):
        rows = sel_ref[h, kt]
        cap = rows[0:1, :]
        for r in range(1, per_tile):
            cap = jnp.where(half == r, rows[r:r + 1, :], cap)
        return jnp.where(d >= 0, cap, NEG)

    def win_cap(h, kt, d):
        return jnp.where(d >= 0, jnp.where(d < WINDOW, BIG, NEG), NEG)

    hi = (q0 + tq + tk - 1) // tk
    o_slc = attend(kslc_ref, 0, hi, slc_cap)
    o_win = attend(kwin_ref, jnp.maximum(q0 - WINDOW, 0) // tk, hi, win_cap)

    out_rows = []
    for h in range(KV_HEADS):
        for g in range(GROUP):
            hh = h * GROUP + g
            sl = slice(g * tq, (g + 1) * tq)
            out_rows.append(gt[hh:hh + 1, :] * o_cmp[h][:, sl]
                            + gt[NSA_HEADS + hh:NSA_HEADS + hh + 1, :] * o_slc[h][:, sl]
                            + gt[2 * NSA_HEADS + hh:2 * NSA_HEADS + hh + 1, :] * o_win[h][:, sl])
    o_ref[0] = jnp.concatenate(out_rows, axis=0).T


def _nsa_prompt_attn(q, ckv, kv_slc, kv_win, gates, tq=128, tk=128):
    b, t, _ = q.shape
    n_cmp = ckv.shape[1]
    n_sel = t // SEL_BLOCK
    cover_t = jnp.asarray(_cover_matrix(n_cmp, n_sel).T)
    kern = functools.partial(_nsa_prompt_kernel, tq=tq, tk=tk)
    return pl.pallas_call(
        kern,
        grid=(b, t // tq),
        in_specs=[pl.BlockSpec((1, tq, NSA_WIDTH), lambda i, j: (i, j, 0)),
                  pl.BlockSpec((1, n_cmp, KV_WIDTH), lambda i, j: (i, 0, 0)),
                  pl.BlockSpec((1, t, KV_WIDTH), lambda i, j: (i, 0, 0)),
                  pl.BlockSpec((1, t, KV_WIDTH), lambda i, j: (i, 0, 0)),
                  pl.BlockSpec((1, tq, LANES), lambda i, j: (i, j, 0)),
                  pl.BlockSpec((n_sel, n_cmp), lambda i, j: (0, 0))],
        out_specs=pl.BlockSpec((1, tq, NSA_WIDTH), lambda i, j: (i, j, 0)),
        out_shape=jax.ShapeDtypeStruct((b, t, NSA_WIDTH), F32),
        scratch_shapes=[pltpu.VMEM((KV_HEADS, n_sel * SEL_BLOCK // tk, tk // SEL_BLOCK, tq), F32),
                        pltpu.VMEM((KV_HEADS, HEAD_DIM + SUBLANES, GROUP * tq), F32),
                        pltpu.VMEM((KT_UNROLL, KV_HEADS, tk, GROUP * tq), F32),
                        pltpu.VMEM((KT_UNROLL, KV_HEADS, tk, GROUP * tq), BF16)],
        compiler_params=_cparams("parallel", "parallel"),
        name="nsa_prompt_attn",
    )(q, ckv, kv_slc, kv_win, gates, cover_t)


TS = SUBLANES
SEL_LANES = 384


def _nsa_sample_select_kernel(q_ref, ckv_ref, cover_ref, ocmp_ref, idx_ref, *, nbatch, past_len, n_sel):
    n_cmp = ckv_ref.shape[1]
    rowi = lax.broadcasted_iota(jnp.int32, (GROUP * TS, 1), 0)
    tpos = past_len + rowi % TS
    slot = lax.broadcasted_iota(jnp.int32, (1, n_cmp), 1)
    dist = tpos - ((slot + 1) * CMP_STRIDE - 1)
    mask = ((dist >= 0) & (slot >= 1)).astype(F32)
    jj = lax.broadcasted_iota(jnp.int32, (1, SEL_LANES), 1)
    jjf = jj.astype(F32)
    qp8 = past_len + lax.broadcasted_iota(jnp.int32, (TS, 1), 0)
    cur = qp8 // SEL_BLOCK
    forced = (jj == 0) | (jj == cur) | (jj == cur - 1)
    valid = jj * SEL_BLOCK <= qp8
    lane = lax.broadcasted_iota(jnp.int32, (1, LANES), 1)
    cov = cover_ref[...].astype(BF16)
    for bb in range(nbatch):
        q = q_ref[bb] * SCALE
        idx_out = jnp.zeros((TS, LANES), jnp.int32)
        heads = []
        for h in range(KV_HEADS):
            qh = jnp.concatenate([q[:, (h * GROUP + g) * HEAD_DIM:(h * GROUP + g + 1) * HEAD_DIM]
                                  for g in range(GROUP)], axis=0).astype(BF16)
            slope = jnp.zeros((GROUP * TS, 1), F32)
            for g in range(GROUP):
                slope = jnp.where(rowi // TS == g, SLOPES[h * GROUP + g], slope)
            ckv = ckv_ref[bb, :, h * 2 * HEAD_DIM:(h + 1) * 2 * HEAD_DIM]
            s = _dot_nt(qh, ckv[:, :HEAD_DIM].astype(BF16)) - slope * dist.astype(F32)
            s = jnp.where(mask > 0, s, NEG)
            p = jnp.exp(s - jnp.max(s, axis=-1, keepdims=True)) * mask
            p = p / jnp.maximum(jnp.sum(p, axis=-1, keepdims=True), 1e-30)
            o = _dot(p.astype(BF16), ckv[:, HEAD_DIM:].astype(BF16))
            heads += [o[g * TS:(g + 1) * TS, :] for g in range(GROUP)]
            psum = p[0:TS]
            for g in range(1, GROUP):
                psum = psum + p[g * TS:(g + 1) * TS]
            p_hi, p_lo = _split_hi_lo(psum)
            imp = _dot(p_hi, cov) + _dot(p_lo, cov)
            imp = jnp.where(forced, BIG, jnp.where(valid, imp, NEG))
            imp = jnp.where(jj < n_sel, imp, -3e38)
            for k in range(SEL_TOP):
                best = jnp.max(imp, axis=-1, keepdims=True)
                pick = jnp.min(jnp.where(imp == best, jjf, float(SEL_LANES)), axis=-1, keepdims=True)
                idx_out = jnp.where(lane == h * SEL_TOP + k, pick.astype(jnp.int32), idx_out)
                imp = jnp.where(jjf == pick, -3e38, imp)
        ocmp_ref[bb] = jnp.concatenate(heads, axis=-1)
        idx_ref[bb] = idx_out


def _nsa_sample_select(q, ckv, past_len, nbatch=4):
    b = q.shape[0]
    n_cmp = ckv.shape[1]
    n_sel = -(-(past_len + 4) // SEL_BLOCK)
    cov = np.zeros((n_cmp, SEL_LANES), np.float32)
    cov[:, :n_sel] = _cover_matrix(n_cmp, n_sel)
    kern = functools.partial(_nsa_sample_select_kernel, nbatch=nbatch, past_len=past_len, n_sel=n_sel)
    return pl.pallas_call(
        kern,
        grid=(b // nbatch,),
        in_specs=[pl.BlockSpec((nbatch, TS, NSA_WIDTH), lambda i: (i, 0, 0)),
                  pl.BlockSpec((nbatch, n_cmp, KV_WIDTH), lambda i: (i, 0, 0)),
                  pl.BlockSpec((n_cmp, SEL_LANES), lambda i: (0, 0))],
        out_specs=[pl.BlockSpec((nbatch, TS, NSA_WIDTH), lambda i: (i, 0, 0)),
                   pl.BlockSpec((nbatch, TS, LANES), lambda i: (i, 0, 0))],
        out_shape=[jax.ShapeDtypeStruct((b, TS, NSA_WIDTH), F32),
                   jax.ShapeDtypeStruct((b, TS, LANES), jnp.int32)],
        compiler_params=_cparams("parallel"),
        name="nsa_sample_select",
    )(q, ckv, jnp.asarray(cov))


def _nsa_sample_attend_kernel(idx_ref, pt_ref, q_ref, ocmp_ref, gates_ref, knew_ref, wbuf_ref, wnew_ref,
                              pool_ref, o_ref, buf_ref, sem, *, t_dec, past_len):
    b = pl.program_id(0)
    last_blk = past_len // SEL_BLOCK
    per_page = PAGE // SEL_BLOCK

    nb = pl.num_programs(0)
    slot = b % 2

    def block_of(t, h, k, bb=None):
        bb = b if bb is None else bb
        return idx_ref[((bb * t_dec + t) * KV_HEADS + h) * SEL_TOP + k]

    def page_copies(bb, sl):
        out = []
        for t in range(t_dec):
            for h in range(KV_HEADS):
                for k in range(SEL_TOP):
                    page = pt_ref[bb, jnp.minimum(block_of(t, h, k, bb), last_blk - 1) // per_page]
                    out.append(pltpu.make_async_copy(pool_ref.at[page, h], buf_ref.at[sl, t, h, k], sem.at[sl]))
        return out

    @pl.when(b == 0)
    def _():
        for cp in page_copies(b, slot):
            cp.start()

    @pl.when(b + 1 < nb)
    def _():
        for cp in page_copies(b + 1, 1 - slot):
            cp.start()

    for cp in page_copies(b, slot):
        cp.wait()

    def rows_to_t(x):
        return jnp.concatenate([x, jnp.zeros((PAGE - TS, 2 * HEAD_DIM), F32)], axis=0).T

    rowg = lax.broadcasted_iota(jnp.int32, (SUBLANES, 1), 0)
    slope_col = [jnp.zeros((SUBLANES, 1), F32) for _ in range(KV_HEADS)]
    for h in range(KV_HEADS):
        for g in range(GROUP):
            slope_col[h] = jnp.where(rowg == g, SLOPES[h * GROUP + g], slope_col[h])
    n_tok = SEL_TOP * PAGE
    lane = lax.broadcasted_iota(jnp.int32, (1, n_tok), 1)
    row = lane % PAGE
    row_half = row // SEL_BLOCK
    slot_of = lane // PAGE
    n_win = wbuf_ref.shape[-1]
    wlane = lax.broadcasted_iota(jnp.int32, (1, n_win + PAGE), 1)
    wpos = jnp.where(wlane < n_win, past_len - n_win + wlane, past_len + wlane - n_win)

    for h in range(KV_HEADS):
        hs = slice(h * 2 * HEAD_DIM, (h + 1) * 2 * HEAD_DIM)
        new_t = rows_to_t(knew_ref[0, :, hs])
        wnew_t = rows_to_t(wnew_ref[0, :, hs])
        kw_t = jnp.concatenate([wbuf_ref[0, h, 0], wnew_t[:HEAD_DIM]], axis=1).astype(BF16)
        vw_t = jnp.concatenate([wbuf_ref[0, h, 1], wnew_t[HEAD_DIM:]], axis=1).astype(BF16)
        for t in range(t_dec):
            qpos = past_len + t
            qh = (q_ref[0, t, h] * SCALE).astype(BF16)
            k_tiles, v_tiles = [], []
            tok = jnp.zeros((1, n_tok), jnp.int32)
            want_half = jnp.zeros((1, n_tok), jnp.int32)
            for k in range(SEL_TOP):
                blk = block_of(t, h, k)
                is_new = blk >= last_blk
                k_tiles.append(jnp.where(is_new, new_t[:HEAD_DIM], buf_ref[slot, t, h, k, 0]))
                v_tiles.append(jnp.where(is_new, new_t[HEAD_DIM:], buf_ref[slot, t, h, k, 1]))
                tok = jnp.where(slot_of == k, (blk // per_page) * PAGE + row, tok)
                want_half = jnp.where(slot_of == k, blk % per_page, want_half)
            k_t = jnp.concatenate(k_tiles, axis=1).astype(BF16)
            v_t = jnp.concatenate(v_tiles, axis=1).astype(BF16)
            d = qpos - tok
            msk = jnp.where(d >= 0, jnp.where(row_half == want_half, 1.0, 0.0), 0.0)
            s = _dot(qh, k_t) - slope_col[h] * d.astype(F32)
            s = jnp.where(msk > 0, s, NEG)
            p = jnp.exp(s - jnp.max(s, axis=-1, keepdims=True)) * msk
            p = p / jnp.maximum(jnp.sum(p, axis=-1, keepdims=True), 1e-30)
            o_slc = _dot_nt(p.astype(BF16), v_t)
            dw = qpos - wpos
            mw = jnp.where(dw >= 0, jnp.where(dw < WINDOW, 1.0, 0.0), 0.0)
            sw = _dot(qh, kw_t) - slope_col[h] * dw.astype(F32)
            sw = jnp.where(mw > 0, sw, NEG)
            pw = jnp.exp(sw - jnp.max(sw, axis=-1, keepdims=True)) * mw
            pw = pw / jnp.maximum(jnp.sum(pw, axis=-1, keepdims=True), 1e-30)
            o_win = _dot_nt(pw.astype(BF16), vw_t)
            gts = jax.nn.sigmoid(gates_ref[0, t, h])
            o_ref[0, t, h] = (gts[:, 0:1] * ocmp_ref[0, t, h] + gts[:, 1:2] * o_slc + gts[:, 2:3] * o_win)


def _nsa_sample_attend(idx, page_table, q5, ocmp5, gates5, kv_slc_new, win_buf_t, kv_win_new, pool_slc_t,
                       t_dec, past_len):
    b = q5.shape[0]
    n_win = win_buf_t.shape[-1]
    blk5 = (1, t_dec, KV_HEADS, SUBLANES, HEAD_DIM)
    grid_spec = pltpu.PrefetchScalarGridSpec(
        num_scalar_prefetch=2,
        grid=(b,),
        in_specs=[pl.BlockSpec(blk5, lambda i, *_: (i, 0, 0, 0, 0)),
                  pl.BlockSpec(blk5, lambda i, *_: (i, 0, 0, 0, 0)),
                  pl.BlockSpec((1, t_dec, KV_HEADS, SUBLANES, LANES), lambda i, *_: (i, 0, 0, 0, 0)),
                  pl.BlockSpec((1, TS, KV_WIDTH), lambda i, *_: (i, 0, 0)),
                  pl.BlockSpec((1, KV_HEADS, 2, HEAD_DIM, n_win), lambda i, *_: (i, 0, 0, 0, 0)),
                  pl.BlockSpec((1, TS, KV_WIDTH), lambda i, *_: (i, 0, 0)),
                  pl.BlockSpec(memory_space=pl.ANY)],
        out_specs=pl.BlockSpec(blk5, lambda i, *_: (i, 0, 0, 0, 0)),
        scratch_shapes=[pltpu.VMEM((2, t_dec, KV_HEADS, SEL_TOP, 2, HEAD_DIM, PAGE), F32),
                        pltpu.SemaphoreType.DMA((2,))],
    )
    kern = functools.partial(_nsa_sample_attend_kernel, t_dec=t_dec, past_len=past_len)
    return pl.pallas_call(
        kern,
        grid_spec=grid_spec,
        out_shape=jax.ShapeDtypeStruct((b, t_dec, KV_HEADS, SUBLANES, HEAD_DIM), F32),
        compiler_params=_cparams("arbitrary"),
        name="nsa_sample_attend",
    )(idx[:, :t_dec, :KV_HEADS * SEL_TOP].reshape(-1), page_table, q5, ocmp5, gates5, kv_slc_new, win_buf_t,
      kv_win_new, pool_slc_t)


NSA_SPLITS = ((0, NSA_WIDTH), (NSA_WIDTH, KV_WIDTH), (NSA_WIDTH + KV_WIDTH, KV_WIDTH),
              (NSA_WIDTH + 2 * KV_WIDTH, KV_WIDTH), (NSA_WIDTH + 3 * KV_WIDTH, MEM_WIDTH),
              (NSA_WIDTH + 3 * KV_WIDTH + MEM_WIDTH, LANES))


def _nsa_in_weight(w_in):
    o = NSA_WIDTH + 3 * KV_WIDTH
    n_gate = 3 * NSA_HEADS
    gates = jnp.pad(w_in[:, o:o + n_gate], ((0, 0), (0, LANES - n_gate)))
    return jnp.concatenate([w_in[:, :o], w_in[:, o + n_gate:], gates], axis=1).astype(BF16)


def _to_heads5(x, t_dec):
    b = x.shape[0]
    w = x.shape[-1] // NSA_HEADS
    x = x[:, :t_dec].reshape(b, t_dec, KV_HEADS, GROUP, w)
    return jnp.pad(x, ((0, 0), (0, 0), (0, 0), (0, SUBLANES - GROUP), (0, 0)))


def kernel(x_prompt, x_sample, mem_prompt, state_ssm_re, state_ssm_im, cache_cmp_kv, cache_slc_kv, cache_win_kv, cache_mem_kv, page_table, norm_mix_pre, norm_mix_post, norm_ffn_pre, norm_ffn_post, w_out, w_mem_kv, w_ffn_in, w_ffn_out, ssm_w_in, ssm_a_re, ssm_a_im, ssm_log_dt, ssm_b_re, ssm_b_im, ssm_c_re, ssm_c_im, ssm_d, ssm_w_glu, ssm_b_glu, nsa_w_in, nsa_cmp_pe, nsa_cmp_w1, nsa_cmp_b1, nsa_cmp_w2):
    bp, seq, d = x_prompt.shape
    bs, t_dec, _ = x_sample.shape
    n_mem = mem_prompt.shape[1]
    depth = w_out.shape[0]
    past_len = page_table.shape[1] * PAGE
    d_ff = w_ffn_out.shape[1]
    chunk = 16

    xp = x_prompt.reshape(bp * seq, d)
    xs = jnp.pad(x_sample, ((0, 0), (0, TS - t_dec), (0, 0))).reshape(bs * TS, d)
    mem2d = mem_prompt.reshape(bp * n_mem, d)

    outs = {k: [] for k in ('ssm_re_p', 'ssm_im_p', 'ssm_re_s', 'ssm_im_s', 'cmp_p', 'slc_p', 'win_p',
                            'cmp_s', 'slc_s', 'win_s', 'mem_p')}
    for i in range(depth):
        j = i // 2
        mem_split = ((0, 2 * MEM_WIDTH),)
        mkv_p, mkv_p_t = _norm_proj(mem2d, norm_mix_pre[i], w_mem_kv[i].astype(BF16), mem_split, do_norm=False,
                                    t_splits=mem_split, seq=n_mem)
        mkv_p = mkv_p.reshape(bp, n_mem, 2 * MEM_WIDTH)
        outs['mem_p'].append(jnp.moveaxis(mkv_p_t.reshape(bp, MEM_HEADS, 2, HEAD_DIM, n_mem), -1, 1))
        mkv_s_t = _kv_cache_t(cache_mem_kv[i])
        if i % 2 == 0:
            w_in = ssm_w_in[j].astype(BF16)
            wglu = ssm_w_glu[j].astype(BF16)
            ssm_w = (ssm_a_re[j], ssm_a_im[j], ssm_log_dt[j], ssm_b_re[j], ssm_b_im[j], ssm_c_re[j], ssm_c_im[j])
            ssm_split = ((0, SSM_WIDTH), (SSM_WIDTH, MEM_WIDTH))
            u, qm = _norm_proj(xp, norm_mix_pre[i], w_in, ssm_split)
            zero = jnp.zeros((bp, SSM_GROUPS, SSM_STATE), F32)
            y, hlr, hli = _s5_scan_octets(u.reshape(bp, seq, SSM_WIDTH), zero, zero,
                                          _s5_operators(*ssm_w, chunk), chunk)
            m1p = _s5_glu(y.reshape(bp * seq, SSM_WIDTH), u, ssm_d[j], wglu, ssm_b_glu[j])
            m2p = _mem_attn(qm.reshape(bp, seq, MEM_WIDTH), mkv_p).reshape(bp * seq, MEM_WIDTH)
            outs['ssm_re_p'].append(hlr)
            outs['ssm_im_p'].append(hli)
            u, qm = _norm_proj(xs, norm_mix_pre[i], w_in, ssm_split)
            y, hlr, hli = _s5_scan(u.reshape(bs, TS, SSM_WIDTH)[:, :t_dec], state_ssm_re[j].astype(F32),
                                   state_ssm_im[j].astype(F32), _s5_operators(*ssm_w, t_dec), t_dec)
            y = jnp.pad(y, ((0, 0), (0, TS - t_dec), (0, 0)))
            m1s = _s5_glu(y.reshape(bs * TS, SSM_WIDTH), u, ssm_d[j], wglu, ssm_b_glu[j])
            m2s = _mem_attn_t(qm.reshape(bs, TS, MEM_WIDTH), mkv_s_t).reshape(bs * TS, MEM_WIDTH)
            outs['ssm_re_s'].append(hlr)
            outs['ssm_im_s'].append(hli)
        else:
            w_in = _nsa_in_weight(nsa_w_in[j])
            wp, cbias, w2b = _cmp_weights(nsa_cmp_pe[j], nsa_cmp_w1[j], nsa_cmp_b1[j], nsa_cmp_w2[j])
            kv_shape = (KV_HEADS, 2, HEAD_DIM)
            q, kc, ks, kw, qm, gates, kc_t, ks_t, kw_t = _norm_proj(
                xp, norm_mix_pre[i], w_in, NSA_SPLITS, t_splits=NSA_SPLITS[1:4], seq=seq)
            kc3, ks3, kw3 = (a.reshape(bp, seq, KV_WIDTH) for a in (kc, ks, kw))

            def rows_major(a_t):
                return jnp.moveaxis(a_t.reshape((bp,) + kv_shape + (a_t.shape[-1],)), -1, 1)
            ident = jnp.arange(bp * (seq // PAGE), dtype=jnp.int32).reshape(bp, seq // PAGE)
            ckv = _cmp_tokens(kc.reshape(bp * seq // PAGE, PAGE, KV_WIDTH), ident, wp, cbias, w2b, False)
            m1p = _nsa_prompt_attn(q.reshape(bp, seq, NSA_WIDTH), ckv, ks3, kw3,
                                   gates.reshape(bp, seq, LANES)).reshape(bp * seq, NSA_WIDTH)
            m2p = _mem_attn(qm.reshape(bp, seq, MEM_WIDTH), mkv_p).reshape(bp * seq, MEM_WIDTH)
            outs['cmp_p'].append(rows_major(kc_t))
            outs['slc_p'].append(rows_major(ks_t))
            outs['win_p'].append(rows_major(kw_t[:, :, seq - min(WINDOW, seq):]))
            q, kc, ks, kw, qm, gates = _norm_proj(xs, norm_mix_pre[i], w_in, NSA_SPLITS)
            kc3, ks3, kw3 = (a.reshape(bs, TS, KV_WIDTH) for a in (kc, ks, kw))
            pool_cmp_t = _kv_cache_t(cache_cmp_kv[j])
            pool_cmp_t = pool_cmp_t.reshape(pool_cmp_t.shape[0], KV_HEADS, 2 * HEAD_DIM, PAGE)
            pool_slc_t = _kv_cache_t(cache_slc_kv[j])
            win_buf_t = _kv_cache_t(cache_win_kv[j])
            ckv = _cmp_tokens(pool_cmp_t, page_table, wp, cbias, w2b, True)
            q3 = q.reshape(bs, TS, NSA_WIDTH)
            ocmp, idx = _nsa_sample_select(q3, ckv, past_len)
            gates3 = gates.reshape(bs, TS, LANES)[:, :, :3 * NSA_HEADS].reshape(bs, TS, 3, NSA_HEADS)
            gates5 = _to_heads5(gates3.transpose(0, 1, 3, 2).reshape(bs, TS, NSA_HEADS * 3), t_dec)
            gates5 = jnp.pad(gates5, ((0, 0),) * 4 + ((0, LANES - 3),))
            o5 = _nsa_sample_attend(idx, page_table, _to_heads5(q3, t_dec), _to_heads5(ocmp, t_dec), gates5,
                                    ks3, win_buf_t, kw3, pool_slc_t, t_dec, past_len)
            o = o5[:, :, :, :GROUP].reshape(bs, t_dec, NSA_WIDTH)
            m1s = jnp.pad(o, ((0, 0), (0, TS - t_dec), (0, 0))).reshape(bs * TS, NSA_WIDTH)
            m2s = _mem_attn_t(qm.reshape(bs, TS, MEM_WIDTH), mkv_s_t).reshape(bs * TS, MEM_WIDTH)
            outs['cmp_s'].append(kc3[:, :t_dec].reshape((bs, t_dec) + kv_shape))
            outs['slc_s'].append(ks3[:, :t_dec].reshape((bs, t_dec) + kv_shape))
            kv_w = jnp.concatenate([cache_win_kv[j], kw3[:, :t_dec].reshape((bs, t_dec) + kv_shape)], axis=1)
            n_keep = min(WINDOW, past_len + t_dec)
            outs['win_s'].append(kv_w[:, kv_w.shape[1] - n_keep:])
        wo = w_out[i].astype(BF16)
        wg = w_ffn_in[i][:, :d_ff].astype(BF16)
        wu = w_ffn_in[i][:, d_ff:].astype(BF16)
        wd = w_ffn_out[i].astype(BF16)
        w1 = m1p.shape[1]
        args = (wo[:w1], wo[w1:], norm_mix_post[i], norm_ffn_pre[i], wg, wu, wd, norm_ffn_post[i])
        xp = _post(xp, m1p, m2p, *args)
        xs = _post(xs, m1s, m2s, *args)

    st = lambda k: jnp.stack(outs[k])
    y_sample = xs.reshape(bs, TS, d)[:, :t_dec]
    return (xp.reshape(bp, seq, d), y_sample, st('ssm_re_p'), st('ssm_im_p'), st('ssm_re_s'), st('ssm_im_s'),
            st('cmp_p'), st('slc_p'), st('win_p'), st('cmp_s'), st('slc_s'), st('win_s'), st('mem_p'))
```

```python
import functools
import math

import numpy as np
import jax
import jax.numpy as jnp
from jax import lax
from jax.experimental import pallas as pl
from jax.experimental.pallas import tpu as pltpu

F32 = jnp.float32
BF16 = jnp.bfloat16

D_MODEL = 1024
PAGE = 128
MEM_HEADS = 4
HEAD_DIM = 64
MEM_WIDTH = MEM_HEADS * HEAD_DIM
SSM_WIDTH = D_MODEL - MEM_WIDTH
SSM_GROUP = 16
SSM_GROUPS = SSM_WIDTH // SSM_GROUP
SSM_STATE = 64
NSA_HEADS = 12
KV_HEADS = 3
GROUP = NSA_HEADS // KV_HEADS
NSA_WIDTH = NSA_HEADS * HEAD_DIM
KV_WIDTH = KV_HEADS * 2 * HEAD_DIM
CMP_BLOCK = 32
CMP_STRIDE = 16
CMP_HIDDEN = 2 * HEAD_DIM
SEL_BLOCK = 64
SEL_TOP = 16
WINDOW = 512
RMS_EPS = 1e-6
NEG = -1e30
BIG = 1e30
SCALE = HEAD_DIM ** -0.5
LANES = 128
SUBLANES = 8
VMEM_LIMIT = 56 * 1024 * 1024
SLOPES = [2.0 ** (-8.0 * (h + 1) / NSA_HEADS) for h in range(NSA_HEADS)]
LOG2E = math.log2(math.e)


def _cparams(*sem):
    return pltpu.CompilerParams(dimension_semantics=sem, vmem_limit_bytes=VMEM_LIMIT)


def _rms(x, g):
    return x * lax.rsqrt(jnp.mean(x * x, axis=-1, keepdims=True) + RMS_EPS) * g


def _gelu(x):
    return 0.5 * x * (1.0 + jnp.tanh(math.sqrt(2.0 / math.pi) * (x + 0.044715 * (x * x * x))))


def _dot(a, b):
    return jnp.dot(a, b, preferred_element_type=F32)


def _dot_nt(a, b):
    return lax.dot_general(a, b, (((1,), (1,)), ((), ())), preferred_element_type=F32)


def _kv_cache_t(cache):
    n = cache.ndim
    return jnp.moveaxis(cache, n - 4, n - 1)


def _norm_proj_kernel(x_ref, g_ref, w_ref, *o_refs, splits, t_splits, do_norm):
    x = x_ref[...]
    if do_norm:
        x = _rms(x, g_ref[...])
    z = _dot(x.astype(BF16), w_ref[...])
    for (start, width), o in zip(splits, o_refs):
        o[...] = z[:, start:start + width]
    for (start, width), o in zip(t_splits, o_refs[len(splits):]):
        o[0] = z[:, start:start + width].T


def _norm_proj(x, g, w, splits, do_norm=True, tm=512, t_splits=(), seq=None):
    rows, d = x.shape
    n = w.shape[1]
    tm = min(tm, rows, seq) if t_splits else min(tm, rows)
    kern = functools.partial(_norm_proj_kernel, splits=tuple(splits), t_splits=tuple(t_splits), do_norm=do_norm)
    out_specs = [pl.BlockSpec((tm, wd), lambda i: (i, 0)) for _, wd in splits]
    out_shape = [jax.ShapeDtypeStruct((rows, wd), F32) for _, wd in splits]
    if t_splits:
        per_seq = seq // tm
        out_specs += [pl.BlockSpec((1, wd, tm), lambda i: (i // per_seq, 0, i % per_seq)) for _, wd in t_splits]
        out_shape += [jax.ShapeDtypeStruct((rows // seq, wd, seq), F32) for _, wd in t_splits]
    return pl.pallas_call(
        kern,
        grid=(rows // tm,),
        in_specs=[pl.BlockSpec((tm, d), lambda i: (i, 0)),
                  pl.BlockSpec((1, d), lambda i: (0, 0)),
                  pl.BlockSpec((d, n), lambda i: (0, 0))],
        out_specs=out_specs,
        out_shape=out_shape,
        compiler_params=_cparams("parallel"),
        name="norm_proj",
    )(x, g.reshape(1, d), w)


def _softmax_rows(s):
    p = jnp.exp(s - jnp.max(s, axis=-1, keepdims=True))
    return p, jnp.sum(p, axis=-1, keepdims=True)


def _mem_attn_kernel(q_ref, kv_ref, o_ref):
    q = q_ref[0]
    kv = kv_ref[0]
    outs = []
    for h in range(MEM_HEADS):
        qh = (q[:, h * HEAD_DIM:(h + 1) * HEAD_DIM] * SCALE).astype(BF16)
        k = kv[:, h * 2 * HEAD_DIM:h * 2 * HEAD_DIM + HEAD_DIM].astype(BF16)
        v = kv[:, h * 2 * HEAD_DIM + HEAD_DIM:(h + 1) * 2 * HEAD_DIM].astype(BF16)
        p, l = _softmax_rows(_dot_nt(qh, k))
        outs.append(_dot(p.astype(BF16), v) / l)
    o_ref[0] = jnp.concatenate(outs, axis=-1)


def _mem_attn(qm, mem_kv, tm=512):
    b, t, _ = qm.shape
    tm = min(tm, t)
    n_mem = mem_kv.shape[1]
    return pl.pallas_call(
        _mem_attn_kernel,
        grid=(b, t // tm),
        in_specs=[pl.BlockSpec((1, tm, MEM_WIDTH), lambda i, j: (i, j, 0)),
                  pl.BlockSpec((1, n_mem, 2 * MEM_WIDTH), lambda i, j: (i, 0, 0))],
        out_specs=pl.BlockSpec((1, tm, MEM_WIDTH), lambda i, j: (i, j, 0)),
        out_shape=jax.ShapeDtypeStruct((b, t, MEM_WIDTH), F32),
        compiler_params=_cparams("parallel", "parallel"),
        name="mem_attn",
    )(qm, mem_kv)


def _mem_attn_t_kernel(q_ref, kv_ref, o_ref):
    q = q_ref[0]
    outs = []
    for h in range(MEM_HEADS):
        qh = (q[:, h * HEAD_DIM:(h + 1) * HEAD_DIM] * SCALE).astype(BF16)
        p, l = _softmax_rows(_dot(qh, kv_ref[0, h, 0].astype(BF16)))
        outs.append(_dot_nt(p.astype(BF16), kv_ref[0, h, 1].astype(BF16)) / l)
    o_ref[0] = jnp.concatenate(outs, axis=-1)


def _mem_attn_t(qm, mem_kv_t):
    b, t, _ = qm.shape
    n_mem = mem_kv_t.shape[-1]
    return pl.pallas_call(
        _mem_attn_t_kernel,
        grid=(b,),
        in_specs=[pl.BlockSpec((1, t, MEM_WIDTH), lambda i: (i, 0, 0)),
                  pl.BlockSpec((1, MEM_HEADS, 2, HEAD_DIM, n_mem), lambda i: (i, 0, 0, 0, 0))],
        out_specs=pl.BlockSpec((1, t, MEM_WIDTH), lambda i: (i, 0, 0)),
        out_shape=jax.ShapeDtypeStruct((b, t, MEM_WIDTH), F32),
        compiler_params=_cparams("parallel"),
        name="mem_attn_t",
    )(qm, mem_kv_t)


def _post_kernel(x_ref, m1_ref, m2_ref, wo1_ref, wo2_ref, g1_ref, g2_ref, wg_ref, wu_ref, wd_ref, g3_ref,
                 o_ref):
    a = _dot(m1_ref[...].astype(BF16), wo1_ref[...]) + _dot(m2_ref[...].astype(BF16), wo2_ref[...])
    x1 = x_ref[...] + _rms(a, g1_ref[...])
    h = _rms(x1, g2_ref[...]).astype(BF16)
    gate = _dot(h, wg_ref[...])
    up = _dot(h, wu_ref[...])
    act = (gate * jax.nn.sigmoid(gate) * up).astype(BF16)
    f = _dot(act, wd_ref[...])
    o_ref[...] = x1 + _rms(f, g3_ref[...])


def _post(x, m1, m2, wo1, wo2, g1, g2, wg, wu, wd, g3, tm=512):
    rows, d = x.shape
    tm = min(tm, rows)
    w1, w2 = m1.shape[1], m2.shape[1]
    dff = wg.shape[1]

    def const(shape):
        return pl.BlockSpec(shape, lambda i: (0, 0), pipeline_mode=pl.Buffered(1))

    return pl.pallas_call(
        _post_kernel,
        grid=(rows // tm,),
        in_specs=[pl.BlockSpec((tm, d), lambda i: (i, 0)),
                  pl.BlockSpec((tm, w1), lambda i: (i, 0)),
                  pl.BlockSpec((tm, w2), lambda i: (i, 0)),
                  const((w1, d)), const((w2, d)), const((1, d)), const((1, d)),
                  const((d, dff)), const((d, dff)), const((dff, d)), const((1, d))],
        out_specs=pl.BlockSpec((tm, d), lambda i: (i, 0)),
        out_shape=jax.ShapeDtypeStruct((rows, d), F32),
        compiler_params=_cparams("parallel"),
        name="post_ffn",
    )(x, m1, m2, wo1, wo2, g1.reshape(1, d), g2.reshape(1, d), wg, wu, wd, g3.reshape(1, d))


def _s5_operators(a_re, a_im, log_dt, b_re, b_im, c_re, c_im, q):
    hp = lax.Precision.HIGHEST
    a_re, a_im = a_re.astype(F32), a_im.astype(F32)
    dt = jnp.exp(log_dt.astype(F32))[:, None]
    mag = jnp.exp(a_re * dt)
    ab_r, ab_i = mag * jnp.cos(a_im * dt), mag * jnp.sin(a_im * dt)
    den = a_re * a_re + a_im * a_im
    nr, ni = ab_r - 1.0, ab_i
    f_r = (nr * a_re + ni * a_im) / den
    f_i = (ni * a_re - nr * a_im) / den
    bb_r = f_r[..., None] * b_re - f_i[..., None] * b_im
    bb_i = f_r[..., None] * b_im + f_i[..., None] * b_re
    pw_r, pw_i = [jnp.ones_like(ab_r)], [jnp.zeros_like(ab_i)]
    for _ in range(q):
        r, i = pw_r[-1], pw_i[-1]
        pw_r.append(r * ab_r - i * ab_i)
        pw_i.append(r * ab_i + i * ab_r)
    pw_r, pw_i = jnp.stack(pw_r), jnp.stack(pw_i)
    g = a_re.shape[0]
    pt_r, pt_i = pw_r.transpose(1, 2, 0)[..., None], pw_i.transpose(1, 2, 0)[..., None]
    w_r = pt_r * bb_r[:, :, None, :] - pt_i * bb_i[:, :, None, :]
    w_i = pt_r * bb_i[:, :, None, :] + pt_i * bb_r[:, :, None, :]
    c2 = jnp.concatenate([c_re, -c_im], axis=-1)
    w2 = jnp.concatenate([w_r, w_i], axis=1).reshape(g, 2 * SSM_STATE, (q + 1) * SSM_GROUP)
    kk = jnp.einsum('gip,gpn->gin', c2, w2, precision=hp).reshape(g, SSM_GROUP, q + 1, SSM_GROUP)
    tau = jnp.arange(q)[None, :] - jnp.arange(q)[:, None]
    kt = kk[:, :, jnp.clip(tau, 0, q), :]
    kt = jnp.where((tau >= 0)[None, None, :, :, None], kt, 0.0)
    toep = kt.transpose(0, 2, 4, 3, 1).reshape(g, q * SSM_GROUP, q * SSM_GROUP)
    rev = q - 1 - jnp.arange(q)
    bst_r = w_r[:, :, rev, :].transpose(0, 2, 3, 1).reshape(g, q * SSM_GROUP, SSM_STATE)
    bst_i = w_i[:, :, rev, :].transpose(0, 2, 3, 1).reshape(g, q * SSM_GROUP, SSM_STATE)
    ar, ai = pw_r[1:], pw_i[1:]
    co_r = c_re[None] * ar[:, :, None, :] - c_im[None] * ai[:, :, None, :]
    co_i = -c_re[None] * ai[:, :, None, :] - c_im[None] * ar[:, :, None, :]
    cout_r = co_r.transpose(1, 3, 0, 2).reshape(g, SSM_STATE, q * SSM_GROUP)
    cout_i = co_i.transpose(1, 3, 0, 2).reshape(g, SSM_STATE, q * SSM_GROUP)
    return (toep.astype(BF16), bst_r.astype(BF16), bst_i.astype(BF16), cout_r.astype(BF16),
            cout_i.astype(BF16), pw_r[q][:, None, :], pw_i[q][:, None, :], kk[:, :, :q, :].astype(BF16))


def _s5_scan_kernel(u_ref, toep_ref, bsr_ref, bsi_ref, cor_ref, coi_ref, aqr_ref, aqi_ref, h0r_ref, h0i_ref,
                    y_ref, hlr_ref, hli_ref, sr_ref, si_ref, hpr_ref, hpi_ref, *, gs, nc, nb):
    for g in range(gs):
        u = u_ref[g].astype(BF16)
        sr_ref[g] = _dot(u, bsr_ref[g])
        si_ref[g] = _dot(u, bsi_ref[g])

    def step(c, carry):
        rows = pl.ds(pl.multiple_of(c * nb, SUBLANES), nb)
        new = []
        for g in range(gs):
            hr, hi = carry[2 * g], carry[2 * g + 1]
            hpr_ref[g, rows, :] = hr
            hpi_ref[g, rows, :] = hi
            ar, ai = aqr_ref[g], aqi_ref[g]
            new.append(ar * hr - ai * hi + sr_ref[g, rows, :])
            new.append(ar * hi + ai * hr + si_ref[g, rows, :])
        return tuple(new)

    init = []
    for g in range(gs):
        init += [h0r_ref[g], h0i_ref[g]]
    fin = lax.fori_loop(0, nc, step, tuple(init))
    for g in range(gs):
        hlr_ref[g] = fin[2 * g]
        hli_ref[g] = fin[2 * g + 1]
        u = u_ref[g].astype(BF16)
        y_ref[g] = (_dot(u, toep_ref[g]) + _dot(hpr_ref[g].astype(BF16), cor_ref[g])
                    + _dot(hpi_ref[g].astype(BF16), coi_ref[g]))


def _s5_scan(u, h0r, h0i, ops, q, gs=4):
    b, l, _ = u.shape
    nc = l // q
    qw = q * SSM_GROUP
    toep, bsr, bsi, cor, coi, aqr, aqi = ops[:7]
    ug = u.reshape(b, nc, q, SSM_GROUPS, SSM_GROUP).transpose(3, 1, 0, 2, 4).reshape(SSM_GROUPS, nc * b, qw)
    h0r_g = h0r.transpose(1, 0, 2)
    h0i_g = h0i.transpose(1, 0, 2)
    rows = nc * b

    def gspec(r, c):
        return pl.BlockSpec((gs, r, c), lambda i: (i, 0, 0))

    kern = functools.partial(_s5_scan_kernel, gs=gs, nc=nc, nb=b)
    y, hlr, hli = pl.pallas_call(
        kern,
        grid=(SSM_GROUPS // gs,),
        in_specs=[gspec(rows, qw), gspec(qw, qw), gspec(qw, SSM_STATE), gspec(qw, SSM_STATE),
                  gspec(SSM_STATE, qw), gspec(SSM_STATE, qw), gspec(1, SSM_STATE), gspec(1, SSM_STATE),
                  gspec(b, SSM_STATE), gspec(b, SSM_STATE)],
        out_specs=[gspec(rows, qw), gspec(b, SSM_STATE), gspec(b, SSM_STATE)],
        out_shape=[jax.ShapeDtypeStruct((SSM_GROUPS, rows, qw), F32),
                   jax.ShapeDtypeStruct((SSM_GROUPS, b, SSM_STATE), F32),
                   jax.ShapeDtypeStruct((SSM_GROUPS, b, SSM_STATE), F32)],
        scratch_shapes=[pltpu.VMEM((gs, rows, SSM_STATE), F32) for _ in range(4)],
        compiler_params=_cparams("parallel"),
        name="s5_scan_groups",
    )(ug, toep, bsr, bsi, cor, coi, aqr, aqi, h0r_g, h0i_g)
    y = y.reshape(SSM_GROUPS, nc, b, q, SSM_GROUP).transpose(2, 1, 3, 0, 4).reshape(b, l, SSM_WIDTH)
    return y, hlr.transpose(1, 0, 2), hli.transpose(1, 0, 2)


OCT = LANES // SSM_GROUP
N_OCT = SSM_GROUPS // OCT
OCT_STATE = OCT * SSM_STATE


def _s5_octet_operators(ops, q):
    _, bsr, bsi, cor, coi, aqr, aqi, kk = ops
    qq = q * LANES
    dt = kk.dtype

    def spread(x, tile_np, keep_np):
        y = jnp.einsum('...k,kn->...n', x, jnp.asarray(tile_np, dt), preferred_element_type=F32)
        return (y * jnp.asarray(keep_np, F32)).astype(dt)

    grp_of_gj = np.arange(LANES) // SSM_GROUP
    grp_of_gp = np.arange(OCT_STATE) // SSM_STATE
    lag = kk.reshape(N_OCT, OCT, SSM_GROUP, q, SSM_GROUP)
    lag = lag.transpose(0, 3, 1, 4, 2).reshape(N_OCT, q, LANES, SSM_GROUP)
    k8 = spread(lag, np.tile(np.eye(SSM_GROUP), (1, OCT)), grp_of_gj[:, None] == grp_of_gj[None, :])

    def bst8(b):
        b = b.reshape(N_OCT, OCT, q, SSM_GROUP, SSM_STATE).transpose(0, 2, 1, 3, 4).reshape(N_OCT, qq, SSM_STATE)
        keep = np.tile(grp_of_gj, q)[:, None] == grp_of_gp[None, :]
        return spread(b, np.tile(np.eye(SSM_STATE), (1, OCT)), keep)

    def cout8(c):
        c = c.reshape(N_OCT, OCT_STATE, q * SSM_GROUP)
        t_of, i_of = np.arange(q * SSM_GROUP) // SSM_GROUP, np.arange(q * SSM_GROUP) % SSM_GROUP
        col_t, col_i = np.arange(qq) // LANES, np.arange(qq) % SSM_GROUP
        tile = (t_of[:, None] == col_t[None, :]) & (i_of[:, None] == col_i[None, :])
        keep = grp_of_gp[:, None] == np.tile(grp_of_gj, q)[None, :]
        return spread(c, tile.astype(np.float32), keep)

    return (k8, bst8(bsr), bst8(bsi), cout8(cor), cout8(coi),
            aqr.reshape(N_OCT, 1, OCT_STATE), aqi.reshape(N_OCT, 1, OCT_STATE))


def _s5_octet_kernel(x_ref, lag_ref, bsr_ref, bsi_ref, cor_ref, coi_ref, aqr_ref, aqi_ref, h0r_ref, h0i_ref,
                     y_ref, hlr_ref, hli_ref, sr_ref, si_ref, hpr_ref, hpi_ref, hr_ref, hi_ref, toep_ref,
                     *, ncb, nb, q):
    @pl.when(pl.program_id(1) == 0)
    def _():
        hr_ref[...] = h0r_ref[0]
        hi_ref[...] = h0i_ref[0]
        zero = jnp.zeros((LANES, LANES), BF16)
        for s in range(q):
            for t in range(q):
                toep_ref[s * LANES:(s + 1) * LANES, t * LANES:(t + 1) * LANES] = (
                    lag_ref[0, t - s] if t >= s else zero)

    x = x_ref[0]
    sr_ref[...] = _dot(x, bsr_ref[0])
    si_ref[...] = _dot(x, bsi_ref[0])
    ar, ai = aqr_ref[0], aqi_ref[0]

    def step(c, carry):
        hr, hi = carry
        rows = pl.ds(pl.multiple_of(c * nb, SUBLANES), nb)
        hpr_ref[rows, :] = hr
        hpi_ref[rows, :] = hi
        return ar * hr - ai * hi + sr_ref[rows, :], ar * hi + ai * hr + si_ref[rows, :]

    hr, hi = lax.fori_loop(0, ncb, step, (hr_ref[...], hi_ref[...]))
    hr_ref[...] = hr
    hi_ref[...] = hi
    hlr_ref[0] = hr
    hli_ref[0] = hi
    y_ref[0] = (_dot(x, toep_ref[...]) + _dot(hpr_ref[...].astype(BF16), cor_ref[0])
                + _dot(hpi_ref[...].astype(BF16), coi_ref[0]))


def _s5_scan_octets(u, h0r, h0i, ops, q, row_blocks=4):
    b, l, _ = u.shape
    nc = l // q
    qq = q * LANES
    rows = nc * b
    rb = rows // row_blocks
    ncb = nc // row_blocks
    ops8 = _s5_octet_operators(ops, q)
    x = u.reshape(b, nc, q, N_OCT, LANES).transpose(3, 1, 0, 2, 4).reshape(N_OCT, rows, qq).astype(BF16)
    h0r8 = h0r.reshape(b, N_OCT, OCT_STATE).transpose(1, 0, 2)
    h0i8 = h0i.reshape(b, N_OCT, OCT_STATE).transpose(1, 0, 2)

    def wspec(r, c):
        return pl.BlockSpec((1, r, c), lambda o, i: (o, 0, 0))

    kern = functools.partial(_s5_octet_kernel, ncb=ncb, nb=b, q=q)
    y, hlr, hli = pl.pallas_call(
        kern,
        grid=(N_OCT, row_blocks),
        in_specs=[pl.BlockSpec((1, rb, qq), lambda o, i: (o, i, 0)),
                  pl.BlockSpec((1, q, LANES, LANES), lambda o, i: (o, 0, 0, 0)),
                  wspec(qq, OCT_STATE), wspec(qq, OCT_STATE), wspec(OCT_STATE, qq),
                  wspec(OCT_STATE, qq), wspec(1, OCT_STATE), wspec(1, OCT_STATE),
                  wspec(b, OCT_STATE), wspec(b, OCT_STATE)],
        out_specs=[pl.BlockSpec((1, rb, qq), lambda o, i: (o, i, 0)), wspec(b, OCT_STATE), wspec(b, OCT_STATE)],
        out_shape=[jax.ShapeDtypeStruct((N_OCT, rows, qq), F32),
                   jax.ShapeDtypeStruct((N_OCT, b, OCT_STATE), F32),
                   jax.ShapeDtypeStruct((N_OCT, b, OCT_STATE), F32)],
        scratch_shapes=[pltpu.VMEM((rb, OCT_STATE), F32) for _ in range(4)]
        + [pltpu.VMEM((b, OCT_STATE), F32) for _ in range(2)] + [pltpu.VMEM((qq, qq), BF16)],
        compiler_params=_cparams("parallel", "arbitrary"),
        name="s5_scan_octets",
    )(x, *ops8, h0r8, h0i8)
    y = y.reshape(N_OCT, nc, b, q, LANES).transpose(2, 1, 3, 0, 4).reshape(b, l, SSM_WIDTH)
    unpack = lambda h: h.transpose(1, 0, 2).reshape(b, SSM_GROUPS, SSM_STATE)
    return y, unpack(hlr), unpack(hli)


def _s5_glu_kernel(y_ref, u_ref, d_ref, w_ref, b_ref, o_ref):
    v = _gelu(y_ref[...] + d_ref[...] * u_ref[...])
    o_ref[...] = v * jax.nn.sigmoid(_dot(v.astype(BF16), w_ref[...]) + b_ref[...])


def _s5_glu(y, u, d, w, bias, tm=512):
    rows, n = y.shape
    tm = min(tm, rows)
    return pl.pallas_call(
        _s5_glu_kernel,
        grid=(rows // tm,),
        in_specs=[pl.BlockSpec((tm, n), lambda i: (i, 0)), pl.BlockSpec((tm, n), lambda i: (i, 0)),
                  pl.BlockSpec((1, n), lambda i: (0, 0)), pl.BlockSpec((n, n), lambda i: (0, 0)),
                  pl.BlockSpec((1, n), lambda i: (0, 0))],
        out_specs=pl.BlockSpec((tm, n), lambda i: (i, 0)),
        out_shape=jax.ShapeDtypeStruct((rows, n), F32),
        compiler_params=_cparams("parallel"),
        name="s5_glu",
    )(y, u, d.reshape(1, n), w, bias.reshape(1, n))


CMP_PAGES = 16
CMP_PITCH = PAGE // CMP_STRIDE + 1
CMP_GROUP_SIZES = (4, 6, 6)


def _cmp_group_pages():
    out, start = [], 0
    for size in CMP_GROUP_SIZES:
        out.append(range(start, start + size))
        start += size
    return out
SUBS = PAGE // CMP_STRIDE


def _cmp_weights(pe, w1, b1, w2):
    w = w1.reshape(2, CMP_STRIDE // 2, 2, 2, HEAD_DIM, CMP_HIDDEN)
    eye = jnp.eye(2, dtype=w1.dtype)
    wp = jnp.einsum('apjcef,cd->pjcedaf', w, eye)
    wp = wp.reshape(CMP_STRIDE // 2, 2 * 2 * HEAD_DIM, 2 * 2 * CMP_HIDDEN)
    bias = (jnp.einsum('jce,jcef->cf', pe, w1, precision=lax.Precision.HIGHEST) + b1).reshape(1, 2 * CMP_HIDDEN)
    w2b = jnp.einsum('cfe,cd->cfde', w2, eye).reshape(2 * CMP_HIDDEN, 2 * HEAD_DIM)
    return wp.astype(BF16), bias.astype(F32), w2b.astype(BF16)


def _cmp_tokens_kernel(pt_ref, *refs, transposed):
    n = CMP_PAGES * SUBS
    if transposed:
        nblk = CMP_PAGES
        page_refs = refs[:nblk]
        wp_ref, bias_ref, w2_ref, o_ref, carry_ref = refs[nblk:nblk + 5]
        xs_refs = refs[nblk + 5:]
        group_pages = _cmp_group_pages()
        where = {k: (gi, (k - pages[0]) * KV_HEADS) for gi, pages in enumerate(group_pages) for k in pages}

        def stage(pages):
            for k in pages:
                gi, base = where[k]
                for h in range(KV_HEADS):
                    x = page_refs[k][0, h].T
                    for v in range(PAGE // SUBLANES):
                        n, j0 = divmod(v * SUBLANES, CMP_STRIDE)
                        xs_refs[gi][base + h, pl.ds(j0 * CMP_PITCH + n, SUBLANES, stride=CMP_PITCH), :] = (
                            x[v * SUBLANES:(v + 1) * SUBLANES, :])

        def rows_of(i, j):
            gi, base = where[i // KV_HEADS]
            return xs_refs[gi][base + i % KV_HEADS, pl.ds(j * CMP_PITCH, SUBS), :]
    else:
        nblk = CMP_PAGES * KV_HEADS
        page_refs = refs[:nblk]
        wp_ref, bias_ref, w2_ref, o_ref, carry_ref = refs[nblk:]

        def rows_of(i, j):
            return page_refs[i][0, pl.ds(j, SUBS, stride=CMP_STRIDE), :]

    @pl.when(pl.program_id(1) == 0)
    def _():
        carry_ref[...] = jnp.zeros_like(carry_ref)

    accs = []
    for pages in (_cmp_group_pages() if transposed else [range(CMP_PAGES)]):
        ng = len(pages) * SUBS
        if transposed:
            stage(pages)
        acc = None
        for jp in range(CMP_STRIDE // 2):
            halves = []
            for jj in range(2):
                j = 2 * jp + jj
                rows = [rows_of(k * KV_HEADS + h, j) for h in range(KV_HEADS) for k in pages]
                halves.append(jnp.concatenate(rows, axis=0))
            lhs = jnp.concatenate(halves, axis=1).astype(BF16)
            part = _dot(lhs, wp_ref[jp])
            acc = part if acc is None else acc + part
        accs.append(acc)
    row = lax.broadcasted_iota(jnp.int32, (n, 1), 0)
    outs = []
    for h in range(KV_HEADS):
        p = jnp.concatenate([a[h * (a.shape[0] // KV_HEADS):(h + 1) * (a.shape[0] // KV_HEADS)] for a in accs],
                            axis=0)
        first = jnp.concatenate([p[:, 0:CMP_HIDDEN], p[:, 2 * CMP_HIDDEN:3 * CMP_HIDDEN]], axis=1)
        second = jnp.concatenate([p[:, CMP_HIDDEN:2 * CMP_HIDDEN], p[:, 3 * CMP_HIDDEN:]], axis=1)
        prev = jnp.where(row == 0, carry_ref[h:h + 1, :], pltpu.roll(first, 1, axis=0))
        carry_ref[h:h + 1, :] = first[n - 1:n, :]
        hid = _gelu(prev + second + bias_ref[...])
        outs.append(_dot(hid.astype(BF16), w2_ref[...]))
    o_ref[0] = jnp.concatenate(outs, axis=1)


def _cmp_tokens(pool, page_table, wp, bias, w2b, transposed):
    b, n_pages = page_table.shape
    steps = n_pages // CMP_PAGES
    n = CMP_PAGES * SUBS

    def page_spec(k, h):
        return pl.BlockSpec((1, PAGE, 2 * HEAD_DIM), lambda i, s, pt: (pt[i, s * CMP_PAGES + k], 0, h))

    def page_spec_t(k):
        return pl.BlockSpec((1, KV_HEADS, 2 * HEAD_DIM, PAGE), lambda i, s, pt: (pt[i, s * CMP_PAGES + k], 0, 0, 0))

    scratch = [pltpu.VMEM((SUBLANES, 2 * CMP_HIDDEN), F32)]
    if transposed:
        scratch += [pltpu.VMEM((len(pages) * KV_HEADS, CMP_STRIDE * CMP_PITCH, 2 * HEAD_DIM), F32)
                    for pages in _cmp_group_pages()]
        page_specs = [page_spec_t(k) for k in range(CMP_PAGES)]
    else:
        page_specs = [page_spec(k, h) for k in range(CMP_PAGES) for h in range(KV_HEADS)]
    grid_spec = pltpu.PrefetchScalarGridSpec(
        num_scalar_prefetch=1,
        grid=(b, steps),
        in_specs=page_specs + [
            pl.BlockSpec(wp.shape, lambda i, s, pt: (0, 0, 0)),
            pl.BlockSpec(bias.shape, lambda i, s, pt: (0, 0)),
            pl.BlockSpec(w2b.shape, lambda i, s, pt: (0, 0))],
        out_specs=pl.BlockSpec((1, n, KV_WIDTH), lambda i, s, pt: (i, s, 0)),
        scratch_shapes=scratch,
    )
    return pl.pallas_call(
        functools.partial(_cmp_tokens_kernel, transposed=transposed),
        grid_spec=grid_spec,
        out_shape=jax.ShapeDtypeStruct((b, steps * n, KV_WIDTH), F32),
        compiler_params=_cparams("parallel", "arbitrary"),
        name="cmp_tokens_t" if transposed else "cmp_tokens",
    )(page_table, *([pool] * len(page_specs)), wp, bias, w2b)


def _cover_matrix(n_tok, n_sel):
    i = np.arange(n_tok)[:, None]
    start = (i - 1) * CMP_STRIDE
    sj = np.arange(n_sel)[None, :] * SEL_BLOCK
    cov = (start < sj + SEL_BLOCK) & (start + CMP_BLOCK > sj) & (i >= 1)
    return cov.astype(np.float32)


KT_UNROLL = 2


def _split_hi_lo(x):
    hi = x.astype(BF16)
    lo = (x - hi.astype(F32)).astype(BF16)
    return hi, lo


def _nsa_prompt_kernel(q_ref, ckv_ref, kslc_ref, kwin_ref, gates_ref, cover_ref, o_ref, sel_ref, acc_ref,
                       s_ref, p_ref, *, tq, tk):
    q0 = pl.program_id(1) * tq
    qt = (q_ref[0] * SCALE).T
    gt = jax.nn.sigmoid(gates_ref[0]).T
    qpos = q0 + lax.broadcasted_iota(jnp.int32, (1, tq), 1)
    lane = lax.broadcasted_iota(jnp.int32, (1, GROUP * tq), 1)
    n_cmp = ckv_ref.shape[1]
    n_sel = cover_ref.shape[0]
    per_tile = tk // SEL_BLOCK
    wide = GROUP * tq

    def tile4(x):
        return jnp.concatenate([x] * GROUP, axis=1)

    ik = lax.broadcasted_iota(jnp.int32, (tk, 1), 0)
    rel = ik - lax.broadcasted_iota(jnp.int32, (1, tq), 1)
    row64 = lax.broadcasted_iota(jnp.int32, (HEAD_DIM, 1), 0)
    lane128 = lax.broadcasted_iota(jnp.int32, (1, 2 * HEAD_DIM), 1)
    kfeat = jnp.where((lane128 == HEAD_DIM) | (lane128 == HEAD_DIM + 1), ik.astype(F32), 0.0)
    q4, slope4, slope2, qaug = [], [], [], []
    for h in range(KV_HEADS):
        qf = jnp.concatenate([qt[(h * GROUP + g) * HEAD_DIM:(h * GROUP + g + 1) * HEAD_DIM, :]
                              for g in range(GROUP)], axis=1)
        q4.append(qf.astype(BF16))
        sl = jnp.zeros((1, wide), F32)
        for g in range(GROUP):
            sl = jnp.where((lane >= g * tq) & (lane < (g + 1) * tq), SLOPES[h * GROUP + g], sl)
        slope4.append(sl)
        sl2 = sl * LOG2E
        slope2.append(sl2)
        sl_hi = sl2.astype(BF16).astype(F32)
        extra = jnp.where(row64 == 0, sl_hi, jnp.where(row64 == 1, sl2 - sl_hi, 0.0))
        qaug.append(jnp.concatenate([qf * LOG2E, extra], axis=0).astype(BF16))

    o_cmp = []
    slot = lax.broadcasted_iota(jnp.int32, (n_cmp, 1), 0)
    dist = qpos - ((slot + 1) * CMP_STRIDE - 1)
    cmask = tile4(((dist >= 0) & (slot >= 1)).astype(F32))
    cdist4 = tile4(dist.astype(F32))
    cov = cover_ref[...].astype(BF16)
    jj = lax.broadcasted_iota(jnp.int32, (n_sel, 1), 0)
    cur = qpos // SEL_BLOCK
    forced = (jj == 0) | (jj == cur) | (jj == cur - 1)
    valid = jj * SEL_BLOCK <= qpos
    for h in range(KV_HEADS):
        ckv = ckv_ref[0, :, h * 2 * HEAD_DIM:(h + 1) * 2 * HEAD_DIM]
        ck = ckv[:, :HEAD_DIM].astype(BF16)
        cvt = ckv.T[HEAD_DIM:, :].astype(BF16)
        s = _dot(ck, q4[h]) - slope4[h] * cdist4
        s = jnp.where(cmask > 0, s, NEG)
        p = jnp.exp(s - jnp.max(s, axis=0, keepdims=True)) * cmask
        p = p / jnp.maximum(jnp.sum(p, axis=0, keepdims=True), 1e-30)
        o_cmp.append(_dot(cvt, p.astype(BF16)))
        psum = p[:, 0:tq]
        for g in range(1, GROUP):
            psum = psum + p[:, g * tq:(g + 1) * tq]
        p_hi, p_lo = _split_hi_lo(psum)
        imp = _dot(cov, p_hi) + _dot(cov, p_lo)
        imp = jnp.where(forced, BIG, jnp.where(valid, imp, NEG))
        cnt = jnp.zeros((n_sel, tq), F32)
        for j2 in range(n_sel):
            r = imp[j2:j2 + 1, :]
            cnt = cnt + jnp.where(r > imp, 1.0, jnp.where((r == imp) & (j2 < jj), 1.0, 0.0))
        sel = jnp.where(cnt < SEL_TOP, BIG, NEG)
        for kk in range(n_sel // per_tile):
            sel_ref[h, kk] = sel[kk * per_tile:(kk + 1) * per_tile, :]

    half = ik // SEL_BLOCK
    ones_rows = jnp.ones((SUBLANES, tk), F32)

    def slc_cap(h, kt, d):
        rows = sel_ref[h, kt]
        cap = rows[0:1, :]
        for r in range(1, per_tile):
            cap = jnp.where(half == r, rows[r:r + 1, :], cap)
        return jnp.where(d >= 0, cap, NEG)

    def win_cap(h, kt, d):
        return jnp.where(d >= 0, jnp.where(d < WINDOW, BIG, NEG), NEG)

    hi = (q0 + tq + tk - 1) // tk
    lo_win = jnp.maximum(q0 - WINDOW, 0) // tk
    branches = ((kslc_ref, slc_cap, 0), (kwin_ref, win_cap, lo_win))
    n_chain = len(branches) * KV_HEADS
    for c in range(n_chain):
        acc_ref[c] = jnp.zeros((HEAD_DIM + SUBLANES, wide), F32)

    def make_body(active):
        def body(i, ms):
            ms = list(ms)
            tiles = []
            for sub in range(KT_UNROLL):
                kt_raw = KT_UNROLL * i + sub
                kt = jnp.minimum(kt_raw, hi - 1)
                k0 = pl.multiple_of(kt * tk, tk)
                tiles.append((kt_raw, kt, k0))
                for br in active:
                    kv_ref = branches[br][0]
                    for h in range(KV_HEADS):
                        kv = kv_ref[0, pl.ds(k0, tk), h * 2 * HEAD_DIM:(h + 1) * 2 * HEAD_DIM]
                        k_aug = jnp.where(lane128 < HEAD_DIM, kv, kfeat).astype(BF16)
                        s_ref[sub, br * KV_HEADS + h] = _dot(k_aug, qaug[h])
            for sub, (kt_raw, kt, k0) in enumerate(tiles):
                k0f = k0.astype(F32)
                for br in active:
                    kv_ref, cap_fn, lo_b = branches[br]
                    live = (kt_raw < hi) & (kt_raw >= lo_b)
                    d = jnp.where(live, (q0 - k0) - rel, -1)
                    alphas = []
                    for h in range(KV_HEADS):
                        c = br * KV_HEADS + h
                        sc = jnp.minimum(s_ref[sub, c], tile4(cap_fn(h, kt, d)))
                        off = slope2[h] * k0f
                        m_new = jnp.maximum(ms[c], jnp.max(sc, axis=0, keepdims=True) + off)
                        p_ref[sub, c] = jnp.exp2(sc - (m_new - off)).astype(BF16)
                        alphas.append(jnp.exp2(ms[c] - m_new))
                        ms[c] = m_new
                    for h in range(KV_HEADS):
                        c = br * KV_HEADS + h
                        kv = kv_ref[0, pl.ds(k0, tk), h * 2 * HEAD_DIM:(h + 1) * 2 * HEAD_DIM]
                        vt = jnp.concatenate([kv.T[HEAD_DIM:, :], ones_rows], axis=0).astype(BF16)
                        acc_ref[c] = alphas[h] * acc_ref[c] + _dot(vt, p_ref[sub, c])
            return tuple(ms)
        return body

    ms0 = tuple(jnp.full((1, wide), NEG, F32) for _ in range(n_chain))
    t_win = lo_win // KT_UNROLL
    t_end = (hi + KT_UNROLL - 1) // KT_UNROLL
    ms1 = lax.fori_loop(0, t_win, make_body((0,)), ms0)
    lax.fori_loop(t_win, t_end, make_body((0, 1)), ms1)
    o_att = [acc_ref[c, :HEAD_DIM, :] / jnp.maximum(acc_ref[c, HEAD_DIM:HEAD_DIM + 1, :], 1e-30)
             for c in range(n_chain)]
    o_slc, o_win = o_att[:KV_HEADS], o_att[KV_HEADS:]

    out_rows = []
    for h in range(KV_HEADS):
        for g in range(GROUP):
            hh = h * GROUP + g
            sl = slice(g * tq, (g + 1) * tq)
            out_rows.append(gt[hh:hh + 1, :] * o_cmp[h][:, sl]
                            + gt[NSA_HEADS + hh:NSA_HEADS + hh + 1, :] * o_slc[h][:, sl]
                            + gt[2 * NSA_HEADS + hh:2 * NSA_HEADS + hh + 1, :] * o_win[h][:, sl])
    o_ref[0] = jnp.concatenate(out_rows, axis=0).T


def _nsa_prompt_attn(q, ckv, kv_slc, kv_win, gates, tq=128, tk=128):
    b, t, _ = q.shape
    n_cmp = ckv.shape[1]
    n_sel = t // SEL_BLOCK
    cover_t = jnp.asarray(_cover_matrix(n_cmp, n_sel).T)
    kern = functools.partial(_nsa_prompt_kernel, tq=tq, tk=tk)
    return pl.pallas_call(
        kern,
        grid=(b, t // tq),
        in_specs=[pl.BlockSpec((1, tq, NSA_WIDTH), lambda i, j: (i, j, 0)),
                  pl.BlockSpec((1, n_cmp, KV_WIDTH), lambda i, j: (i, 0, 0)),
                  pl.BlockSpec((1, t, KV_WIDTH), lambda i, j: (i, 0, 0)),
                  pl.BlockSpec((1, t, KV_WIDTH), lambda i, j: (i, 0, 0)),
                  pl.BlockSpec((1, tq, LANES), lambda i, j: (i, j, 0)),
                  pl.BlockSpec((n_sel, n_cmp), lambda i, j: (0, 0))],
        out_specs=pl.BlockSpec((1, tq, NSA_WIDTH), lambda i, j: (i, j, 0)),
        out_shape=jax.ShapeDtypeStruct((b, t, NSA_WIDTH), F32),
        scratch_shapes=[pltpu.VMEM((KV_HEADS, n_sel * SEL_BLOCK // tk, tk // SEL_BLOCK, tq), F32),
                        pltpu.VMEM((2 * KV_HEADS, HEAD_DIM + SUBLANES, GROUP * tq), F32),
                        pltpu.VMEM((KT_UNROLL, 2 * KV_HEADS, tk, GROUP * tq), F32),
                        pltpu.VMEM((KT_UNROLL, 2 * KV_HEADS, tk, GROUP * tq), BF16)],
        compiler_params=_cparams("parallel", "parallel"),
        name="nsa_prompt_attn",
    )(q, ckv, kv_slc, kv_win, gates, cover_t)


TS = SUBLANES
SEL_LANES = 384


def _nsa_sample_select_kernel(q_ref, ckv_ref, cover_ref, ocmp_ref, idx_ref, *, nbatch, past_len, n_sel):
    n_cmp = ckv_ref.shape[1]
    rowi = lax.broadcasted_iota(jnp.int32, (GROUP * TS, 1), 0)
    tpos = past_len + rowi % TS
    slot = lax.broadcasted_iota(jnp.int32, (1, n_cmp), 1)
    dist = tpos - ((slot + 1) * CMP_STRIDE - 1)
    mask = ((dist >= 0) & (slot >= 1)).astype(F32)
    jj = lax.broadcasted_iota(jnp.int32, (1, SEL_LANES), 1)
    jjf = jj.astype(F32)
    qp8 = past_len + lax.broadcasted_iota(jnp.int32, (TS, 1), 0)
    cur = qp8 // SEL_BLOCK
    forced = (jj == 0) | (jj == cur) | (jj == cur - 1)
    valid = jj * SEL_BLOCK <= qp8
    lane = lax.broadcasted_iota(jnp.int32, (1, LANES), 1)
    cov = cover_ref[...].astype(BF16)
    chains = [(bb, h) for bb in range(nbatch) for h in range(KV_HEADS)]
    distf = dist.astype(F32)
    scores, probs, outs, imps = {}, {}, {}, {}
    for bb, h in chains:
        q = q_ref[bb] * SCALE
        qh = jnp.concatenate([q[:, (h * GROUP + g) * HEAD_DIM:(h * GROUP + g + 1) * HEAD_DIM]
                              for g in range(GROUP)], axis=0).astype(BF16)
        slope = jnp.zeros((GROUP * TS, 1), F32)
        for g in range(GROUP):
            slope = jnp.where(rowi // TS == g, SLOPES[h * GROUP + g], slope)
        ck = ckv_ref[bb, :, h * 2 * HEAD_DIM:h * 2 * HEAD_DIM + HEAD_DIM].astype(BF16)
        scores[bb, h] = jnp.where(mask > 0, _dot_nt(qh, ck) - slope * distf, NEG)
    for c in chains:
        s = scores[c]
        p = jnp.exp(s - jnp.max(s, axis=-1, keepdims=True)) * mask
        probs[c] = p / jnp.maximum(jnp.sum(p, axis=-1, keepdims=True), 1e-30)
    for bb, h in chains:
        p = probs[bb, h]
        cv = ckv_ref[bb, :, h * 2 * HEAD_DIM + HEAD_DIM:(h + 1) * 2 * HEAD_DIM].astype(BF16)
        outs[bb, h] = _dot(p.astype(BF16), cv)
        psum = p[0:TS]
        for g in range(1, GROUP):
            psum = psum + p[g * TS:(g + 1) * TS]
        p_hi, p_lo = _split_hi_lo(psum)
        imp = _dot(p_hi, cov) + _dot(p_lo, cov)
        imp = jnp.where(forced, BIG, jnp.where(valid, imp, NEG))
        imps[bb, h] = jnp.where(jj < n_sel, imp, -3e38)
    for bb in range(nbatch):
        ocmp_ref[bb] = jnp.concatenate([outs[bb, h][g * TS:(g + 1) * TS, :]
                                        for h in range(KV_HEADS) for g in range(GROUP)], axis=-1)
    idx_out = [jnp.zeros((TS, LANES), jnp.int32) for _ in range(nbatch)]
    for k in range(SEL_TOP):
        for bb in range(nbatch):
            for h in range(KV_HEADS):
                imp = imps[bb, h]
                best = jnp.max(imp, axis=-1, keepdims=True)
                pick = jnp.min(jnp.where(imp == best, jjf, float(SEL_LANES)), axis=-1, keepdims=True)
                idx_out[bb] = jnp.where(lane == h * SEL_TOP + k, pick.astype(jnp.int32), idx_out[bb])
                imps[bb, h] = jnp.where(jjf == pick, -3e38, imp)
    for bb in range(nbatch):
        idx_ref[bb] = idx_out[bb]


def _nsa_sample_select(q, ckv, past_len, nbatch=4):
    b = q.shape[0]
    n_cmp = ckv.shape[1]
    n_sel = -(-(past_len + 4) // SEL_BLOCK)
    cov = np.zeros((n_cmp, SEL_LANES), np.float32)
    cov[:, :n_sel] = _cover_matrix(n_cmp, n_sel)
    kern = functools.partial(_nsa_sample_select_kernel, nbatch=nbatch, past_len=past_len, n_sel=n_sel)
    return pl.pallas_call(
        kern,
        grid=(b // nbatch,),
        in_specs=[pl.BlockSpec((nbatch, TS, NSA_WIDTH), lambda i: (i, 0, 0)),
                  pl.BlockSpec((nbatch, n_cmp, KV_WIDTH), lambda i: (i, 0, 0)),
                  pl.BlockSpec((n_cmp, SEL_LANES), lambda i: (0, 0))],
        out_specs=[pl.BlockSpec((nbatch, TS, NSA_WIDTH), lambda i: (i, 0, 0)),
                   pl.BlockSpec((nbatch, TS, LANES), lambda i: (i, 0, 0))],
        out_shape=[jax.ShapeDtypeStruct((b, TS, NSA_WIDTH), F32),
                   jax.ShapeDtypeStruct((b, TS, LANES), jnp.int32)],
        compiler_params=_cparams("parallel"),
        name="nsa_sample_select",
    )(q, ckv, jnp.asarray(cov))


def _nsa_sample_attend_kernel(idx_ref, pt_ref, q_ref, ocmp_ref, gates_ref, knew_ref, wbuf_ref, wnew_ref,
                              pool_ref, o_ref, buf_ref, sem, *, t_dec, past_len):
    b = pl.program_id(0)
    last_blk = past_len // SEL_BLOCK
    per_page = PAGE // SEL_BLOCK

    nb = pl.num_programs(0)
    slot = b % 2

    def block_of(t, h, k, bb=None):
        bb = b if bb is None else bb
        return idx_ref[((bb * t_dec + t) * KV_HEADS + h) * SEL_TOP + k]

    def page_copies(bb, sl):
        out = []
        for t in range(t_dec):
            for h in range(KV_HEADS):
                for k in range(SEL_TOP):
                    page = pt_ref[bb, jnp.minimum(block_of(t, h, k, bb), last_blk - 1) // per_page]
                    out.append(pltpu.make_async_copy(pool_ref.at[page, h], buf_ref.at[sl, t, h, k], sem.at[sl]))
        return out

    @pl.when(b == 0)
    def _():
        for cp in page_copies(b, slot):
            cp.start()

    @pl.when(b + 1 < nb)
    def _():
        for cp in page_copies(b + 1, 1 - slot):
            cp.start()

    for cp in page_copies(b, slot):
        cp.wait()

    def rows_to_t(x):
        return jnp.concatenate([x, jnp.zeros((PAGE - TS, 2 * HEAD_DIM), F32)], axis=0).T

    rowg = lax.broadcasted_iota(jnp.int32, (SUBLANES, 1), 0)
    slope_col = [jnp.zeros((SUBLANES, 1), F32) for _ in range(KV_HEADS)]
    for h in range(KV_HEADS):
        for g in range(GROUP):
            slope_col[h] = jnp.where(rowg == g, SLOPES[h * GROUP + g], slope_col[h])
    n_tok = SEL_TOP * PAGE
    lane = lax.broadcasted_iota(jnp.int32, (1, n_tok), 1)
    row = lane % PAGE
    row_half = row // SEL_BLOCK
    slot_of = lane // PAGE
    n_win = wbuf_ref.shape[-1]
    wlane = lax.broadcasted_iota(jnp.int32, (1, n_win + PAGE), 1)
    wpos = jnp.where(wlane < n_win, past_len - n_win + wlane, past_len + wlane - n_win)

    for h in range(KV_HEADS):
        hs = slice(h * 2 * HEAD_DIM, (h + 1) * 2 * HEAD_DIM)
        new_t = rows_to_t(knew_ref[0, :, hs])
        wnew_t = rows_to_t(wnew_ref[0, :, hs])
        kw_t = jnp.concatenate([wbuf_ref[0, h, 0], wnew_t[:HEAD_DIM]], axis=1).astype(BF16)
        vw_t = jnp.concatenate([wbuf_ref[0, h, 1], wnew_t[HEAD_DIM:]], axis=1).astype(BF16)
        for t in range(t_dec):
            qpos = past_len + t
            qh = (q_ref[0, t, h] * SCALE).astype(BF16)
            k_tiles, v_tiles = [], []
            tok = jnp.zeros((1, n_tok), jnp.int32)
            want_half = jnp.zeros((1, n_tok), jnp.int32)
            for k in range(SEL_TOP):
                blk = block_of(t, h, k)
                is_new = blk >= last_blk
                k_tiles.append(jnp.where(is_new, new_t[:HEAD_DIM], buf_ref[slot, t, h, k, 0]))
                v_tiles.append(jnp.where(is_new, new_t[HEAD_DIM:], buf_ref[slot, t, h, k, 1]))
                tok = jnp.where(slot_of == k, (blk // per_page) * PAGE + row, tok)
                want_half = jnp.where(slot_of == k, blk % per_page, want_half)
            k_t = jnp.concatenate(k_tiles, axis=1).astype(BF16)
            v_t = jnp.concatenate(v_tiles, axis=1).astype(BF16)
            d = qpos - tok
            msk = jnp.where(d >= 0, jnp.where(row_half == want_half, 1.0, 0.0), 0.0)
            s = _dot(qh, k_t) - slope_col[h] * d.astype(F32)
            s = jnp.where(msk > 0, s, NEG)
            p = jnp.exp(s - jnp.max(s, axis=-1, keepdims=True)) * msk
            p = p / jnp.maximum(jnp.sum(p, axis=-1, keepdims=True), 1e-30)
            o_slc = _dot_nt(p.astype(BF16), v_t)
            dw = qpos - wpos
            mw = jnp.where(dw >= 0, jnp.where(dw < WINDOW, 1.0, 0.0), 0.0)
            sw = _dot(qh, kw_t) - slope_col[h] * dw.astype(F32)
            sw = jnp.where(mw > 0, sw, NEG)
            pw = jnp.exp(sw - jnp.max(sw, axis=-1, keepdims=True)) * mw
            pw = pw / jnp.maximum(jnp.sum(pw, axis=-1, keepdims=True), 1e-30)
            o_win = _dot_nt(pw.astype(BF16), vw_t)
            gts = jax.nn.sigmoid(gates_ref[0, t, h])
            o_ref[0, t, h] = (gts[:, 0:1] * ocmp_ref[0, t, h] + gts[:, 1:2] * o_slc + gts[:, 2:3] * o_win)


def _nsa_sample_attend(idx, page_table, q5, ocmp5, gates5, kv_slc_new, win_buf_t, kv_win_new, pool_slc_t,
                       t_dec, past_len):
    b = q5.shape[0]
    n_win = win_buf_t.shape[-1]
    blk5 = (1, t_dec, KV_HEADS, SUBLANES, HEAD_DIM)
    grid_spec = pltpu.PrefetchScalarGridSpec(
        num_scalar_prefetch=2,
        grid=(b,),
        in_specs=[pl.BlockSpec(blk5, lambda i, *_: (i, 0, 0, 0, 0)),
                  pl.BlockSpec(blk5, lambda i, *_: (i, 0, 0, 0, 0)),
                  pl.BlockSpec((1, t_dec, KV_HEADS, SUBLANES, LANES), lambda i, *_: (i, 0, 0, 0, 0)),
                  pl.BlockSpec((1, TS, KV_WIDTH), lambda i, *_: (i, 0, 0)),
                  pl.BlockSpec((1, KV_HEADS, 2, HEAD_DIM, n_win), lambda i, *_: (i, 0, 0, 0, 0)),
                  pl.BlockSpec((1, TS, KV_WIDTH), lambda i, *_: (i, 0, 0)),
                  pl.BlockSpec(memory_space=pl.ANY)],
        out_specs=pl.BlockSpec(blk5, lambda i, *_: (i, 0, 0, 0, 0)),
        scratch_shapes=[pltpu.VMEM((2, t_dec, KV_HEADS, SEL_TOP, 2, HEAD_DIM, PAGE), F32),
                        pltpu.SemaphoreType.DMA((2,))],
    )
    kern = functools.partial(_nsa_sample_attend_kernel, t_dec=t_dec, past_len=past_len)
    return pl.pallas_call(
        kern,
        grid_spec=grid_spec,
        out_shape=jax.ShapeDtypeStruct((b, t_dec, KV_HEADS, SUBLANES, HEAD_DIM), F32),
        compiler_params=_cparams("arbitrary"),
        name="nsa_sample_attend",
    )(idx[:, :t_dec, :KV_HEADS * SEL_TOP].reshape(-1), page_table, q5, ocmp5, gates5, kv_slc_new, win_buf_t,
      kv_win_new, pool_slc_t)


NSA_SPLITS = ((0, NSA_WIDTH), (NSA_WIDTH, KV_WIDTH), (NSA_WIDTH + KV_WIDTH, KV_WIDTH),
              (NSA_WIDTH + 2 * KV_WIDTH, KV_WIDTH), (NSA_WIDTH + 3 * KV_WIDTH, MEM_WIDTH),
              (NSA_WIDTH + 3 * KV_WIDTH + MEM_WIDTH, LANES))


def _nsa_in_weight(w_in):
    o = NSA_WIDTH + 3 * KV_WIDTH
    n_gate = 3 * NSA_HEADS
    gates = jnp.pad(w_in[:, o:o + n_gate], ((0, 0), (0, LANES - n_gate)))
    return jnp.concatenate([w_in[:, :o], w_in[:, o + n_gate:], gates], axis=1).astype(BF16)


def _to_heads5(x, t_dec):
    b = x.shape[0]
    w = x.shape[-1] // NSA_HEADS
    x = x[:, :t_dec].reshape(b, t_dec, KV_HEADS, GROUP, w)
    return jnp.pad(x, ((0, 0), (0, 0), (0, 0), (0, SUBLANES - GROUP), (0, 0)))


def kernel(x_prompt, x_sample, mem_prompt, state_ssm_re, state_ssm_im, cache_cmp_kv, cache_slc_kv, cache_win_kv, cache_mem_kv, page_table, norm_mix_pre, norm_mix_post, norm_ffn_pre, norm_ffn_post, w_out, w_mem_kv, w_ffn_in, w_ffn_out, ssm_w_in, ssm_a_re, ssm_a_im, ssm_log_dt, ssm_b_re, ssm_b_im, ssm_c_re, ssm_c_im, ssm_d, ssm_w_glu, ssm_b_glu, nsa_w_in, nsa_cmp_pe, nsa_cmp_w1, nsa_cmp_b1, nsa_cmp_w2):
    bp, seq, d = x_prompt.shape
    bs, t_dec, _ = x_sample.shape
    n_mem = mem_prompt.shape[1]
    depth = w_out.shape[0]
    past_len = page_table.shape[1] * PAGE
    d_ff = w_ffn_out.shape[1]
    chunk = 16

    xp = x_prompt.reshape(bp * seq, d)
    xs = jnp.pad(x_sample, ((0, 0), (0, TS - t_dec), (0, 0))).reshape(bs * TS, d)
    mem2d = mem_prompt.reshape(bp * n_mem, d)

    outs = {k: [] for k in ('ssm_re_p', 'ssm_im_p', 'ssm_re_s', 'ssm_im_s', 'cmp_p', 'slc_p', 'win_p',
                            'cmp_s', 'slc_s', 'win_s', 'mem_p')}
    for i in range(depth):
        j = i // 2
        mem_split = ((0, 2 * MEM_WIDTH),)
        mkv_p, mkv_p_t = _norm_proj(mem2d, norm_mix_pre[i], w_mem_kv[i].astype(BF16), mem_split, do_norm=False,
                                    t_splits=mem_split, seq=n_mem)
        mkv_p = mkv_p.reshape(bp, n_mem, 2 * MEM_WIDTH)
        outs['mem_p'].append(jnp.moveaxis(mkv_p_t.reshape(bp, MEM_HEADS, 2, HEAD_DIM, n_mem), -1, 1))
        mkv_s_t = _kv_cache_t(cache_mem_kv[i])
        if i % 2 == 0:
            w_in = ssm_w_in[j].astype(BF16)
            wglu = ssm_w_glu[j].astype(BF16)
            ssm_w = (ssm_a_re[j], ssm_a_im[j], ssm_log_dt[j], ssm_b_re[j], ssm_b_im[j], ssm_c_re[j], ssm_c_im[j])
            ssm_split = ((0, SSM_WIDTH), (SSM_WIDTH, MEM_WIDTH))
            u, qm = _norm_proj(xp, norm_mix_pre[i], w_in, ssm_split)
            zero = jnp.zeros((bp, SSM_GROUPS, SSM_STATE), F32)
            y, hlr, hli = _s5_scan_octets(u.reshape(bp, seq, SSM_WIDTH), zero, zero,
                                          _s5_operators(*ssm_w, chunk), chunk)
            m1p = _s5_glu(y.reshape(bp * seq, SSM_WIDTH), u, ssm_d[j], wglu, ssm_b_glu[j])
            m2p = _mem_attn(qm.reshape(bp, seq, MEM_WIDTH), mkv_p).reshape(bp * seq, MEM_WIDTH)
            outs['ssm_re_p'].append(hlr)
            outs['ssm_im_p'].append(hli)
            u, qm = _norm_proj(xs, norm_mix_pre[i], w_in, ssm_split)
            y, hlr, hli = _s5_scan(u.reshape(bs, TS, SSM_WIDTH)[:, :t_dec], state_ssm_re[j].astype(F32),
                                   state_ssm_im[j].astype(F32), _s5_operators(*ssm_w, t_dec), t_dec)
            y = jnp.pad(y, ((0, 0), (0, TS - t_dec), (0, 0)))
            m1s = _s5_glu(y.reshape(bs * TS, SSM_WIDTH), u, ssm_d[j], wglu, ssm_b_glu[j])
            m2s = _mem_attn_t(qm.reshape(bs, TS, MEM_WIDTH), mkv_s_t).reshape(bs * TS, MEM_WIDTH)
            outs['ssm_re_s'].append(hlr)
            outs['ssm_im_s'].append(hli)
        else:
            w_in = _nsa_in_weight(nsa_w_in[j])
            wp, cbias, w2b = _cmp_weights(nsa_cmp_pe[j], nsa_cmp_w1[j], nsa_cmp_b1[j], nsa_cmp_w2[j])
            kv_shape = (KV_HEADS, 2, HEAD_DIM)
            q, kc, ks, kw, qm, gates, kc_t, ks_t, kw_t = _norm_proj(
                xp, norm_mix_pre[i], w_in, NSA_SPLITS, t_splits=NSA_SPLITS[1:4], seq=seq)
            kc3, ks3, kw3 = (a.reshape(bp, seq, KV_WIDTH) for a in (kc, ks, kw))

            def rows_major(a_t):
                return jnp.moveaxis(a_t.reshape((bp,) + kv_shape + (a_t.shape[-1],)), -1, 1)
            ident = jnp.arange(bp * (seq // PAGE), dtype=jnp.int32).reshape(bp, seq // PAGE)
            ckv = _cmp_tokens(kc.reshape(bp * seq // PAGE, PAGE, KV_WIDTH), ident, wp, cbias, w2b, False)
            m1p = _nsa_prompt_attn(q.reshape(bp, seq, NSA_WIDTH), ckv, ks3, kw3,
                                   gates.reshape(bp, seq, LANES)).reshape(bp * seq, NSA_WIDTH)
            m2p = _mem_attn(qm.reshape(bp, seq, MEM_WIDTH), mkv_p).reshape(bp * seq, MEM_WIDTH)
            outs['cmp_p'].append(rows_major(kc_t))
            outs['slc_p'].append(rows_major(ks_t))
            outs['win_p'].append(rows_major(kw_t[:, :, seq - min(WINDOW, seq):]))
            q, kc, ks, kw, qm, gates = _norm_proj(xs, norm_mix_pre[i], w_in, NSA_SPLITS)
            kc3, ks3, kw3 = (a.reshape(bs, TS, KV_WIDTH) for a in (kc, ks, kw))
            pool_cmp_t = _kv_cache_t(cache_cmp_kv[j])
            pool_cmp_t = pool_cmp_t.reshape(pool_cmp_t.shape[0], KV_HEADS, 2 * HEAD_DIM, PAGE)
            pool_slc_t = _kv_cache_t(cache_slc_kv[j])
            win_buf_t = _kv_cache_t(cache_win_kv[j])
            ckv = _cmp_tokens(pool_cmp_t, page_table, wp, cbias, w2b, True)
            q3 = q.reshape(bs, TS, NSA_WIDTH)
            ocmp, idx = _nsa_sample_select(q3, ckv, past_len)
            gates3 = gates.reshape(bs, TS, LANES)[:, :, :3 * NSA_HEADS].reshape(bs, TS, 3, NSA_HEADS)
            gates5 = _to_heads5(gates3.transpose(0, 1, 3, 2).reshape(bs, TS, NSA_HEADS * 3), t_dec)
            gates5 = jnp.pad(gates5, ((0, 0),) * 4 + ((0, LANES - 3),))
            o5 = _nsa_sample_attend(idx, page_table, _to_heads5(q3, t_dec), _to_heads5(ocmp, t_dec), gates5,
                                    ks3, win_buf_t, kw3, pool_slc_t, t_dec, past_len)
            o = o5[:, :, :, :GROUP].reshape(bs, t_dec, NSA_WIDTH)
            m1s = jnp.pad(o, ((0, 0), (0, TS - t_dec), (0, 0))).reshape(bs * TS, NSA_WIDTH)
            m2s = _mem_attn_t(qm.reshape(bs, TS, MEM_WIDTH), mkv_s_t).reshape(bs * TS, MEM_WIDTH)
            outs['cmp_s'].append(kc3[:, :t_dec].reshape((bs, t_dec) + kv_shape))
            outs['slc_s'].append(ks3[:, :t_dec].reshape((bs, t_dec) + kv_shape))
            kv_w = jnp.concatenate([cache_win_kv[j], kw3[:, :t_dec].reshape((bs, t_dec) + kv_shape)], axis=1)
            n_keep = min(WINDOW, past_len + t_dec)
            outs['win_s'].append(kv_w[:, kv_w.shape[1] - n_keep:])
        wo = w_out[i].astype(BF16)
        wg = w_ffn_in[i][:, :d_ff].astype(BF16)
        wu = w_ffn_in[i][:, d_ff:].astype(BF16)
        wd = w_ffn_out[i].astype(BF16)
        w1 = m1p.shape[1]
        args = (wo[:w1], wo[w1:], norm_mix_post[i], norm_ffn_pre[i], wg, wu, wd, norm_ffn_post[i])
        xp = _post(xp, m1p, m2p, *args)
        xs = _post(xs, m1s, m2s, *args)

    st = lambda k: jnp.stack(outs[k])
    y_sample = xs.reshape(bs, TS, d)[:, :t_dec]
    return (xp.reshape(bp, seq, d), y_sample, st('ssm_re_p'), st('ssm_im_p'), st('ssm_re_s'), st('ssm_im_s'),
            st('cmp_p'), st('slc_p'), st('win_p'), st('cmp_s'), st('slc_s'), st('win_s'), st('mem_p'))
```

```python
import functools
import math

import numpy as np
import jax
import jax.numpy as jnp
from jax import lax
from jax.experimental import pallas as pl
from jax.experimental.pallas import tpu as pltpu

F32 = jnp.float32
BF16 = jnp.bfloat16

D_MODEL = 1024
PAGE = 128
MEM_HEADS = 4
HEAD_DIM = 64
MEM_WIDTH = MEM_HEADS * HEAD_DIM
SSM_WIDTH = D_MODEL - MEM_WIDTH
SSM_GROUP = 16
SSM_GROUPS = SSM_WIDTH // SSM_GROUP
SSM_STATE = 64
NSA_HEADS = 12
KV_HEADS = 3
GROUP = NSA_HEADS // KV_HEADS
NSA_WIDTH = NSA_HEADS * HEAD_DIM
KV_WIDTH = KV_HEADS * 2 * HEAD_DIM
CMP_BLOCK = 32
CMP_STRIDE = 16
CMP_HIDDEN = 2 * HEAD_DIM
SEL_BLOCK = 64
SEL_TOP = 16
WINDOW = 512
RMS_EPS = 1e-6
NEG = -1e30
BIG = 1e30
SCALE = HEAD_DIM ** -0.5
LANES = 128
SUBLANES = 8
VMEM_LIMIT = 56 * 1024 * 1024
SLOPES = [2.0 ** (-8.0 * (h + 1) / NSA_HEADS) for h in range(NSA_HEADS)]
LOG2E = math.log2(math.e)


def _cparams(*sem):
    return pltpu.CompilerParams(dimension_semantics=sem, vmem_limit_bytes=VMEM_LIMIT)


def _rms(x, g):
    return x * lax.rsqrt(jnp.mean(x * x, axis=-1, keepdims=True) + RMS_EPS) * g


def _gelu(x):
    return 0.5 * x * (1.0 + jnp.tanh(math.sqrt(2.0 / math.pi) * (x + 0.044715 * (x * x * x))))


def _dot(a, b):
    return jnp.dot(a, b, preferred_element_type=F32)


def _dot_nt(a, b):
    return lax.dot_general(a, b, (((1,), (1,)), ((), ())), preferred_element_type=F32)


def _kv_cache_t(cache):
    n = cache.ndim
    return jnp.moveaxis(cache, n - 4, n - 1)


def _norm_proj_kernel(x_ref, g_ref, w_ref, *o_refs, splits, t_splits, h_splits, do_norm):
    x = x_ref[...]
    if do_norm:
        x = _rms(x, g_ref[...])
    z = _dot(x.astype(BF16), w_ref[...])
    for (start, width), o in zip(splits + h_splits, o_refs):
        o[...] = z[:, start:start + width].astype(o.dtype)
    for (start, width), o in zip(t_splits, o_refs[len(splits) + len(h_splits):]):
        o[0] = z[:, start:start + width].T


def _norm_proj(x, g, w, splits, do_norm=True, tm=512, t_splits=(), seq=None, h_splits=()):
    rows, d = x.shape
    n = w.shape[1]
    tm = min(tm, rows, seq) if t_splits else min(tm, rows)
    splits, t_splits, h_splits = tuple(splits), tuple(t_splits), tuple(h_splits)
    kern = functools.partial(_norm_proj_kernel, splits=splits, t_splits=t_splits, h_splits=h_splits,
                             do_norm=do_norm)
    out_specs = [pl.BlockSpec((tm, wd), lambda i: (i, 0)) for _, wd in splits + h_splits]
    out_shape = ([jax.ShapeDtypeStruct((rows, wd), F32) for _, wd in splits]
                 + [jax.ShapeDtypeStruct((rows, wd), BF16) for _, wd in h_splits])
    if t_splits:
        per_seq = seq // tm
        out_specs += [pl.BlockSpec((1, wd, tm), lambda i: (i // per_seq, 0, i % per_seq)) for _, wd in t_splits]
        out_shape += [jax.ShapeDtypeStruct((rows // seq, wd, seq), F32) for _, wd in t_splits]
    return pl.pallas_call(
        kern,
        grid=(rows // tm,),
        in_specs=[pl.BlockSpec((tm, d), lambda i: (i, 0)),
                  pl.BlockSpec((1, d), lambda i: (0, 0)),
                  pl.BlockSpec((d, n), lambda i: (0, 0))],
        out_specs=out_specs,
        out_shape=out_shape,
        compiler_params=_cparams("parallel"),
        name="norm_proj",
    )(x, g.reshape(1, d), w)


def _softmax_rows(s):
    p = jnp.exp(s - jnp.max(s, axis=-1, keepdims=True))
    return p, jnp.sum(p, axis=-1, keepdims=True)


def _mem_attn_kernel(q_ref, kv_ref, o_ref):
    q = q_ref[0]
    kv = kv_ref[0]
    outs = []
    for h in range(MEM_HEADS):
        qh = (q[:, h * HEAD_DIM:(h + 1) * HEAD_DIM] * SCALE).astype(BF16)
        k = kv[:, h * 2 * HEAD_DIM:h * 2 * HEAD_DIM + HEAD_DIM].astype(BF16)
        v = kv[:, h * 2 * HEAD_DIM + HEAD_DIM:(h + 1) * 2 * HEAD_DIM].astype(BF16)
        p, l = _softmax_rows(_dot_nt(qh, k))
        outs.append(_dot(p.astype(BF16), v) / l)
    o_ref[0] = jnp.concatenate(outs, axis=-1)


def _mem_attn(qm, mem_kv, tm=512):
    b, t, _ = qm.shape
    tm = min(tm, t)
    n_mem = mem_kv.shape[1]
    return pl.pallas_call(
        _mem_attn_kernel,
        grid=(b, t // tm),
        in_specs=[pl.BlockSpec((1, tm, MEM_WIDTH), lambda i, j: (i, j, 0)),
                  pl.BlockSpec((1, n_mem, 2 * MEM_WIDTH), lambda i, j: (i, 0, 0))],
        out_specs=pl.BlockSpec((1, tm, MEM_WIDTH), lambda i, j: (i, j, 0)),
        out_shape=jax.ShapeDtypeStruct((b, t, MEM_WIDTH), F32),
        compiler_params=_cparams("parallel", "parallel"),
        name="mem_attn",
    )(qm, mem_kv)


def _mem_attn_t_kernel(q_ref, kv_ref, o_ref, *, nbatch):
    pairs = [(bb, h) for bb in range(nbatch) for h in range(MEM_HEADS)]
    scores, probs, outs = {}, {}, {}
    for bb, h in pairs:
        qh = (q_ref[bb][:, h * HEAD_DIM:(h + 1) * HEAD_DIM] * SCALE).astype(BF16)
        scores[bb, h] = _dot(qh, kv_ref[bb, h, 0].astype(BF16))
    for c in pairs:
        probs[c] = _softmax_rows(scores[c])
    for bb, h in pairs:
        p, l = probs[bb, h]
        outs[bb, h] = _dot_nt(p.astype(BF16), kv_ref[bb, h, 1].astype(BF16)) / l
    for bb in range(nbatch):
        o_ref[bb] = jnp.concatenate([outs[bb, h] for h in range(MEM_HEADS)], axis=-1)


def _mem_attn_t(qm, mem_kv_t, nbatch=4):
    b, t, _ = qm.shape
    n_mem = mem_kv_t.shape[-1]
    return pl.pallas_call(
        functools.partial(_mem_attn_t_kernel, nbatch=nbatch),
        grid=(b // nbatch,),
        in_specs=[pl.BlockSpec((nbatch, t, MEM_WIDTH), lambda i: (i, 0, 0)),
                  pl.BlockSpec((nbatch, MEM_HEADS, 2, HEAD_DIM, n_mem), lambda i: (i, 0, 0, 0, 0))],
        out_specs=pl.BlockSpec((nbatch, t, MEM_WIDTH), lambda i: (i, 0, 0)),
        out_shape=jax.ShapeDtypeStruct((b, t, MEM_WIDTH), F32),
        compiler_params=_cparams("parallel"),
        name="mem_attn_t",
    )(qm, mem_kv_t)


def _post_kernel(x_ref, m1_ref, m2_ref, wo1_ref, wo2_ref, g1_ref, g2_ref, wg_ref, wu_ref, wd_ref, g3_ref,
                 o_ref):
    a = _dot(m1_ref[...].astype(BF16), wo1_ref[...]) + _dot(m2_ref[...].astype(BF16), wo2_ref[...])
    x1 = x_ref[...] + _rms(a, g1_ref[...])
    h = _rms(x1, g2_ref[...]).astype(BF16)
    gate = _dot(h, wg_ref[...])
    up = _dot(h, wu_ref[...])
    act = (gate * jax.nn.sigmoid(gate) * up).astype(BF16)
    f = _dot(act, wd_ref[...])
    o_ref[...] = x1 + _rms(f, g3_ref[...])


def _post(x, m1, m2, wo1, wo2, g1, g2, wg, wu, wd, g3, tm=512):
    rows, d = x.shape
    tm = min(tm, rows)
    w1, w2 = m1.shape[1], m2.shape[1]
    dff = wg.shape[1]

    def const(shape):
        return pl.BlockSpec(shape, lambda i: (0, 0), pipeline_mode=pl.Buffered(1))

    return pl.pallas_call(
        _post_kernel,
        grid=(rows // tm,),
        in_specs=[pl.BlockSpec((tm, d), lambda i: (i, 0)),
                  pl.BlockSpec((tm, w1), lambda i: (i, 0)),
                  pl.BlockSpec((tm, w2), lambda i: (i, 0)),
                  const((w1, d)), const((w2, d)), const((1, d)), const((1, d)),
                  const((d, dff)), const((d, dff)), const((dff, d)), const((1, d))],
        out_specs=pl.BlockSpec((tm, d), lambda i: (i, 0)),
        out_shape=jax.ShapeDtypeStruct((rows, d), F32),
        compiler_params=_cparams("parallel"),
        name="post_ffn",
    )(x, m1, m2, wo1, wo2, g1.reshape(1, d), g2.reshape(1, d), wg, wu, wd, g3.reshape(1, d))


def _s5_operators(a_re, a_im, log_dt, b_re, b_im, c_re, c_im, q):
    hp = lax.Precision.HIGHEST
    a_re, a_im = a_re.astype(F32), a_im.astype(F32)
    dt = jnp.exp(log_dt.astype(F32))[:, None]
    mag = jnp.exp(a_re * dt)
    ab_r, ab_i = mag * jnp.cos(a_im * dt), mag * jnp.sin(a_im * dt)
    den = a_re * a_re + a_im * a_im
    nr, ni = ab_r - 1.0, ab_i
    f_r = (nr * a_re + ni * a_im) / den
    f_i = (ni * a_re - nr * a_im) / den
    bb_r = f_r[..., None] * b_re - f_i[..., None] * b_im
    bb_i = f_r[..., None] * b_im + f_i[..., None] * b_re
    pw_r, pw_i = [jnp.ones_like(ab_r)], [jnp.zeros_like(ab_i)]
    for _ in range(q):
        r, i = pw_r[-1], pw_i[-1]
        pw_r.append(r * ab_r - i * ab_i)
        pw_i.append(r * ab_i + i * ab_r)
    pw_r, pw_i = jnp.stack(pw_r), jnp.stack(pw_i)
    g = a_re.shape[0]
    pt_r, pt_i = pw_r.transpose(1, 2, 0)[..., None], pw_i.transpose(1, 2, 0)[..., None]
    w_r = pt_r * bb_r[:, :, None, :] - pt_i * bb_i[:, :, None, :]
    w_i = pt_r * bb_i[:, :, None, :] + pt_i * bb_r[:, :, None, :]
    c2 = jnp.concatenate([c_re, -c_im], axis=-1)
    w2 = jnp.concatenate([w_r, w_i], axis=1).reshape(g, 2 * SSM_STATE, (q + 1) * SSM_GROUP)
    kk = jnp.einsum('gip,gpn->gin', c2, w2, precision=hp).reshape(g, SSM_GROUP, q + 1, SSM_GROUP)
    tau = jnp.arange(q)[None, :] - jnp.arange(q)[:, None]
    kt = kk[:, :, jnp.clip(tau, 0, q), :]
    kt = jnp.where((tau >= 0)[None, None, :, :, None], kt, 0.0)
    toep = kt.transpose(0, 2, 4, 3, 1).reshape(g, q * SSM_GROUP, q * SSM_GROUP)
    rev = q - 1 - jnp.arange(q)
    bst_r = w_r[:, :, rev, :].transpose(0, 2, 3, 1).reshape(g, q * SSM_GROUP, SSM_STATE)
    bst_i = w_i[:, :, rev, :].transpose(0, 2, 3, 1).reshape(g, q * SSM_GROUP, SSM_STATE)
    ar, ai = pw_r[1:], pw_i[1:]
    co_r = c_re[None] * ar[:, :, None, :] - c_im[None] * ai[:, :, None, :]
    co_i = -c_re[None] * ai[:, :, None, :] - c_im[None] * ar[:, :, None, :]
    cout_r = co_r.transpose(1, 3, 0, 2).reshape(g, SSM_STATE, q * SSM_GROUP)
    cout_i = co_i.transpose(1, 3, 0, 2).reshape(g, SSM_STATE, q * SSM_GROUP)
    return (toep.astype(BF16), bst_r.astype(BF16), bst_i.astype(BF16), cout_r.astype(BF16),
            cout_i.astype(BF16), pw_r[q][:, None, :], pw_i[q][:, None, :], kk[:, :, :q, :].astype(BF16))


def _s5_scan_kernel(u_ref, toep_ref, bsr_ref, bsi_ref, cor_ref, coi_ref, aqr_ref, aqi_ref, h0r_ref, h0i_ref,
                    y_ref, hlr_ref, hli_ref, sr_ref, si_ref, hpr_ref, hpi_ref, *, gs, nc, nb):
    for g in range(gs):
        u = u_ref[g].astype(BF16)
        sr_ref[g] = _dot(u, bsr_ref[g])
        si_ref[g] = _dot(u, bsi_ref[g])

    def step(c, carry):
        rows = pl.ds(pl.multiple_of(c * nb, SUBLANES), nb)
        new = []
        for g in range(gs):
            hr, hi = carry[2 * g], carry[2 * g + 1]
            hpr_ref[g, rows, :] = hr
            hpi_ref[g, rows, :] = hi
            ar, ai = aqr_ref[g], aqi_ref[g]
            new.append(ar * hr - ai * hi + sr_ref[g, rows, :])
            new.append(ar * hi + ai * hr + si_ref[g, rows, :])
        return tuple(new)

    init = []
    for g in range(gs):
        init += [h0r_ref[g], h0i_ref[g]]
    fin = lax.fori_loop(0, nc, step, tuple(init))
    for g in range(gs):
        hlr_ref[g] = fin[2 * g]
        hli_ref[g] = fin[2 * g + 1]
        u = u_ref[g].astype(BF16)
        y_ref[g] = (_dot(u, toep_ref[g]) + _dot(hpr_ref[g].astype(BF16), cor_ref[g])
                    + _dot(hpi_ref[g].astype(BF16), coi_ref[g]))


def _s5_scan(u, h0r, h0i, ops, q, gs=4):
    b, l, _ = u.shape
    nc = l // q
    qw = q * SSM_GROUP
    toep, bsr, bsi, cor, coi, aqr, aqi = ops[:7]
    ug = u.reshape(b, nc, q, SSM_GROUPS, SSM_GROUP).transpose(3, 1, 0, 2, 4).reshape(SSM_GROUPS, nc * b, qw)
    h0r_g = h0r.transpose(1, 0, 2)
    h0i_g = h0i.transpose(1, 0, 2)
    rows = nc * b

    def gspec(r, c):
        return pl.BlockSpec((gs, r, c), lambda i: (i, 0, 0))

    kern = functools.partial(_s5_scan_kernel, gs=gs, nc=nc, nb=b)
    y, hlr, hli = pl.pallas_call(
        kern,
        grid=(SSM_GROUPS // gs,),
        in_specs=[gspec(rows, qw), gspec(qw, qw), gspec(qw, SSM_STATE), gspec(qw, SSM_STATE),
                  gspec(SSM_STATE, qw), gspec(SSM_STATE, qw), gspec(1, SSM_STATE), gspec(1, SSM_STATE),
                  gspec(b, SSM_STATE), gspec(b, SSM_STATE)],
        out_specs=[gspec(rows, qw), gspec(b, SSM_STATE), gspec(b, SSM_STATE)],
        out_shape=[jax.ShapeDtypeStruct((SSM_GROUPS, rows, qw), F32),
                   jax.ShapeDtypeStruct((SSM_GROUPS, b, SSM_STATE), F32),
                   jax.ShapeDtypeStruct((SSM_GROUPS, b, SSM_STATE), F32)],
        scratch_shapes=[pltpu.VMEM((gs, rows, SSM_STATE), F32) for _ in range(4)],
        compiler_params=_cparams("parallel"),
        name="s5_scan_groups",
    )(ug, toep, bsr, bsi, cor, coi, aqr, aqi, h0r_g, h0i_g)
    y = y.reshape(SSM_GROUPS, nc, b, q, SSM_GROUP).transpose(2, 1, 3, 0, 4).reshape(b, l, SSM_WIDTH)
    return y, hlr.transpose(1, 0, 2), hli.transpose(1, 0, 2)


OCT = LANES // SSM_GROUP
N_OCT = SSM_GROUPS // OCT
OCT_STATE = OCT * SSM_STATE


def _s5_octet_operators(ops, q):
    _, bsr, bsi, cor, coi, aqr, aqi, kk = ops
    qq = q * LANES
    dt = kk.dtype

    def spread(x, tile_np, keep_np):
        y = jnp.einsum('...k,kn->...n', x, jnp.asarray(tile_np, dt), preferred_element_type=F32)
        return (y * jnp.asarray(keep_np, F32)).astype(dt)

    grp_of_gj = np.arange(LANES) // SSM_GROUP
    grp_of_gp = np.arange(OCT_STATE) // SSM_STATE
    lag = kk.reshape(N_OCT, OCT, SSM_GROUP, q, SSM_GROUP)
    lag = lag.transpose(0, 3, 1, 4, 2).reshape(N_OCT, q, LANES, SSM_GROUP)
    k8 = spread(lag, np.tile(np.eye(SSM_GROUP), (1, OCT)), grp_of_gj[:, None] == grp_of_gj[None, :])

    def bst8(b):
        b = b.reshape(N_OCT, OCT, q, SSM_GROUP, SSM_STATE).transpose(0, 2, 1, 3, 4).reshape(N_OCT, qq, SSM_STATE)
        keep = np.tile(grp_of_gj, q)[:, None] == grp_of_gp[None, :]
        return spread(b, np.tile(np.eye(SSM_STATE), (1, OCT)), keep)

    def cout8(c):
        c = c.reshape(N_OCT, OCT_STATE, q * SSM_GROUP)
        t_of, i_of = np.arange(q * SSM_GROUP) // SSM_GROUP, np.arange(q * SSM_GROUP) % SSM_GROUP
        col_t, col_i = np.arange(qq) // LANES, np.arange(qq) % SSM_GROUP
        tile = (t_of[:, None] == col_t[None, :]) & (i_of[:, None] == col_i[None, :])
        keep = grp_of_gp[:, None] == np.tile(grp_of_gj, q)[None, :]
        return spread(c, tile.astype(np.float32), keep)

    return (k8, bst8(bsr), bst8(bsi), cout8(cor), cout8(coi),
            aqr.reshape(N_OCT, 1, OCT_STATE), aqi.reshape(N_OCT, 1, OCT_STATE))


def _s5_octet_kernel(x_ref, lag_ref, bsr_ref, bsi_ref, cor_ref, coi_ref, aqr_ref, aqi_ref, h0r_ref, h0i_ref,
                     y_ref, hlr_ref, hli_ref, sr_ref, si_ref, hpr_ref, hpi_ref, hr_ref, hi_ref, toep_ref,
                     *, ncb, nb, q):
    @pl.when(pl.program_id(1) == 0)
    def _():
        hr_ref[...] = h0r_ref[0]
        hi_ref[...] = h0i_ref[0]
        zero = jnp.zeros((LANES, LANES), BF16)
        for s in range(q):
            for t in range(q):
                toep_ref[s * LANES:(s + 1) * LANES, t * LANES:(t + 1) * LANES] = (
                    lag_ref[0, t - s] if t >= s else zero)

    x = x_ref[0]
    sr_ref[...] = _dot(x, bsr_ref[0])
    si_ref[...] = _dot(x, bsi_ref[0])
    ar, ai = aqr_ref[0], aqi_ref[0]

    def step(c, carry):
        hr, hi = carry
        rows = pl.ds(pl.multiple_of(c * nb, SUBLANES), nb)
        hpr_ref[rows, :] = hr
        hpi_ref[rows, :] = hi
        return ar * hr - ai * hi + sr_ref[rows, :], ar * hi + ai * hr + si_ref[rows, :]

    hr, hi = lax.fori_loop(0, ncb, step, (hr_ref[...], hi_ref[...]))
    hr_ref[...] = hr
    hi_ref[...] = hi
    hlr_ref[0] = hr
    hli_ref[0] = hi
    y_ref[0] = (_dot(x, toep_ref[...]) + _dot(hpr_ref[...].astype(BF16), cor_ref[0])
                + _dot(hpi_ref[...].astype(BF16), coi_ref[0]))


def _s5_scan_octets(u, h0r, h0i, ops, q, row_blocks=4):
    b, l, _ = u.shape
    nc = l // q
    qq = q * LANES
    rows = nc * b
    rb = rows // row_blocks
    ncb = nc // row_blocks
    ops8 = _s5_octet_operators(ops, q)
    x = u.reshape(b, nc, q, N_OCT, LANES).transpose(3, 1, 0, 2, 4).reshape(N_OCT, rows, qq).astype(BF16)
    h0r8 = h0r.reshape(b, N_OCT, OCT_STATE).transpose(1, 0, 2)
    h0i8 = h0i.reshape(b, N_OCT, OCT_STATE).transpose(1, 0, 2)

    def wspec(r, c):
        return pl.BlockSpec((1, r, c), lambda o, i: (o, 0, 0))

    kern = functools.partial(_s5_octet_kernel, ncb=ncb, nb=b, q=q)
    y, hlr, hli = pl.pallas_call(
        kern,
        grid=(N_OCT, row_blocks),
        in_specs=[pl.BlockSpec((1, rb, qq), lambda o, i: (o, i, 0)),
                  pl.BlockSpec((1, q, LANES, LANES), lambda o, i: (o, 0, 0, 0)),
                  wspec(qq, OCT_STATE), wspec(qq, OCT_STATE), wspec(OCT_STATE, qq),
                  wspec(OCT_STATE, qq), wspec(1, OCT_STATE), wspec(1, OCT_STATE),
                  wspec(b, OCT_STATE), wspec(b, OCT_STATE)],
        out_specs=[pl.BlockSpec((1, rb, qq), lambda o, i: (o, i, 0)), wspec(b, OCT_STATE), wspec(b, OCT_STATE)],
        out_shape=[jax.ShapeDtypeStruct((N_OCT, rows, qq), F32),
                   jax.ShapeDtypeStruct((N_OCT, b, OCT_STATE), F32),
                   jax.ShapeDtypeStruct((N_OCT, b, OCT_STATE), F32)],
        scratch_shapes=[pltpu.VMEM((rb, OCT_STATE), F32) for _ in range(4)]
        + [pltpu.VMEM((b, OCT_STATE), F32) for _ in range(2)] + [pltpu.VMEM((qq, qq), BF16)],
        compiler_params=_cparams("parallel", "arbitrary"),
        name="s5_scan_octets",
    )(x, *ops8, h0r8, h0i8)
    y = y.reshape(N_OCT, nc, b, q, LANES).transpose(2, 1, 3, 0, 4).reshape(b, l, SSM_WIDTH)
    unpack = lambda h: h.transpose(1, 0, 2).reshape(b, SSM_GROUPS, SSM_STATE)
    return y, unpack(hlr), unpack(hli)


def _s5_glu_kernel(y_ref, u_ref, d_ref, w_ref, b_ref, o_ref):
    v = _gelu(y_ref[...] + d_ref[...] * u_ref[...])
    o_ref[...] = v * jax.nn.sigmoid(_dot(v.astype(BF16), w_ref[...]) + b_ref[...])


def _s5_glu(y, u, d, w, bias, tm=512):
    rows, n = y.shape
    tm = min(tm, rows)
    return pl.pallas_call(
        _s5_glu_kernel,
        grid=(rows // tm,),
        in_specs=[pl.BlockSpec((tm, n), lambda i: (i, 0)), pl.BlockSpec((tm, n), lambda i: (i, 0)),
                  pl.BlockSpec((1, n), lambda i: (0, 0)), pl.BlockSpec((n, n), lambda i: (0, 0)),
                  pl.BlockSpec((1, n), lambda i: (0, 0))],
        out_specs=pl.BlockSpec((tm, n), lambda i: (i, 0)),
        out_shape=jax.ShapeDtypeStruct((rows, n), F32),
        compiler_params=_cparams("parallel"),
        name="s5_glu",
    )(y, u, d.reshape(1, n), w, bias.reshape(1, n))


CMP_PAGES = 16
CMP_PITCH = PAGE // CMP_STRIDE + 1
CMP_GROUP_SIZES = (4, 6, 6)


def _cmp_group_pages():
    out, start = [], 0
    for size in CMP_GROUP_SIZES:
        out.append(range(start, start + size))
        start += size
    return out
SUBS = PAGE // CMP_STRIDE


def _cmp_weights(pe, w1, b1, w2):
    w = w1.reshape(2, CMP_STRIDE // 2, 2, 2, HEAD_DIM, CMP_HIDDEN)
    eye = jnp.eye(2, dtype=w1.dtype)
    wp = jnp.einsum('apjcef,cd->pjcedaf', w, eye)
    wp = wp.reshape(CMP_STRIDE // 2, 2 * 2 * HEAD_DIM, 2 * 2 * CMP_HIDDEN)
    bias = (jnp.einsum('jce,jcef->cf', pe, w1, precision=lax.Precision.HIGHEST) + b1).reshape(1, 2 * CMP_HIDDEN)
    w2b = jnp.einsum('cfe,cd->cfde', w2, eye).reshape(2 * CMP_HIDDEN, 2 * HEAD_DIM)
    return wp.astype(BF16), bias.astype(F32), w2b.astype(BF16)


def _cmp_tokens_kernel(pt_ref, *refs, transposed):
    n = CMP_PAGES * SUBS
    if transposed:
        nblk = CMP_PAGES
        page_refs = refs[:nblk]
        wp_ref, bias_ref, w2_ref, o_ref, carry_ref = refs[nblk:nblk + 5]
        xs_refs = refs[nblk + 5:]
        group_pages = _cmp_group_pages()
        where = {k: (gi, (k - pages[0]) * KV_HEADS) for gi, pages in enumerate(group_pages) for k in pages}

        def stage(pages):
            for k in pages:
                gi, base = where[k]
                for h in range(KV_HEADS):
                    x = page_refs[k][0, h].T
                    for v in range(PAGE // SUBLANES):
                        n, j0 = divmod(v * SUBLANES, CMP_STRIDE)
                        xs_refs[gi][base + h, pl.ds(j0 * CMP_PITCH + n, SUBLANES, stride=CMP_PITCH), :] = (
                            x[v * SUBLANES:(v + 1) * SUBLANES, :])

        def rows_of(i, j):
            gi, base = where[i // KV_HEADS]
            return xs_refs[gi][base + i % KV_HEADS, pl.ds(j * CMP_PITCH, SUBS), :]
    else:
        nblk = CMP_PAGES * KV_HEADS
        page_refs = refs[:nblk]
        wp_ref, bias_ref, w2_ref, o_ref, carry_ref = refs[nblk:]

        def rows_of(i, j):
            return page_refs[i][0, pl.ds(j, SUBS, stride=CMP_STRIDE), :]

    @pl.when(pl.program_id(1) == 0)
    def _():
        carry_ref[...] = jnp.zeros_like(carry_ref)

    accs = []
    for pages in (_cmp_group_pages() if transposed else [range(CMP_PAGES)]):
        ng = len(pages) * SUBS
        if transposed:
            stage(pages)
        acc = None
        for jp in range(CMP_STRIDE // 2):
            halves = []
            for jj in range(2):
                j = 2 * jp + jj
                rows = [rows_of(k * KV_HEADS + h, j) for h in range(KV_HEADS) for k in pages]
                halves.append(jnp.concatenate(rows, axis=0))
            lhs = jnp.concatenate(halves, axis=1).astype(BF16)
            part = _dot(lhs, wp_ref[jp])
            acc = part if acc is None else acc + part
        accs.append(acc)
    row = lax.broadcasted_iota(jnp.int32, (n, 1), 0)
    outs = []
    for h in range(KV_HEADS):
        p = jnp.concatenate([a[h * (a.shape[0] // KV_HEADS):(h + 1) * (a.shape[0] // KV_HEADS)] for a in accs],
                            axis=0)
        first = jnp.concatenate([p[:, 0:CMP_HIDDEN], p[:, 2 * CMP_HIDDEN:3 * CMP_HIDDEN]], axis=1)
        second = jnp.concatenate([p[:, CMP_HIDDEN:2 * CMP_HIDDEN], p[:, 3 * CMP_HIDDEN:]], axis=1)
        prev = jnp.where(row == 0, carry_ref[h:h + 1, :], pltpu.roll(first, 1, axis=0))
        carry_ref[h:h + 1, :] = first[n - 1:n, :]
        hid = _gelu(prev + second + bias_ref[...])
        outs.append(_dot(hid.astype(BF16), w2_ref[...]))
    o_ref[0] = jnp.concatenate(outs, axis=1)


def _cmp_tokens(pool, page_table, wp, bias, w2b, transposed):
    b, n_pages = page_table.shape
    steps = n_pages // CMP_PAGES
    n = CMP_PAGES * SUBS

    def page_spec(k, h):
        return pl.BlockSpec((1, PAGE, 2 * HEAD_DIM), lambda i, s, pt: (pt[i, s * CMP_PAGES + k], 0, h))

    def page_spec_t(k):
        return pl.BlockSpec((1, KV_HEADS, 2 * HEAD_DIM, PAGE), lambda i, s, pt: (pt[i, s * CMP_PAGES + k], 0, 0, 0))

    scratch = [pltpu.VMEM((SUBLANES, 2 * CMP_HIDDEN), F32)]
    if transposed:
        scratch += [pltpu.VMEM((len(pages) * KV_HEADS, CMP_STRIDE * CMP_PITCH, 2 * HEAD_DIM), F32)
                    for pages in _cmp_group_pages()]
        page_specs = [page_spec_t(k) for k in range(CMP_PAGES)]
    else:
        page_specs = [page_spec(k, h) for k in range(CMP_PAGES) for h in range(KV_HEADS)]
    grid_spec = pltpu.PrefetchScalarGridSpec(
        num_scalar_prefetch=1,
        grid=(b, steps),
        in_specs=page_specs + [
            pl.BlockSpec(wp.shape, lambda i, s, pt: (0, 0, 0)),
            pl.BlockSpec(bias.shape, lambda i, s, pt: (0, 0)),
            pl.BlockSpec(w2b.shape, lambda i, s, pt: (0, 0))],
        out_specs=pl.BlockSpec((1, n, KV_WIDTH), lambda i, s, pt: (i, s, 0)),
        scratch_shapes=scratch,
    )
    return pl.pallas_call(
        functools.partial(_cmp_tokens_kernel, transposed=transposed),
        grid_spec=grid_spec,
        out_shape=jax.ShapeDtypeStruct((b, steps * n, KV_WIDTH), F32),
        compiler_params=_cparams("parallel", "arbitrary"),
        name="cmp_tokens_t" if transposed else "cmp_tokens",
    )(page_table, *([pool] * len(page_specs)), wp, bias, w2b)


def _cover_matrix(n_tok, n_sel):
    i = np.arange(n_tok)[:, None]
    start = (i - 1) * CMP_STRIDE
    sj = np.arange(n_sel)[None, :] * SEL_BLOCK
    cov = (start < sj + SEL_BLOCK) & (start + CMP_BLOCK > sj) & (i >= 1)
    return cov.astype(np.float32)


KT_UNROLL = 2


def _split_hi_lo(x):
    hi = x.astype(BF16)
    lo = (x - hi.astype(F32)).astype(BF16)
    return hi, lo


def _nsa_prompt_kernel(q_ref, ckv_ref, kslc_ref, kwin_ref, gates_ref, cover_ref, o_ref, sel_ref, acc_ref,
                       s_ref, p_ref, *, tq, tk):
    q0 = pl.program_id(1) * tq
    qt = (q_ref[0] * SCALE).T
    gt = jax.nn.sigmoid(gates_ref[0]).T
    qpos = q0 + lax.broadcasted_iota(jnp.int32, (1, tq), 1)
    lane = lax.broadcasted_iota(jnp.int32, (1, GROUP * tq), 1)
    n_cmp = ckv_ref.shape[1]
    n_sel = cover_ref.shape[0]
    per_tile = tk // SEL_BLOCK
    wide = GROUP * tq

    def tile4(x):
        return jnp.concatenate([x] * GROUP, axis=1)

    ik = lax.broadcasted_iota(jnp.int32, (tk, 1), 0)
    rel = ik - lax.broadcasted_iota(jnp.int32, (1, tq), 1)
    row64 = lax.broadcasted_iota(jnp.int32, (HEAD_DIM, 1), 0)
    lane128 = lax.broadcasted_iota(jnp.int32, (1, 2 * HEAD_DIM), 1)
    kfeat = jnp.where((lane128 == HEAD_DIM) | (lane128 == HEAD_DIM + 1), ik.astype(F32), 0.0)
    q4, slope4, slope2, qaug = [], [], [], []
    for h in range(KV_HEADS):
        qf = jnp.concatenate([qt[(h * GROUP + g) * HEAD_DIM:(h * GROUP + g + 1) * HEAD_DIM, :]
                              for g in range(GROUP)], axis=1)
        q4.append(qf.astype(BF16))
        sl = jnp.zeros((1, wide), F32)
        for g in range(GROUP):
            sl = jnp.where((lane >= g * tq) & (lane < (g + 1) * tq), SLOPES[h * GROUP + g], sl)
        slope4.append(sl)
        sl2 = sl * LOG2E
        slope2.append(sl2)
        sl_hi = sl2.astype(BF16).astype(F32)
        extra = jnp.where(row64 == 0, sl_hi, jnp.where(row64 == 1, sl2 - sl_hi, 0.0))
        qaug.append(jnp.concatenate([qf * LOG2E, extra], axis=0).astype(BF16))

    o_cmp = []
    slot = lax.broadcasted_iota(jnp.int32, (n_cmp, 1), 0)
    dist = qpos - ((slot + 1) * CMP_STRIDE - 1)
    cmask = tile4(((dist >= 0) & (slot >= 1)).astype(F32))
    cdist4 = tile4(dist.astype(F32))
    cov = cover_ref[...].astype(BF16)
    jj = lax.broadcasted_iota(jnp.int32, (n_sel, 1), 0)
    cur = qpos // SEL_BLOCK
    forced = (jj == 0) | (jj == cur) | (jj == cur - 1)
    valid = jj * SEL_BLOCK <= qpos
    for h in range(KV_HEADS):
        ckv = ckv_ref[0, :, h * 2 * HEAD_DIM:(h + 1) * 2 * HEAD_DIM]
        ck = ckv[:, :HEAD_DIM].astype(BF16)
        cvt = ckv.T[HEAD_DIM:, :].astype(BF16)
        s = _dot(ck, q4[h]) - slope4[h] * cdist4
        s = jnp.where(cmask > 0, s, NEG)
        p = jnp.exp(s - jnp.max(s, axis=0, keepdims=True)) * cmask
        p = p / jnp.maximum(jnp.sum(p, axis=0, keepdims=True), 1e-30)
        o_cmp.append(_dot(cvt, p.astype(BF16)))
        psum = p[:, 0:tq]
        for g in range(1, GROUP):
            psum = psum + p[:, g * tq:(g + 1) * tq]
        p_hi, p_lo = _split_hi_lo(psum)
        imp = _dot(cov, p_hi) + _dot(cov, p_lo)
        imp = jnp.where(forced, BIG, jnp.where(valid, imp, NEG))
        cnt = jnp.zeros((n_sel, tq), F32)
        for j2 in range(n_sel):
            r = imp[j2:j2 + 1, :]
            cnt = cnt + jnp.where(r > imp, 1.0, jnp.where((r == imp) & (j2 < jj), 1.0, 0.0))
        sel = jnp.where(cnt < SEL_TOP, BIG, NEG)
        for kk in range(n_sel // per_tile):
            sel_ref[h, kk] = sel[kk * per_tile:(kk + 1) * per_tile, :]

    half = ik // SEL_BLOCK
    ones_rows = jnp.ones((SUBLANES, tk), F32)

    def slc_cap(h, kt, d):
        rows = sel_ref[h, kt]
        cap = rows[0:1, :]
        for r in range(1, per_tile):
            cap = jnp.where(half == r, rows[r:r + 1, :], cap)
        return jnp.where(d >= 0, cap, NEG)

    def win_cap(h, kt, d):
        return jnp.where(d >= 0, jnp.where(d < WINDOW, BIG, NEG), NEG)

    hi = (q0 + tq + tk - 1) // tk
    lo_win = jnp.maximum(q0 - WINDOW, 0) // tk
    branches = ((kslc_ref, slc_cap, 0), (kwin_ref, win_cap, lo_win))
    n_chain = len(branches) * KV_HEADS
    for c in range(n_chain):
        acc_ref[c] = jnp.zeros((HEAD_DIM + SUBLANES, wide), F32)

    def make_body(active):
        def body(i, ms):
            ms = list(ms)
            tiles = []
            for sub in range(KT_UNROLL):
                kt_raw = KT_UNROLL * i + sub
                kt = jnp.minimum(kt_raw, hi - 1)
                k0 = pl.multiple_of(kt * tk, tk)
                tiles.append((kt_raw, kt, k0))
                for br in active:
                    kv_ref = branches[br][0]
                    for h in range(KV_HEADS):
                        kv = kv_ref[0, pl.ds(k0, tk), h * 2 * HEAD_DIM:(h + 1) * 2 * HEAD_DIM]
                        k_aug = jnp.where(lane128 < HEAD_DIM, kv, kfeat).astype(BF16)
                        s_ref[sub, br * KV_HEADS + h] = _dot(k_aug, qaug[h])
            for sub, (kt_raw, kt, k0) in enumerate(tiles):
                k0f = k0.astype(F32)
                for br in active:
                    kv_ref, cap_fn, lo_b = branches[br]
                    live = (kt_raw < hi) & (kt_raw >= lo_b)
                    d = jnp.where(live, (q0 - k0) - rel, -1)
                    alphas = []
                    for h in range(KV_HEADS):
                        c = br * KV_HEADS + h
                        sc = jnp.minimum(s_ref[sub, c], tile4(cap_fn(h, kt, d)))
                        off = slope2[h] * k0f
                        m_new = jnp.maximum(ms[c], jnp.max(sc, axis=0, keepdims=True) + off)
                        p_ref[sub, c] = jnp.exp2(sc - (m_new - off)).astype(BF16)
                        alphas.append(jnp.exp2(ms[c] - m_new))
                        ms[c] = m_new
                    for h in range(KV_HEADS):
                        c = br * KV_HEADS + h
                        kv = kv_ref[0, pl.ds(k0, tk), h * 2 * HEAD_DIM:(h + 1) * 2 * HEAD_DIM]
                        vt = jnp.concatenate([kv.T[HEAD_DIM:, :], ones_rows], axis=0).astype(BF16)
                        acc_ref[c] = alphas[h] * acc_ref[c] + _dot(vt, p_ref[sub, c])
            return tuple(ms)
        return body

    ms0 = tuple(jnp.full((1, wide), NEG, F32) for _ in range(n_chain))
    t_win = lo_win // KT_UNROLL
    t_end = (hi + KT_UNROLL - 1) // KT_UNROLL
    ms1 = lax.fori_loop(0, t_win, make_body((0,)), ms0)
    lax.fori_loop(t_win, t_end, make_body((0, 1)), ms1)
    o_att = [acc_ref[c, :HEAD_DIM, :] / jnp.maximum(acc_ref[c, HEAD_DIM:HEAD_DIM + 1, :], 1e-30)
             for c in range(n_chain)]
    o_slc, o_win = o_att[:KV_HEADS], o_att[KV_HEADS:]

    out_rows = []
    for h in range(KV_HEADS):
        for g in range(GROUP):
            hh = h * GROUP + g
            sl = slice(g * tq, (g + 1) * tq)
            out_rows.append(gt[hh:hh + 1, :] * o_cmp[h][:, sl]
                            + gt[NSA_HEADS + hh:NSA_HEADS + hh + 1, :] * o_slc[h][:, sl]
                            + gt[2 * NSA_HEADS + hh:2 * NSA_HEADS + hh + 1, :] * o_win[h][:, sl])
    o_ref[0] = jnp.concatenate(out_rows, axis=0).T


def _nsa_prompt_attn(q, ckv, kv_slc, kv_win, gates, tq=128, tk=128):
    b, t, _ = q.shape
    n_cmp = ckv.shape[1]
    n_sel = t // SEL_BLOCK
    cover_t = jnp.asarray(_cover_matrix(n_cmp, n_sel).T)
    kern = functools.partial(_nsa_prompt_kernel, tq=tq, tk=tk)
    return pl.pallas_call(
        kern,
        grid=(b, t // tq),
        in_specs=[pl.BlockSpec((1, tq, NSA_WIDTH), lambda i, j: (i, j, 0)),
                  pl.BlockSpec((1, n_cmp, KV_WIDTH), lambda i, j: (i, 0, 0)),
                  pl.BlockSpec((1, t, KV_WIDTH), lambda i, j: (i, 0, 0)),
                  pl.BlockSpec((1, t, KV_WIDTH), lambda i, j: (i, 0, 0)),
                  pl.BlockSpec((1, tq, LANES), lambda i, j: (i, j, 0)),
                  pl.BlockSpec((n_sel, n_cmp), lambda i, j: (0, 0))],
        out_specs=pl.BlockSpec((1, tq, NSA_WIDTH), lambda i, j: (i, j, 0)),
        out_shape=jax.ShapeDtypeStruct((b, t, NSA_WIDTH), F32),
        scratch_shapes=[pltpu.VMEM((KV_HEADS, n_sel * SEL_BLOCK // tk, tk // SEL_BLOCK, tq), F32),
                        pltpu.VMEM((2 * KV_HEADS, HEAD_DIM + SUBLANES, GROUP * tq), F32),
                        pltpu.VMEM((KT_UNROLL, 2 * KV_HEADS, tk, GROUP * tq), F32),
                        pltpu.VMEM((KT_UNROLL, 2 * KV_HEADS, tk, GROUP * tq), BF16)],
        compiler_params=_cparams("parallel", "parallel"),
        name="nsa_prompt_attn",
    )(q, ckv, kv_slc, kv_win, gates, cover_t)


TS = SUBLANES
SEL_LANES = 384


def _nsa_sample_select_kernel(q_ref, ckv_ref, cover_ref, ocmp_ref, idx_ref, *, nbatch, past_len, n_sel):
    n_cmp = ckv_ref.shape[1]
    rowi = lax.broadcasted_iota(jnp.int32, (GROUP * TS, 1), 0)
    tpos = past_len + rowi % TS
    slot = lax.broadcasted_iota(jnp.int32, (1, n_cmp), 1)
    dist = tpos - ((slot + 1) * CMP_STRIDE - 1)
    mask = ((dist >= 0) & (slot >= 1)).astype(F32)
    jj = lax.broadcasted_iota(jnp.int32, (1, SEL_LANES), 1)
    jjf = jj.astype(F32)
    qp8 = past_len + lax.broadcasted_iota(jnp.int32, (TS, 1), 0)
    cur = qp8 // SEL_BLOCK
    forced = (jj == 0) | (jj == cur) | (jj == cur - 1)
    valid = jj * SEL_BLOCK <= qp8
    lane = lax.broadcasted_iota(jnp.int32, (1, LANES), 1)
    cov = cover_ref[...].astype(BF16)
    chains = [(bb, h) for bb in range(nbatch) for h in range(KV_HEADS)]
    distf = dist.astype(F32)
    scores, probs, outs, imps = {}, {}, {}, {}
    for bb, h in chains:
        q = q_ref[bb] * SCALE
        qh = jnp.concatenate([q[:, (h * GROUP + g) * HEAD_DIM:(h * GROUP + g + 1) * HEAD_DIM]
                              for g in range(GROUP)], axis=0).astype(BF16)
        slope = jnp.zeros((GROUP * TS, 1), F32)
        for g in range(GROUP):
            slope = jnp.where(rowi // TS == g, SLOPES[h * GROUP + g], slope)
        ck = ckv_ref[bb, :, h * 2 * HEAD_DIM:h * 2 * HEAD_DIM + HEAD_DIM].astype(BF16)
        scores[bb, h] = jnp.where(mask > 0, _dot_nt(qh, ck) - slope * distf, NEG)
    for c in chains:
        s = scores[c]
        p = jnp.exp(s - jnp.max(s, axis=-1, keepdims=True)) * mask
        probs[c] = p / jnp.maximum(jnp.sum(p, axis=-1, keepdims=True), 1e-30)
    for bb, h in chains:
        p = probs[bb, h]
        cv = ckv_ref[bb, :, h * 2 * HEAD_DIM + HEAD_DIM:(h + 1) * 2 * HEAD_DIM].astype(BF16)
        outs[bb, h] = _dot(p.astype(BF16), cv)
        psum = p[0:TS]
        for g in range(1, GROUP):
            psum = psum + p[g * TS:(g + 1) * TS]
        p_hi, p_lo = _split_hi_lo(psum)
        imp = _dot(p_hi, cov) + _dot(p_lo, cov)
        imp = jnp.where(forced, BIG, jnp.where(valid, imp, NEG))
        imps[bb, h] = jnp.where(jj < n_sel, imp, -3e38)
    for bb in range(nbatch):
        ocmp_ref[bb] = jnp.concatenate([outs[bb, h][g * TS:(g + 1) * TS, :]
                                        for h in range(KV_HEADS) for g in range(GROUP)], axis=-1)
    idx_out = [jnp.zeros((TS, LANES), jnp.int32) for _ in range(nbatch)]
    for k in range(SEL_TOP):
        for bb in range(nbatch):
            for h in range(KV_HEADS):
                imp = imps[bb, h]
                best = jnp.max(imp, axis=-1, keepdims=True)
                pick = jnp.min(jnp.where(imp == best, jjf, float(SEL_LANES)), axis=-1, keepdims=True)
                idx_out[bb] = jnp.where(lane == h * SEL_TOP + k, pick.astype(jnp.int32), idx_out[bb])
                imps[bb, h] = jnp.where(jjf == pick, -3e38, imp)
    for bb in range(nbatch):
        idx_ref[bb] = idx_out[bb]


def _nsa_sample_select(q, ckv, past_len, nbatch=4):
    b = q.shape[0]
    n_cmp = ckv.shape[1]
    n_sel = -(-(past_len + 4) // SEL_BLOCK)
    cov = np.zeros((n_cmp, SEL_LANES), np.float32)
    cov[:, :n_sel] = _cover_matrix(n_cmp, n_sel)
    kern = functools.partial(_nsa_sample_select_kernel, nbatch=nbatch, past_len=past_len, n_sel=n_sel)
    return pl.pallas_call(
        kern,
        grid=(b // nbatch,),
        in_specs=[pl.BlockSpec((nbatch, TS, NSA_WIDTH), lambda i: (i, 0, 0)),
                  pl.BlockSpec((nbatch, n_cmp, KV_WIDTH), lambda i: (i, 0, 0)),
                  pl.BlockSpec((n_cmp, SEL_LANES), lambda i: (0, 0))],
        out_specs=[pl.BlockSpec((nbatch, TS, NSA_WIDTH), lambda i: (i, 0, 0)),
                   pl.BlockSpec((nbatch, TS, LANES), lambda i: (i, 0, 0))],
        out_shape=[jax.ShapeDtypeStruct((b, TS, NSA_WIDTH), F32),
                   jax.ShapeDtypeStruct((b, TS, LANES), jnp.int32)],
        compiler_params=_cparams("parallel"),
        name="nsa_sample_select",
    )(q, ckv, jnp.asarray(cov))


def _nsa_sample_attend_kernel(idx_ref, pt_ref, q_ref, ocmp_ref, gates_ref, knew_ref, wbuf_ref, wnew_ref,
                              pool_ref, o_ref, buf_ref, sem, *, t_dec, past_len):
    b = pl.program_id(0)
    last_blk = past_len // SEL_BLOCK
    per_page = PAGE // SEL_BLOCK

    nb = pl.num_programs(0)
    slot = b % 2

    def block_of(t, h, k, bb=None):
        bb = b if bb is None else bb
        return idx_ref[((bb * t_dec + t) * KV_HEADS + h) * SEL_TOP + k]

    def page_copies(bb, sl):
        out = []
        for t in range(t_dec):
            for h in range(KV_HEADS):
                for k in range(SEL_TOP):
                    page = pt_ref[bb, jnp.minimum(block_of(t, h, k, bb), last_blk - 1) // per_page]
                    out.append(pltpu.make_async_copy(pool_ref.at[page, h], buf_ref.at[sl, t, h, k], sem.at[sl]))
        return out

    @pl.when(b == 0)
    def _():
        for cp in page_copies(b, slot):
            cp.start()

    @pl.when(b + 1 < nb)
    def _():
        for cp in page_copies(b + 1, 1 - slot):
            cp.start()

    for cp in page_copies(b, slot):
        cp.wait()

    def rows_to_t(x):
        return jnp.concatenate([x, jnp.zeros((PAGE - TS, 2 * HEAD_DIM), F32)], axis=0).T

    rowg = lax.broadcasted_iota(jnp.int32, (SUBLANES, 1), 0)
    slope_col = [jnp.zeros((SUBLANES, 1), F32) for _ in range(KV_HEADS)]
    for h in range(KV_HEADS):
        for g in range(GROUP):
            slope_col[h] = jnp.where(rowg == g, SLOPES[h * GROUP + g], slope_col[h])
    n_tok = SEL_TOP * PAGE
    lane = lax.broadcasted_iota(jnp.int32, (1, n_tok), 1)
    row = lane % PAGE
    row_half = row // SEL_BLOCK
    slot_of = lane // PAGE
    n_win = wbuf_ref.shape[-1]
    wlane = lax.broadcasted_iota(jnp.int32, (1, n_win + PAGE), 1)
    wpos = jnp.where(wlane < n_win, past_len - n_win + wlane, past_len + wlane - n_win)

    def masked_softmax(s, msk):
        p = jnp.exp(s - jnp.max(s, axis=-1, keepdims=True)) * msk
        return p / jnp.maximum(jnp.sum(p, axis=-1, keepdims=True), 1e-30)

    pairs = [(t, h) for h in range(KV_HEADS) for t in range(t_dec)]
    vw_ts, v_ts, s_slc, s_win, m_slc, m_win = {}, {}, {}, {}, {}, {}
    for h in range(KV_HEADS):
        hs = slice(h * 2 * HEAD_DIM, (h + 1) * 2 * HEAD_DIM)
        new_t = rows_to_t(knew_ref[0, :, hs])
        wnew_t = rows_to_t(wnew_ref[0, :, hs])
        kw_t = jnp.concatenate([wbuf_ref[0, h, 0], wnew_t[:HEAD_DIM]], axis=1).astype(BF16)
        vw_ts[h] = jnp.concatenate([wbuf_ref[0, h, 1], wnew_t[HEAD_DIM:]], axis=1).astype(BF16)
        for t in range(t_dec):
            qpos = past_len + t
            qh = (q_ref[0, t, h] * SCALE).astype(BF16)
            k_tiles, v_tiles = [], []
            tok = jnp.zeros((1, n_tok), jnp.int32)
            want_half = jnp.zeros((1, n_tok), jnp.int32)
            for k in range(SEL_TOP):
                blk = block_of(t, h, k)
                is_new = blk >= last_blk
                k_tiles.append(jnp.where(is_new, new_t[:HEAD_DIM], buf_ref[slot, t, h, k, 0]))
                v_tiles.append(jnp.where(is_new, new_t[HEAD_DIM:], buf_ref[slot, t, h, k, 1]))
                tok = jnp.where(slot_of == k, (blk // per_page) * PAGE + row, tok)
                want_half = jnp.where(slot_of == k, blk % per_page, want_half)
            k_t = jnp.concatenate(k_tiles, axis=1).astype(BF16)
            v_ts[t, h] = jnp.concatenate(v_tiles, axis=1).astype(BF16)
            d = qpos - tok
            m_slc[t, h] = jnp.where(d >= 0, jnp.where(row_half == want_half, 1.0, 0.0), 0.0)
            s_slc[t, h] = jnp.where(m_slc[t, h] > 0, _dot(qh, k_t) - slope_col[h] * d.astype(F32), NEG)
            dw = qpos - wpos
            m_win[t, h] = jnp.where(dw >= 0, jnp.where(dw < WINDOW, 1.0, 0.0), 0.0)
            s_win[t, h] = jnp.where(m_win[t, h] > 0, _dot(qh, kw_t) - slope_col[h] * dw.astype(F32), NEG)
    p_slc = {c: masked_softmax(s_slc[c], m_slc[c]) for c in pairs}
    p_win = {c: masked_softmax(s_win[c], m_win[c]) for c in pairs}
    for t, h in pairs:
        o_slc = _dot_nt(p_slc[t, h].astype(BF16), v_ts[t, h])
        o_win = _dot_nt(p_win[t, h].astype(BF16), vw_ts[h])
        gts = jax.nn.sigmoid(gates_ref[0, t, h])
        o_ref[0, t, h] = (gts[:, 0:1] * ocmp_ref[0, t, h] + gts[:, 1:2] * o_slc + gts[:, 2:3] * o_win)


def _nsa_sample_attend(idx, page_table, q5, ocmp5, gates5, kv_slc_new, win_buf_t, kv_win_new, pool_slc_t,
                       t_dec, past_len):
    b = q5.shape[0]
    n_win = win_buf_t.shape[-1]
    blk5 = (1, t_dec, KV_HEADS, SUBLANES, HEAD_DIM)
    grid_spec = pltpu.PrefetchScalarGridSpec(
        num_scalar_prefetch=2,
        grid=(b,),
        in_specs=[pl.BlockSpec(blk5, lambda i, *_: (i, 0, 0, 0, 0)),
                  pl.BlockSpec(blk5, lambda i, *_: (i, 0, 0, 0, 0)),
                  pl.BlockSpec((1, t_dec, KV_HEADS, SUBLANES, LANES), lambda i, *_: (i, 0, 0, 0, 0)),
                  pl.BlockSpec((1, TS, KV_WIDTH), lambda i, *_: (i, 0, 0)),
                  pl.BlockSpec((1, KV_HEADS, 2, HEAD_DIM, n_win), lambda i, *_: (i, 0, 0, 0, 0)),
                  pl.BlockSpec((1, TS, KV_WIDTH), lambda i, *_: (i, 0, 0)),
                  pl.BlockSpec(memory_space=pl.ANY)],
        out_specs=pl.BlockSpec(blk5, lambda i, *_: (i, 0, 0, 0, 0)),
        scratch_shapes=[pltpu.VMEM((2, t_dec, KV_HEADS, SEL_TOP, 2, HEAD_DIM, PAGE), F32),
                        pltpu.SemaphoreType.DMA((2,))],
    )
    kern = functools.partial(_nsa_sample_attend_kernel, t_dec=t_dec, past_len=past_len)
    return pl.pallas_call(
        kern,
        grid_spec=grid_spec,
        out_shape=jax.ShapeDtypeStruct((b, t_dec, KV_HEADS, SUBLANES, HEAD_DIM), F32),
        compiler_params=_cparams("arbitrary"),
        name="nsa_sample_attend",
    )(idx[:, :t_dec, :KV_HEADS * SEL_TOP].reshape(-1), page_table, q5, ocmp5, gates5, kv_slc_new, win_buf_t,
      kv_win_new, pool_slc_t)


NSA_SPLITS = ((0, NSA_WIDTH), (NSA_WIDTH, KV_WIDTH), (NSA_WIDTH + KV_WIDTH, KV_WIDTH),
              (NSA_WIDTH + 2 * KV_WIDTH, KV_WIDTH), (NSA_WIDTH + 3 * KV_WIDTH, MEM_WIDTH),
              (NSA_WIDTH + 3 * KV_WIDTH + MEM_WIDTH, LANES))


def _nsa_in_weight(w_in):
    o = NSA_WIDTH + 3 * KV_WIDTH
    n_gate = 3 * NSA_HEADS
    gates = jnp.pad(w_in[:, o:o + n_gate], ((0, 0), (0, LANES - n_gate)))
    return jnp.concatenate([w_in[:, :o], w_in[:, o + n_gate:], gates], axis=1).astype(BF16)


def _to_heads5(x, t_dec):
    b = x.shape[0]
    w = x.shape[-1] // NSA_HEADS
    x = x[:, :t_dec].reshape(b, t_dec, KV_HEADS, GROUP, w)
    return jnp.pad(x, ((0, 0), (0, 0), (0, 0), (0, SUBLANES - GROUP), (0, 0)))


def kernel(x_prompt, x_sample, mem_prompt, state_ssm_re, state_ssm_im, cache_cmp_kv, cache_slc_kv, cache_win_kv, cache_mem_kv, page_table, norm_mix_pre, norm_mix_post, norm_ffn_pre, norm_ffn_post, w_out, w_mem_kv, w_ffn_in, w_ffn_out, ssm_w_in, ssm_a_re, ssm_a_im, ssm_log_dt, ssm_b_re, ssm_b_im, ssm_c_re, ssm_c_im, ssm_d, ssm_w_glu, ssm_b_glu, nsa_w_in, nsa_cmp_pe, nsa_cmp_w1, nsa_cmp_b1, nsa_cmp_w2):
    bp, seq, d = x_prompt.shape
    bs, t_dec, _ = x_sample.shape
    n_mem = mem_prompt.shape[1]
    depth = w_out.shape[0]
    past_len = page_table.shape[1] * PAGE
    d_ff = w_ffn_out.shape[1]
    chunk = 16

    xp = x_prompt.reshape(bp * seq, d)
    xs = jnp.pad(x_sample, ((0, 0), (0, TS - t_dec), (0, 0))).reshape(bs * TS, d)
    mem2d = mem_prompt.reshape(bp * n_mem, d)

    outs = {k: [] for k in ('ssm_re_p', 'ssm_im_p', 'ssm_re_s', 'ssm_im_s', 'cmp_p', 'slc_p', 'win_p',
                            'cmp_s', 'slc_s', 'win_s', 'mem_p')}
    for i in range(depth):
        j = i // 2
        mem_split = ((0, 2 * MEM_WIDTH),)
        mkv_p, mkv_p_t = _norm_proj(mem2d, norm_mix_pre[i], w_mem_kv[i].astype(BF16), mem_split, do_norm=False,
                                    t_splits=mem_split, seq=n_mem)
        mkv_p = mkv_p.reshape(bp, n_mem, 2 * MEM_WIDTH)
        outs['mem_p'].append(jnp.moveaxis(mkv_p_t.reshape(bp, MEM_HEADS, 2, HEAD_DIM, n_mem), -1, 1))
        mkv_s_t = _kv_cache_t(cache_mem_kv[i])
        if i % 2 == 0:
            w_in = ssm_w_in[j].astype(BF16)
            wglu = ssm_w_glu[j].astype(BF16)
            ssm_w = (ssm_a_re[j], ssm_a_im[j], ssm_log_dt[j], ssm_b_re[j], ssm_b_im[j], ssm_c_re[j], ssm_c_im[j])
            ssm_split = ((0, SSM_WIDTH), (SSM_WIDTH, MEM_WIDTH))
            u, qm, u_half = _norm_proj(xp, norm_mix_pre[i], w_in, ssm_split, h_splits=ssm_split[:1])
            zero = jnp.zeros((bp, SSM_GROUPS, SSM_STATE), F32)
            y, hlr, hli = _s5_scan_octets(u_half.reshape(bp, seq, SSM_WIDTH), zero, zero,
                                          _s5_operators(*ssm_w, chunk), chunk)
            m1p = _s5_glu(y.reshape(bp * seq, SSM_WIDTH), u, ssm_d[j], wglu, ssm_b_glu[j])
            m2p = _mem_attn(qm.reshape(bp, seq, MEM_WIDTH), mkv_p).reshape(bp * seq, MEM_WIDTH)
            outs['ssm_re_p'].append(hlr)
            outs['ssm_im_p'].append(hli)
            u, qm = _norm_proj(xs, norm_mix_pre[i], w_in, ssm_split)
            y, hlr, hli = _s5_scan(u.reshape(bs, TS, SSM_WIDTH)[:, :t_dec], state_ssm_re[j].astype(F32),
                                   state_ssm_im[j].astype(F32), _s5_operators(*ssm_w, t_dec), t_dec)
            y = jnp.pad(y, ((0, 0), (0, TS - t_dec), (0, 0)))
            m1s = _s5_glu(y.reshape(bs * TS, SSM_WIDTH), u, ssm_d[j], wglu, ssm_b_glu[j])
            m2s = _mem_attn_t(qm.reshape(bs, TS, MEM_WIDTH), mkv_s_t).reshape(bs * TS, MEM_WIDTH)
            outs['ssm_re_s'].append(hlr)
            outs['ssm_im_s'].append(hli)
        else:
            w_in = _nsa_in_weight(nsa_w_in[j])
            wp, cbias, w2b = _cmp_weights(nsa_cmp_pe[j], nsa_cmp_w1[j], nsa_cmp_b1[j], nsa_cmp_w2[j])
            kv_shape = (KV_HEADS, 2, HEAD_DIM)
            q, kc, ks, kw, qm, gates, kc_t, ks_t, kw_t = _norm_proj(
                xp, norm_mix_pre[i], w_in, NSA_SPLITS, t_splits=NSA_SPLITS[1:4], seq=seq)
            kc3, ks3, kw3 = (a.reshape(bp, seq, KV_WIDTH) for a in (kc, ks, kw))

            def rows_major(a_t):
                return jnp.moveaxis(a_t.reshape((bp,) + kv_shape + (a_t.shape[-1],)), -1, 1)
            ident = jnp.arange(bp * (seq // PAGE), dtype=jnp.int32).reshape(bp, seq // PAGE)
            ckv = _cmp_tokens(kc.reshape(bp * seq // PAGE, PAGE, KV_WIDTH), ident, wp, cbias, w2b, False)
            m1p = _nsa_prompt_attn(q.reshape(bp, seq, NSA_WIDTH), ckv, ks3, kw3,
                                   gates.reshape(bp, seq, LANES)).reshape(bp * seq, NSA_WIDTH)
            m2p = _mem_attn(qm.reshape(bp, seq, MEM_WIDTH), mkv_p).reshape(bp * seq, MEM_WIDTH)
            outs['cmp_p'].append(rows_major(kc_t))
            outs['slc_p'].append(rows_major(ks_t))
            outs['win_p'].append(rows_major(kw_t[:, :, seq - min(WINDOW, seq):]))
            q, kc, ks, kw, qm, gates = _norm_proj(xs, norm_mix_pre[i], w_in, NSA_SPLITS)
            kc3, ks3, kw3 = (a.reshape(bs, TS, KV_WIDTH) for a in (kc, ks, kw))
            pool_cmp_t = _kv_cache_t(cache_cmp_kv[j])
            pool_cmp_t = pool_cmp_t.reshape(pool_cmp_t.shape[0], KV_HEADS, 2 * HEAD_DIM, PAGE)
            pool_slc_t = _kv_cache_t(cache_slc_kv[j])
            win_buf_t = _kv_cache_t(cache_win_kv[j])
            ckv = _cmp_tokens(pool_cmp_t, page_table, wp, cbias, w2b, True)
            q3 = q.reshape(bs, TS, NSA_WIDTH)
            ocmp, idx = _nsa_sample_select(q3, ckv, past_len)
            gates3 = gates.reshape(bs, TS, LANES)[:, :, :3 * NSA_HEADS].reshape(bs, TS, 3, NSA_HEADS)
            gates5 = _to_heads5(gates3.transpose(0, 1, 3, 2).reshape(bs, TS, NSA_HEADS * 3), t_dec)
            gates5 = jnp.pad(gates5, ((0, 0),) * 4 + ((0, LANES - 3),))
            o5 = _nsa_sample_attend(idx, page_table, _to_heads5(q3, t_dec), _to_heads5(ocmp, t_dec), gates5,
                                    ks3, win_buf_t, kw3, pool_slc_t, t_dec, past_len)
            o = o5[:, :, :, :GROUP].reshape(bs, t_dec, NSA_WIDTH)
            m1s = jnp.pad(o, ((0, 0), (0, TS - t_dec), (0, 0))).reshape(bs * TS, NSA_WIDTH)
            m2s = _mem_attn_t(qm.reshape(bs, TS, MEM_WIDTH), mkv_s_t).reshape(bs * TS, MEM_WIDTH)
            outs['cmp_s'].append(kc3[:, :t_dec].reshape((bs, t_dec) + kv_shape))
            outs['slc_s'].append(ks3[:, :t_dec].reshape((bs, t_dec) + kv_shape))
            kv_w = jnp.concatenate([cache_win_kv[j], kw3[:, :t_dec].reshape((bs, t_dec) + kv_shape)], axis=1)
            n_keep = min(WINDOW, past_len + t_dec)
            outs['win_s'].append(kv_w[:, kv_w.shape[1] - n_keep:])
        wo = w_out[i].astype(BF16)
        wg = w_ffn_in[i][:, :d_ff].astype(BF16)
        wu = w_ffn_in[i][:, d_ff:].astype(BF16)
        wd = w_ffn_out[i].astype(BF16)
        w1 = m1p.shape[1]
        args = (wo[:w1], wo[w1:], norm_mix_post[i], norm_ffn_pre[i], wg, wu, wd, norm_ffn_post[i])
        xp = _post(xp, m1p, m2p, *args)
        xs = _post(xs, m1s, m2s, *args)

    st = lambda k: jnp.stack(outs[k])
    y_sample = xs.reshape(bs, TS, d)[:, :t_dec]
    return (xp.reshape(bp, seq, d), y_sample, st('ssm_re_p'), st('ssm_im_p'), st('ssm_re_s'), st('ssm_im_s'),
            st('cmp_p'), st('slc_p'), st('win_p'), st('cmp_s'), st('slc_s'), st('win_s'), st('mem_p'))
```

```python
import functools
import math

import numpy as np
import jax
import jax.numpy as jnp
from jax import lax
from jax.experimental import pallas as pl
from jax.experimental.pallas import tpu as pltpu

F32 = jnp.float32
BF16 = jnp.bfloat16

D_MODEL = 1024
PAGE = 128
MEM_HEADS = 4
HEAD_DIM = 64
MEM_WIDTH = MEM_HEADS * HEAD_DIM
SSM_WIDTH = D_MODEL - MEM_WIDTH
SSM_GROUP = 16
SSM_GROUPS = SSM_WIDTH // SSM_GROUP
SSM_STATE = 64
NSA_HEADS = 12
KV_HEADS = 3
GROUP = NSA_HEADS // KV_HEADS
NSA_WIDTH = NSA_HEADS * HEAD_DIM
KV_WIDTH = KV_HEADS * 2 * HEAD_DIM
CMP_BLOCK = 32
CMP_STRIDE = 16
CMP_HIDDEN = 2 * HEAD_DIM
SEL_BLOCK = 64
SEL_TOP = 16
WINDOW = 512
RMS_EPS = 1e-6
NEG = -1e30
BIG = 1e30
SCALE = HEAD_DIM ** -0.5
LANES = 128
SUBLANES = 8
VMEM_LIMIT = 56 * 1024 * 1024
SLOPES = [2.0 ** (-8.0 * (h + 1) / NSA_HEADS) for h in range(NSA_HEADS)]
LOG2E = math.log2(math.e)


def _cparams(*sem):
    return pltpu.CompilerParams(dimension_semantics=sem, vmem_limit_bytes=VMEM_LIMIT)


def _rms(x, g):
    return x * lax.rsqrt(jnp.mean(x * x, axis=-1, keepdims=True) + RMS_EPS) * g


def _gelu(x):
    return 0.5 * x * (1.0 + jnp.tanh(math.sqrt(2.0 / math.pi) * (x + 0.044715 * (x * x * x))))


def _dot(a, b):
    return jnp.dot(a, b, preferred_element_type=F32)


def _dot_nt(a, b):
    return lax.dot_general(a, b, (((1,), (1,)), ((), ())), preferred_element_type=F32)


def _kv_cache_t(cache):
    n = cache.ndim
    return jnp.moveaxis(cache, n - 4, n - 1)


def _norm_proj_kernel(x_ref, g_ref, w_ref, *o_refs, splits, t_splits, h_splits, do_norm):
    x = x_ref[...]
    if do_norm:
        x = _rms(x, g_ref[...])
    z = _dot(x.astype(BF16), w_ref[...])
    for (start, width), o in zip(splits + h_splits, o_refs):
        o[...] = z[:, start:start + width].astype(o.dtype)
    for (start, width), o in zip(t_splits, o_refs[len(splits) + len(h_splits):]):
        o[0] = z[:, start:start + width].T


def _norm_proj(x, g, w, splits, do_norm=True, tm=512, t_splits=(), seq=None, h_splits=()):
    rows, d = x.shape
    n = w.shape[1]
    tm = min(tm, rows, seq) if t_splits else min(tm, rows)
    splits, t_splits, h_splits = tuple(splits), tuple(t_splits), tuple(h_splits)
    kern = functools.partial(_norm_proj_kernel, splits=splits, t_splits=t_splits, h_splits=h_splits,
                             do_norm=do_norm)
    out_specs = [pl.BlockSpec((tm, wd), lambda i: (i, 0)) for _, wd in splits + h_splits]
    out_shape = ([jax.ShapeDtypeStruct((rows, wd), F32) for _, wd in splits]
                 + [jax.ShapeDtypeStruct((rows, wd), BF16) for _, wd in h_splits])
    if t_splits:
        per_seq = seq // tm
        out_specs += [pl.BlockSpec((1, wd, tm), lambda i: (i // per_seq, 0, i % per_seq)) for _, wd in t_splits]
        out_shape += [jax.ShapeDtypeStruct((rows // seq, wd, seq), F32) for _, wd in t_splits]
    return pl.pallas_call(
        kern,
        grid=(rows // tm,),
        in_specs=[pl.BlockSpec((tm, d), lambda i: (i, 0)),
                  pl.BlockSpec((1, d), lambda i: (0, 0)),
                  pl.BlockSpec((d, n), lambda i: (0, 0))],
        out_specs=out_specs,
        out_shape=out_shape,
        compiler_params=_cparams("parallel"),
        name="norm_proj",
    )(x, g.reshape(1, d), w)


def _softmax_rows(s):
    p = jnp.exp(s - jnp.max(s, axis=-1, keepdims=True))
    return p, jnp.sum(p, axis=-1, keepdims=True)


def _mem_attn_kernel(q_ref, kv_ref, o_ref):
    q = q_ref[0]
    kv = kv_ref[0]
    outs = []
    for h in range(MEM_HEADS):
        qh = (q[:, h * HEAD_DIM:(h + 1) * HEAD_DIM] * SCALE).astype(BF16)
        k = kv[:, h * 2 * HEAD_DIM:h * 2 * HEAD_DIM + HEAD_DIM].astype(BF16)
        v = kv[:, h * 2 * HEAD_DIM + HEAD_DIM:(h + 1) * 2 * HEAD_DIM].astype(BF16)
        p, l = _softmax_rows(_dot_nt(qh, k))
        outs.append(_dot(p.astype(BF16), v) / l)
    o_ref[0] = jnp.concatenate(outs, axis=-1)


def _mem_attn(qm, mem_kv, tm=512):
    b, t, _ = qm.shape
    tm = min(tm, t)
    n_mem = mem_kv.shape[1]
    return pl.pallas_call(
        _mem_attn_kernel,
        grid=(b, t // tm),
        in_specs=[pl.BlockSpec((1, tm, MEM_WIDTH), lambda i, j: (i, j, 0)),
                  pl.BlockSpec((1, n_mem, 2 * MEM_WIDTH), lambda i, j: (i, 0, 0))],
        out_specs=pl.BlockSpec((1, tm, MEM_WIDTH), lambda i, j: (i, j, 0)),
        out_shape=jax.ShapeDtypeStruct((b, t, MEM_WIDTH), F32),
        compiler_params=_cparams("parallel", "parallel"),
        name="mem_attn",
    )(qm, mem_kv)


def _mem_attn_t_kernel(q_ref, kv_ref, o_ref, *, nbatch):
    pairs = [(bb, h) for bb in range(nbatch) for h in range(MEM_HEADS)]
    scores, probs, outs = {}, {}, {}
    for bb, h in pairs:
        qh = (q_ref[bb][:, h * HEAD_DIM:(h + 1) * HEAD_DIM] * SCALE).astype(BF16)
        scores[bb, h] = _dot(qh, kv_ref[bb, h, 0].astype(BF16))
    for c in pairs:
        probs[c] = _softmax_rows(scores[c])
    for bb, h in pairs:
        p, l = probs[bb, h]
        outs[bb, h] = _dot_nt(p.astype(BF16), kv_ref[bb, h, 1].astype(BF16)) / l
    for bb in range(nbatch):
        o_ref[bb] = jnp.concatenate([outs[bb, h] for h in range(MEM_HEADS)], axis=-1)


def _mem_attn_t(qm, mem_kv_t, nbatch=4):
    b, t, _ = qm.shape
    n_mem = mem_kv_t.shape[-1]
    return pl.pallas_call(
        functools.partial(_mem_attn_t_kernel, nbatch=nbatch),
        grid=(b // nbatch,),
        in_specs=[pl.BlockSpec((nbatch, t, MEM_WIDTH), lambda i: (i, 0, 0)),
                  pl.BlockSpec((nbatch, MEM_HEADS, 2, HEAD_DIM, n_mem), lambda i: (i, 0, 0, 0, 0))],
        out_specs=pl.BlockSpec((nbatch, t, MEM_WIDTH), lambda i: (i, 0, 0)),
        out_shape=jax.ShapeDtypeStruct((b, t, MEM_WIDTH), F32),
        compiler_params=_cparams("parallel"),
        name="mem_attn_t",
    )(qm, mem_kv_t)


def _post_kernel(x_ref, m1_ref, m2_ref, wo1_ref, wo2_ref, g1_ref, g2_ref, wg_ref, wu_ref, wd_ref, g3_ref,
                 o_ref):
    a = _dot(m1_ref[...].astype(BF16), wo1_ref[...]) + _dot(m2_ref[...].astype(BF16), wo2_ref[...])
    x1 = x_ref[...] + _rms(a, g1_ref[...])
    h = _rms(x1, g2_ref[...]).astype(BF16)
    gate = _dot(h, wg_ref[...])
    up = _dot(h, wu_ref[...])
    act = (gate * jax.nn.sigmoid(gate) * up).astype(BF16)
    f = _dot(act, wd_ref[...])
    o_ref[...] = x1 + _rms(f, g3_ref[...])


def _post(x, m1, m2, wo1, wo2, g1, g2, wg, wu, wd, g3, tm=512):
    rows, d = x.shape
    tm = min(tm, rows)
    w1, w2 = m1.shape[1], m2.shape[1]
    dff = wg.shape[1]

    def const(shape):
        return pl.BlockSpec(shape, lambda i: (0, 0), pipeline_mode=pl.Buffered(1))

    return pl.pallas_call(
        _post_kernel,
        grid=(rows // tm,),
        in_specs=[pl.BlockSpec((tm, d), lambda i: (i, 0)),
                  pl.BlockSpec((tm, w1), lambda i: (i, 0)),
                  pl.BlockSpec((tm, w2), lambda i: (i, 0)),
                  const((w1, d)), const((w2, d)), const((1, d)), const((1, d)),
                  const((d, dff)), const((d, dff)), const((dff, d)), const((1, d))],
        out_specs=pl.BlockSpec((tm, d), lambda i: (i, 0)),
        out_shape=jax.ShapeDtypeStruct((rows, d), F32),
        compiler_params=_cparams("parallel"),
        name="post_ffn",
    )(x, m1, m2, wo1, wo2, g1.reshape(1, d), g2.reshape(1, d), wg, wu, wd, g3.reshape(1, d))


def _s5_operators(a_re, a_im, log_dt, b_re, b_im, c_re, c_im, q):
    hp = lax.Precision.HIGHEST
    a_re, a_im = a_re.astype(F32), a_im.astype(F32)
    dt = jnp.exp(log_dt.astype(F32))[:, None]
    mag = jnp.exp(a_re * dt)
    ab_r, ab_i = mag * jnp.cos(a_im * dt), mag * jnp.sin(a_im * dt)
    den = a_re * a_re + a_im * a_im
    nr, ni = ab_r - 1.0, ab_i
    f_r = (nr * a_re + ni * a_im) / den
    f_i = (ni * a_re - nr * a_im) / den
    bb_r = f_r[..., None] * b_re - f_i[..., None] * b_im
    bb_i = f_r[..., None] * b_im + f_i[..., None] * b_re
    pw_r, pw_i = [jnp.ones_like(ab_r)], [jnp.zeros_like(ab_i)]
    for _ in range(q):
        r, i = pw_r[-1], pw_i[-1]
        pw_r.append(r * ab_r - i * ab_i)
        pw_i.append(r * ab_i + i * ab_r)
    pw_r, pw_i = jnp.stack(pw_r), jnp.stack(pw_i)
    g = a_re.shape[0]
    pt_r, pt_i = pw_r.transpose(1, 2, 0)[..., None], pw_i.transpose(1, 2, 0)[..., None]
    w_r = pt_r * bb_r[:, :, None, :] - pt_i * bb_i[:, :, None, :]
    w_i = pt_r * bb_i[:, :, None, :] + pt_i * bb_r[:, :, None, :]
    c2 = jnp.concatenate([c_re, -c_im], axis=-1)
    w2 = jnp.concatenate([w_r, w_i], axis=1).reshape(g, 2 * SSM_STATE, (q + 1) * SSM_GROUP)
    kk = jnp.einsum('gip,gpn->gin', c2, w2, precision=hp).reshape(g, SSM_GROUP, q + 1, SSM_GROUP)
    tau = jnp.arange(q)[None, :] - jnp.arange(q)[:, None]
    kt = kk[:, :, jnp.clip(tau, 0, q), :]
    kt = jnp.where((tau >= 0)[None, None, :, :, None], kt, 0.0)
    toep = kt.transpose(0, 2, 4, 3, 1).reshape(g, q * SSM_GROUP, q * SSM_GROUP)
    rev = q - 1 - jnp.arange(q)
    bst_r = w_r[:, :, rev, :].transpose(0, 2, 3, 1).reshape(g, q * SSM_GROUP, SSM_STATE)
    bst_i = w_i[:, :, rev, :].transpose(0, 2, 3, 1).reshape(g, q * SSM_GROUP, SSM_STATE)
    ar, ai = pw_r[1:], pw_i[1:]
    co_r = c_re[None] * ar[:, :, None, :] - c_im[None] * ai[:, :, None, :]
    co_i = -c_re[None] * ai[:, :, None, :] - c_im[None] * ar[:, :, None, :]
    cout_r = co_r.transpose(1, 3, 0, 2).reshape(g, SSM_STATE, q * SSM_GROUP)
    cout_i = co_i.transpose(1, 3, 0, 2).reshape(g, SSM_STATE, q * SSM_GROUP)
    return (toep.astype(BF16), bst_r.astype(BF16), bst_i.astype(BF16), cout_r.astype(BF16),
            cout_i.astype(BF16), pw_r[q][:, None, :], pw_i[q][:, None, :], kk[:, :, :q, :].astype(BF16))


def _s5_scan_kernel(u_ref, toep_ref, bsr_ref, bsi_ref, cor_ref, coi_ref, aqr_ref, aqi_ref, h0r_ref, h0i_ref,
                    y_ref, hlr_ref, hli_ref, sr_ref, si_ref, hpr_ref, hpi_ref, *, gs, nc, nb):
    for g in range(gs):
        u = u_ref[g].astype(BF16)
        sr_ref[g] = _dot(u, bsr_ref[g])
        si_ref[g] = _dot(u, bsi_ref[g])

    def step(c, carry):
        rows = pl.ds(pl.multiple_of(c * nb, SUBLANES), nb)
        new = []
        for g in range(gs):
            hr, hi = carry[2 * g], carry[2 * g + 1]
            hpr_ref[g, rows, :] = hr
            hpi_ref[g, rows, :] = hi
            ar, ai = aqr_ref[g], aqi_ref[g]
            new.append(ar * hr - ai * hi + sr_ref[g, rows, :])
            new.append(ar * hi + ai * hr + si_ref[g, rows, :])
        return tuple(new)

    init = []
    for g in range(gs):
        init += [h0r_ref[g], h0i_ref[g]]
    fin = lax.fori_loop(0, nc, step, tuple(init))
    for g in range(gs):
        hlr_ref[g] = fin[2 * g]
        hli_ref[g] = fin[2 * g + 1]
        u = u_ref[g].astype(BF16)
        y_ref[g] = (_dot(u, toep_ref[g]) + _dot(hpr_ref[g].astype(BF16), cor_ref[g])
                    + _dot(hpi_ref[g].astype(BF16), coi_ref[g]))


def _s5_scan(u, h0r, h0i, ops, q, gs=4):
    b, l, _ = u.shape
    nc = l // q
    qw = q * SSM_GROUP
    toep, bsr, bsi, cor, coi, aqr, aqi = ops[:7]
    ug = u.reshape(b, nc, q, SSM_GROUPS, SSM_GROUP).transpose(3, 1, 0, 2, 4).reshape(SSM_GROUPS, nc * b, qw)
    h0r_g = h0r.transpose(1, 0, 2)
    h0i_g = h0i.transpose(1, 0, 2)
    rows = nc * b

    def gspec(r, c):
        return pl.BlockSpec((gs, r, c), lambda i: (i, 0, 0))

    kern = functools.partial(_s5_scan_kernel, gs=gs, nc=nc, nb=b)
    y, hlr, hli = pl.pallas_call(
        kern,
        grid=(SSM_GROUPS // gs,),
        in_specs=[gspec(rows, qw), gspec(qw, qw), gspec(qw, SSM_STATE), gspec(qw, SSM_STATE),
                  gspec(SSM_STATE, qw), gspec(SSM_STATE, qw), gspec(1, SSM_STATE), gspec(1, SSM_STATE),
                  gspec(b, SSM_STATE), gspec(b, SSM_STATE)],
        out_specs=[gspec(rows, qw), gspec(b, SSM_STATE), gspec(b, SSM_STATE)],
        out_shape=[jax.ShapeDtypeStruct((SSM_GROUPS, rows, qw), F32),
                   jax.ShapeDtypeStruct((SSM_GROUPS, b, SSM_STATE), F32),
                   jax.ShapeDtypeStruct((SSM_GROUPS, b, SSM_STATE), F32)],
        scratch_shapes=[pltpu.VMEM((gs, rows, SSM_STATE), F32) for _ in range(4)],
        compiler_params=_cparams("parallel"),
        name="s5_scan_groups",
    )(ug, toep, bsr, bsi, cor, coi, aqr, aqi, h0r_g, h0i_g)
    y = y.reshape(SSM_GROUPS, nc, b, q, SSM_GROUP).transpose(2, 1, 3, 0, 4).reshape(b, l, SSM_WIDTH)
    return y, hlr.transpose(1, 0, 2), hli.transpose(1, 0, 2)


OCT = LANES // SSM_GROUP
N_OCT = SSM_GROUPS // OCT
OCT_STATE = OCT * SSM_STATE


def _s5_octet_operators(ops, q):
    _, bsr, bsi, cor, coi, aqr, aqi, kk = ops
    qq = q * LANES
    dt = kk.dtype

    def spread(x, tile_np, keep_np):
        y = jnp.einsum('...k,kn->...n', x, jnp.asarray(tile_np, dt), preferred_element_type=F32)
        return (y * jnp.asarray(keep_np, F32)).astype(dt)

    grp_of_gj = np.arange(LANES) // SSM_GROUP
    grp_of_gp = np.arange(OCT_STATE) // SSM_STATE
    lag = kk.reshape(N_OCT, OCT, SSM_GROUP, q, SSM_GROUP)
    lag = lag.transpose(0, 3, 1, 4, 2).reshape(N_OCT, q, LANES, SSM_GROUP)
    k8 = spread(lag, np.tile(np.eye(SSM_GROUP), (1, OCT)), grp_of_gj[:, None] == grp_of_gj[None, :])

    def bst8(b):
        b = b.reshape(N_OCT, OCT, q, SSM_GROUP, SSM_STATE).transpose(0, 2, 1, 3, 4).reshape(N_OCT, qq, SSM_STATE)
        keep = np.tile(grp_of_gj, q)[:, None] == grp_of_gp[None, :]
        return spread(b, np.tile(np.eye(SSM_STATE), (1, OCT)), keep)

    def cout8(c):
        c = c.reshape(N_OCT, OCT_STATE, q * SSM_GROUP)
        t_of, i_of = np.arange(q * SSM_GROUP) // SSM_GROUP, np.arange(q * SSM_GROUP) % SSM_GROUP
        col_t, col_i = np.arange(qq) // LANES, np.arange(qq) % SSM_GROUP
        tile = (t_of[:, None] == col_t[None, :]) & (i_of[:, None] == col_i[None, :])
        keep = grp_of_gp[:, None] == np.tile(grp_of_gj, q)[None, :]
        return spread(c, tile.astype(np.float32), keep)

    return (k8, bst8(bsr), bst8(bsi), cout8(cor), cout8(coi),
            aqr.reshape(N_OCT, 1, OCT_STATE), aqi.reshape(N_OCT, 1, OCT_STATE))


def _s5_octet_kernel(x_ref, lag_ref, bsr_ref, bsi_ref, cor_ref, coi_ref, aqr_ref, aqi_ref, h0r_ref, h0i_ref,
                     y_ref, hlr_ref, hli_ref, sr_ref, si_ref, hpr_ref, hpi_ref, hr_ref, hi_ref, toep_ref,
                     *, ncb, nb, q):
    @pl.when(pl.program_id(1) == 0)
    def _():
        hr_ref[...] = h0r_ref[0]
        hi_ref[...] = h0i_ref[0]
        zero = jnp.zeros((LANES, LANES), BF16)
        for s in range(q):
            for t in range(q):
                toep_ref[s * LANES:(s + 1) * LANES, t * LANES:(t + 1) * LANES] = (
                    lag_ref[0, t - s] if t >= s else zero)

    x = x_ref[0]
    sr_ref[...] = _dot(x, bsr_ref[0])
    si_ref[...] = _dot(x, bsi_ref[0])
    ar, ai = aqr_ref[0], aqi_ref[0]

    def step(c, carry):
        hr, hi = carry
        rows = pl.ds(pl.multiple_of(c * nb, SUBLANES), nb)
        hpr_ref[rows, :] = hr
        hpi_ref[rows, :] = hi
        return ar * hr - ai * hi + sr_ref[rows, :], ar * hi + ai * hr + si_ref[rows, :]

    hr, hi = lax.fori_loop(0, ncb, step, (hr_ref[...], hi_ref[...]))
    hr_ref[...] = hr
    hi_ref[...] = hi
    hlr_ref[0] = hr
    hli_ref[0] = hi
    y_ref[0] = (_dot(x, toep_ref[...]) + _dot(hpr_ref[...].astype(BF16), cor_ref[0])
                + _dot(hpi_ref[...].astype(BF16), coi_ref[0]))


def _s5_scan_octets(u, h0r, h0i, ops, q, row_blocks=2):
    b, l, _ = u.shape
    nc = l // q
    qq = q * LANES
    rows = nc * b
    rb = rows // row_blocks
    ncb = nc // row_blocks
    ops8 = _s5_octet_operators(ops, q)
    x = u.reshape(b, nc, q, N_OCT, LANES).transpose(3, 1, 0, 2, 4).reshape(N_OCT, rows, qq).astype(BF16)
    h0r8 = h0r.reshape(b, N_OCT, OCT_STATE).transpose(1, 0, 2)
    h0i8 = h0i.reshape(b, N_OCT, OCT_STATE).transpose(1, 0, 2)

    def wspec(r, c):
        return pl.BlockSpec((1, r, c), lambda o, i: (o, 0, 0))

    kern = functools.partial(_s5_octet_kernel, ncb=ncb, nb=b, q=q)
    y, hlr, hli = pl.pallas_call(
        kern,
        grid=(N_OCT, row_blocks),
        in_specs=[pl.BlockSpec((1, rb, qq), lambda o, i: (o, i, 0)),
                  pl.BlockSpec((1, q, LANES, LANES), lambda o, i: (o, 0, 0, 0)),
                  wspec(qq, OCT_STATE), wspec(qq, OCT_STATE), wspec(OCT_STATE, qq),
                  wspec(OCT_STATE, qq), wspec(1, OCT_STATE), wspec(1, OCT_STATE),
                  wspec(b, OCT_STATE), wspec(b, OCT_STATE)],
        out_specs=[pl.BlockSpec((1, rb, qq), lambda o, i: (o, i, 0)), wspec(b, OCT_STATE), wspec(b, OCT_STATE)],
        out_shape=[jax.ShapeDtypeStruct((N_OCT, rows, qq), F32),
                   jax.ShapeDtypeStruct((N_OCT, b, OCT_STATE), F32),
                   jax.ShapeDtypeStruct((N_OCT, b, OCT_STATE), F32)],
        scratch_shapes=[pltpu.VMEM((rb, OCT_STATE), F32) for _ in range(4)]
        + [pltpu.VMEM((b, OCT_STATE), F32) for _ in range(2)] + [pltpu.VMEM((qq, qq), BF16)],
        compiler_params=_cparams("parallel", "arbitrary"),
        name="s5_scan_octets",
    )(x, *ops8, h0r8, h0i8)
    y = y.reshape(N_OCT, nc, b, q, LANES).transpose(2, 1, 3, 0, 4).reshape(b, l, SSM_WIDTH)
    unpack = lambda h: h.transpose(1, 0, 2).reshape(b, SSM_GROUPS, SSM_STATE)
    return y, unpack(hlr), unpack(hli)


def _s5_glu_kernel(y_ref, u_ref, d_ref, w_ref, b_ref, o_ref):
    v = _gelu(y_ref[...] + d_ref[...] * u_ref[...])
    o_ref[...] = v * jax.nn.sigmoid(_dot(v.astype(BF16), w_ref[...]) + b_ref[...])


def _s5_glu(y, u, d, w, bias, tm=512):
    rows, n = y.shape
    tm = min(tm, rows)
    return pl.pallas_call(
        _s5_glu_kernel,
        grid=(rows // tm,),
        in_specs=[pl.BlockSpec((tm, n), lambda i: (i, 0)), pl.BlockSpec((tm, n), lambda i: (i, 0)),
                  pl.BlockSpec((1, n), lambda i: (0, 0)), pl.BlockSpec((n, n), lambda i: (0, 0)),
                  pl.BlockSpec((1, n), lambda i: (0, 0))],
        out_specs=pl.BlockSpec((tm, n), lambda i: (i, 0)),
        out_shape=jax.ShapeDtypeStruct((rows, n), F32),
        compiler_params=_cparams("parallel"),
        name="s5_glu",
    )(y, u, d.reshape(1, n), w, bias.reshape(1, n))


CMP_PAGES = 16
CMP_PITCH = PAGE // CMP_STRIDE + 1
CMP_GROUP_SIZES = (4, 6, 6)


def _cmp_group_pages():
    out, start = [], 0
    for size in CMP_GROUP_SIZES:
        out.append(range(start, start + size))
        start += size
    return out
SUBS = PAGE // CMP_STRIDE


def _cmp_weights(pe, w1, b1, w2):
    w = w1.reshape(2, CMP_STRIDE // 2, 2, 2, HEAD_DIM, CMP_HIDDEN)
    eye = jnp.eye(2, dtype=w1.dtype)
    wp = jnp.einsum('apjcef,cd->pjcedaf', w, eye)
    wp = wp.reshape(CMP_STRIDE // 2, 2 * 2 * HEAD_DIM, 2 * 2 * CMP_HIDDEN)
    bias = (jnp.einsum('jce,jcef->cf', pe, w1, precision=lax.Precision.HIGHEST) + b1).reshape(1, 2 * CMP_HIDDEN)
    w2b = jnp.einsum('cfe,cd->cfde', w2, eye).reshape(2 * CMP_HIDDEN, 2 * HEAD_DIM)
    return wp.astype(BF16), bias.astype(F32), w2b.astype(BF16)


def _cmp_tokens_kernel(pt_ref, *refs, transposed):
    n = CMP_PAGES * SUBS
    if transposed:
        nblk = CMP_PAGES
        page_refs = refs[:nblk]
        wp_ref, bias_ref, w2_ref, o_ref, carry_ref = refs[nblk:nblk + 5]
        xs_refs = refs[nblk + 5:]
        group_pages = _cmp_group_pages()
        where = {k: (gi, (k - pages[0]) * KV_HEADS) for gi, pages in enumerate(group_pages) for k in pages}

        def stage(pages):
            for k in pages:
                gi, base = where[k]
                for h in range(KV_HEADS):
                    x = page_refs[k][0, h].T
                    for v in range(PAGE // SUBLANES):
                        n, j0 = divmod(v * SUBLANES, CMP_STRIDE)
                        xs_refs[gi][base + h, pl.ds(j0 * CMP_PITCH + n, SUBLANES, stride=CMP_PITCH), :] = (
                            x[v * SUBLANES:(v + 1) * SUBLANES, :])

        def rows_of(i, j):
            gi, base = where[i // KV_HEADS]
            return xs_refs[gi][base + i % KV_HEADS, pl.ds(j * CMP_PITCH, SUBS), :]
    else:
        nblk = CMP_PAGES * KV_HEADS
        page_refs = refs[:nblk]
        wp_ref, bias_ref, w2_ref, o_ref, carry_ref = refs[nblk:]

        def rows_of(i, j):
            return page_refs[i][0, pl.ds(j, SUBS, stride=CMP_STRIDE), :]

    @pl.when(pl.program_id(1) == 0)
    def _():
        carry_ref[...] = jnp.zeros_like(carry_ref)

    accs = []
    for pages in (_cmp_group_pages() if transposed else [range(CMP_PAGES)]):
        ng = len(pages) * SUBS
        if transposed:
            stage(pages)
        acc = None
        for jp in range(CMP_STRIDE // 2):
            halves = []
            for jj in range(2):
                j = 2 * jp + jj
                rows = [rows_of(k * KV_HEADS + h, j) for h in range(KV_HEADS) for k in pages]
                halves.append(jnp.concatenate(rows, axis=0))
            lhs = jnp.concatenate(halves, axis=1).astype(BF16)
            part = _dot(lhs, wp_ref[jp])
            acc = part if acc is None else acc + part
        accs.append(acc)
    row = lax.broadcasted_iota(jnp.int32, (n, 1), 0)
    outs = []
    for h in range(KV_HEADS):
        p = jnp.concatenate([a[h * (a.shape[0] // KV_HEADS):(h + 1) * (a.shape[0] // KV_HEADS)] for a in accs],
                            axis=0)
        first = jnp.concatenate([p[:, 0:CMP_HIDDEN], p[:, 2 * CMP_HIDDEN:3 * CMP_HIDDEN]], axis=1)
        second = jnp.concatenate([p[:, CMP_HIDDEN:2 * CMP_HIDDEN], p[:, 3 * CMP_HIDDEN:]], axis=1)
        prev = jnp.where(row == 0, carry_ref[h:h + 1, :], pltpu.roll(first, 1, axis=0))
        carry_ref[h:h + 1, :] = first[n - 1:n, :]
        hid = _gelu(prev + second + bias_ref[...])
        outs.append(_dot(hid.astype(BF16), w2_ref[...]))
    o_ref[0] = jnp.concatenate(outs, axis=1)


def _cmp_tokens(pool, page_table, wp, bias, w2b, transposed):
    b, n_pages = page_table.shape
    steps = n_pages // CMP_PAGES
    n = CMP_PAGES * SUBS

    def page_spec(k, h):
        return pl.BlockSpec((1, PAGE, 2 * HEAD_DIM), lambda i, s, pt: (pt[i, s * CMP_PAGES + k], 0, h))

    def page_spec_t(k):
        return pl.BlockSpec((1, KV_HEADS, 2 * HEAD_DIM, PAGE), lambda i, s, pt: (pt[i, s * CMP_PAGES + k], 0, 0, 0))

    scratch = [pltpu.VMEM((SUBLANES, 2 * CMP_HIDDEN), F32)]
    if transposed:
        scratch += [pltpu.VMEM((len(pages) * KV_HEADS, CMP_STRIDE * CMP_PITCH, 2 * HEAD_DIM), F32)
                    for pages in _cmp_group_pages()]
        page_specs = [page_spec_t(k) for k in range(CMP_PAGES)]
    else:
        page_specs = [page_spec(k, h) for k in range(CMP_PAGES) for h in range(KV_HEADS)]
    grid_spec = pltpu.PrefetchScalarGridSpec(
        num_scalar_prefetch=1,
        grid=(b, steps),
        in_specs=page_specs + [
            pl.BlockSpec(wp.shape, lambda i, s, pt: (0, 0, 0)),
            pl.BlockSpec(bias.shape, lambda i, s, pt: (0, 0)),
            pl.BlockSpec(w2b.shape, lambda i, s, pt: (0, 0))],
        out_specs=pl.BlockSpec((1, n, KV_WIDTH), lambda i, s, pt: (i, s, 0)),
        scratch_shapes=scratch,
    )
    return pl.pallas_call(
        functools.partial(_cmp_tokens_kernel, transposed=transposed),
        grid_spec=grid_spec,
        out_shape=jax.ShapeDtypeStruct((b, steps * n, KV_WIDTH), F32),
        compiler_params=_cparams("parallel", "arbitrary"),
        name="cmp_tokens_t" if transposed else "cmp_tokens",
    )(page_table, *([pool] * len(page_specs)), wp, bias, w2b)


def _cover_matrix(n_tok, n_sel):
    i = np.arange(n_tok)[:, None]
    start = (i - 1) * CMP_STRIDE
    sj = np.arange(n_sel)[None, :] * SEL_BLOCK
    cov = (start < sj + SEL_BLOCK) & (start + CMP_BLOCK > sj) & (i >= 1)
    return cov.astype(np.float32)


KT_UNROLL = 2


def _split_hi_lo(x):
    hi = x.astype(BF16)
    lo = (x - hi.astype(F32)).astype(BF16)
    return hi, lo


def _nsa_prompt_kernel(q_ref, ckv_ref, kslc_ref, kwin_ref, gates_ref, cover_ref, o_ref, sel_ref, acc_ref,
                       s_ref, p_ref, *, tq, tk):
    q0 = pl.program_id(1) * tq
    qt = (q_ref[0] * SCALE).T
    gt = jax.nn.sigmoid(gates_ref[0]).T
    qpos = q0 + lax.broadcasted_iota(jnp.int32, (1, tq), 1)
    lane = lax.broadcasted_iota(jnp.int32, (1, GROUP * tq), 1)
    n_cmp = ckv_ref.shape[1]
    n_sel = cover_ref.shape[0]
    per_tile = tk // SEL_BLOCK
    wide = GROUP * tq

    def tile4(x):
        return jnp.concatenate([x] * GROUP, axis=1)

    ik = lax.broadcasted_iota(jnp.int32, (tk, 1), 0)
    rel = ik - lax.broadcasted_iota(jnp.int32, (1, tq), 1)
    row64 = lax.broadcasted_iota(jnp.int32, (HEAD_DIM, 1), 0)
    lane128 = lax.broadcasted_iota(jnp.int32, (1, 2 * HEAD_DIM), 1)
    kfeat = jnp.where((lane128 == HEAD_DIM) | (lane128 == HEAD_DIM + 1), ik.astype(F32), 0.0)
    q4, slope4, slope2, qaug = [], [], [], []
    for h in range(KV_HEADS):
        qf = jnp.concatenate([qt[(h * GROUP + g) * HEAD_DIM:(h * GROUP + g + 1) * HEAD_DIM, :]
                              for g in range(GROUP)], axis=1)
        q4.append(qf.astype(BF16))
        sl = jnp.zeros((1, wide), F32)
        for g in range(GROUP):
            sl = jnp.where((lane >= g * tq) & (lane < (g + 1) * tq), SLOPES[h * GROUP + g], sl)
        slope4.append(sl)
        sl2 = sl * LOG2E
        slope2.append(sl2)
        sl_hi = sl2.astype(BF16).astype(F32)
        extra = jnp.where(row64 == 0, sl_hi, jnp.where(row64 == 1, sl2 - sl_hi, 0.0))
        qaug.append(jnp.concatenate([qf * LOG2E, extra], axis=0).astype(BF16))

    o_cmp = []
    slot = lax.broadcasted_iota(jnp.int32, (n_cmp, 1), 0)
    dist = qpos - ((slot + 1) * CMP_STRIDE - 1)
    cmask = tile4(((dist >= 0) & (slot >= 1)).astype(F32))
    cdist4 = tile4(dist.astype(F32))
    cov = cover_ref[...].astype(BF16)
    jj = lax.broadcasted_iota(jnp.int32, (n_sel, 1), 0)
    cur = qpos // SEL_BLOCK
    forced = (jj == 0) | (jj == cur) | (jj == cur - 1)
    valid = jj * SEL_BLOCK <= qpos
    for h in range(KV_HEADS):
        ckv = ckv_ref[0, :, h * 2 * HEAD_DIM:(h + 1) * 2 * HEAD_DIM]
        ck = ckv[:, :HEAD_DIM].astype(BF16)
        cvt = ckv.T[HEAD_DIM:, :].astype(BF16)
        s = _dot(ck, q4[h]) - slope4[h] * cdist4
        s = jnp.where(cmask > 0, s, NEG)
        p = jnp.exp(s - jnp.max(s, axis=0, keepdims=True)) * cmask
        p = p / jnp.maximum(jnp.sum(p, axis=0, keepdims=True), 1e-30)
        o_cmp.append(_dot(cvt, p.astype(BF16)))
        psum = p[:, 0:tq]
        for g in range(1, GROUP):
            psum = psum + p[:, g * tq:(g + 1) * tq]
        p_hi, p_lo = _split_hi_lo(psum)
        imp = _dot(cov, p_hi) + _dot(cov, p_lo)
        imp = jnp.where(forced, BIG, jnp.where(valid, imp, NEG))
        cnt = jnp.zeros((n_sel, tq), F32)
        for j2 in range(n_sel):
            r = imp[j2:j2 + 1, :]
            cnt = cnt + jnp.where(r > imp, 1.0, jnp.where((r == imp) & (j2 < jj), 1.0, 0.0))
        sel = jnp.where(cnt < SEL_TOP, BIG, NEG)
        for kk in range(n_sel // per_tile):
            sel_ref[h, kk] = sel[kk * per_tile:(kk + 1) * per_tile, :]

    half = ik // SEL_BLOCK
    ones_rows = jnp.ones((SUBLANES, tk), F32)

    def slc_cap(h, kt, d):
        rows = sel_ref[h, kt]
        cap = rows[0:1, :]
        for r in range(1, per_tile):
            cap = jnp.where(half == r, rows[r:r + 1, :], cap)
        return jnp.where(d >= 0, cap, NEG)

    def win_cap(h, kt, d):
        return jnp.where(d >= 0, jnp.where(d < WINDOW, BIG, NEG), NEG)

    hi = (q0 + tq + tk - 1) // tk
    lo_win = jnp.maximum(q0 - WINDOW, 0) // tk
    branches = ((kslc_ref, slc_cap, 0), (kwin_ref, win_cap, lo_win))
    n_chain = len(branches) * KV_HEADS
    for c in range(n_chain):
        acc_ref[c] = jnp.zeros((HEAD_DIM + SUBLANES, wide), F32)

    def make_body(active):
        def body(i, ms):
            ms = list(ms)
            tiles = []
            for sub in range(KT_UNROLL):
                kt_raw = KT_UNROLL * i + sub
                kt = jnp.minimum(kt_raw, hi - 1)
                k0 = pl.multiple_of(kt * tk, tk)
                tiles.append((kt_raw, kt, k0))
                for br in active:
                    kv_ref = branches[br][0]
                    for h in range(KV_HEADS):
                        kv = kv_ref[0, pl.ds(k0, tk), h * 2 * HEAD_DIM:(h + 1) * 2 * HEAD_DIM]
                        k_aug = jnp.where(lane128 < HEAD_DIM, kv, kfeat).astype(BF16)
                        s_ref[sub, br * KV_HEADS + h] = _dot(k_aug, qaug[h])
            for sub, (kt_raw, kt, k0) in enumerate(tiles):
                k0f = k0.astype(F32)
                for br in active:
                    kv_ref, cap_fn, lo_b = branches[br]
                    live = (kt_raw < hi) & (kt_raw >= lo_b)
                    d = jnp.where(live, (q0 - k0) - rel, -1)
                    alphas = []
                    for h in range(KV_HEADS):
                        c = br * KV_HEADS + h
                        sc = jnp.minimum(s_ref[sub, c], tile4(cap_fn(h, kt, d)))
                        off = slope2[h] * k0f
                        m_new = jnp.maximum(ms[c], jnp.max(sc, axis=0, keepdims=True) + off)
                        p_ref[sub, c] = jnp.exp2(sc - (m_new - off)).astype(BF16)
                        alphas.append(jnp.exp2(ms[c] - m_new))
                        ms[c] = m_new
                    for h in range(KV_HEADS):
                        c = br * KV_HEADS + h
                        kv = kv_ref[0, pl.ds(k0, tk), h * 2 * HEAD_DIM:(h + 1) * 2 * HEAD_DIM]
                        vt = jnp.concatenate([kv.T[HEAD_DIM:, :], ones_rows], axis=0).astype(BF16)
                        acc_ref[c] = alphas[h] * acc_ref[c] + _dot(vt, p_ref[sub, c])
            return tuple(ms)
        return body

    ms0 = tuple(jnp.full((1, wide), NEG, F32) for _ in range(n_chain))
    t_win = lo_win // KT_UNROLL
    t_end = (hi + KT_UNROLL - 1) // KT_UNROLL
    ms1 = lax.fori_loop(0, t_win, make_body((0,)), ms0)
    lax.fori_loop(t_win, t_end, make_body((0, 1)), ms1)
    o_att = [acc_ref[c, :HEAD_DIM, :] / jnp.maximum(acc_ref[c, HEAD_DIM:HEAD_DIM + 1, :], 1e-30)
             for c in range(n_chain)]
    o_slc, o_win = o_att[:KV_HEADS], o_att[KV_HEADS:]

    out_rows = []
    for h in range(KV_HEADS):
        for g in range(GROUP):
            hh = h * GROUP + g
            sl = slice(g * tq, (g + 1) * tq)
            out_rows.append(gt[hh:hh + 1, :] * o_cmp[h][:, sl]
                            + gt[NSA_HEADS + hh:NSA_HEADS + hh + 1, :] * o_slc[h][:, sl]
                            + gt[2 * NSA_HEADS + hh:2 * NSA_HEADS + hh + 1, :] * o_win[h][:, sl])
    o_ref[0] = jnp.concatenate(out_rows, axis=0).T


def _nsa_prompt_attn(q, ckv, kv_slc, kv_win, gates, tq=128, tk=128):
    b, t, _ = q.shape
    n_cmp = ckv.shape[1]
    n_sel = t // SEL_BLOCK
    cover_t = jnp.asarray(_cover_matrix(n_cmp, n_sel).T)
    kern = functools.partial(_nsa_prompt_kernel, tq=tq, tk=tk)
    return pl.pallas_call(
        kern,
        grid=(b, t // tq),
        in_specs=[pl.BlockSpec((1, tq, NSA_WIDTH), lambda i, j: (i, j, 0)),
                  pl.BlockSpec((1, n_cmp, KV_WIDTH), lambda i, j: (i, 0, 0)),
                  pl.BlockSpec((1, t, KV_WIDTH), lambda i, j: (i, 0, 0)),
                  pl.BlockSpec((1, t, KV_WIDTH), lambda i, j: (i, 0, 0)),
                  pl.BlockSpec((1, tq, LANES), lambda i, j: (i, j, 0)),
                  pl.BlockSpec((n_sel, n_cmp), lambda i, j: (0, 0))],
        out_specs=pl.BlockSpec((1, tq, NSA_WIDTH), lambda i, j: (i, j, 0)),
        out_shape=jax.ShapeDtypeStruct((b, t, NSA_WIDTH), F32),
        scratch_shapes=[pltpu.VMEM((KV_HEADS, n_sel * SEL_BLOCK // tk, tk // SEL_BLOCK, tq), F32),
                        pltpu.VMEM((2 * KV_HEADS, HEAD_DIM + SUBLANES, GROUP * tq), F32),
                        pltpu.VMEM((KT_UNROLL, 2 * KV_HEADS, tk, GROUP * tq), F32),
                        pltpu.VMEM((KT_UNROLL, 2 * KV_HEADS, tk, GROUP * tq), BF16)],
        compiler_params=_cparams("parallel", "parallel"),
        name="nsa_prompt_attn",
    )(q, ckv, kv_slc, kv_win, gates, cover_t)


TS = SUBLANES
SEL_LANES = 384


def _nsa_sample_select_kernel(q_ref, ckv_ref, cover_ref, ocmp_ref, idx_ref, *, nbatch, past_len, n_sel):
    n_cmp = ckv_ref.shape[1]
    rowi = lax.broadcasted_iota(jnp.int32, (GROUP * TS, 1), 0)
    tpos = past_len + rowi % TS
    slot = lax.broadcasted_iota(jnp.int32, (1, n_cmp), 1)
    dist = tpos - ((slot + 1) * CMP_STRIDE - 1)
    mask = ((dist >= 0) & (slot >= 1)).astype(F32)
    jj = lax.broadcasted_iota(jnp.int32, (1, SEL_LANES), 1)
    jjf = jj.astype(F32)
    qp8 = past_len + lax.broadcasted_iota(jnp.int32, (TS, 1), 0)
    cur = qp8 // SEL_BLOCK
    forced = (jj == 0) | (jj == cur) | (jj == cur - 1)
    valid = jj * SEL_BLOCK <= qp8
    lane = lax.broadcasted_iota(jnp.int32, (1, LANES), 1)
    cov = cover_ref[...].astype(BF16)
    chains = [(bb, h) for bb in range(nbatch) for h in range(KV_HEADS)]
    distf = dist.astype(F32)
    scores, probs, outs, imps = {}, {}, {}, {}
    for bb, h in chains:
        q = q_ref[bb] * SCALE
        qh = jnp.concatenate([q[:, (h * GROUP + g) * HEAD_DIM:(h * GROUP + g + 1) * HEAD_DIM]
                              for g in range(GROUP)], axis=0).astype(BF16)
        slope = jnp.zeros((GROUP * TS, 1), F32)
        for g in range(GROUP):
            slope = jnp.where(rowi // TS == g, SLOPES[h * GROUP + g], slope)
        ck = ckv_ref[bb, :, h * 2 * HEAD_DIM:h * 2 * HEAD_DIM + HEAD_DIM].astype(BF16)
        scores[bb, h] = jnp.where(mask > 0, _dot_nt(qh, ck) - slope * distf, NEG)
    for c in chains:
        s = scores[c]
        p = jnp.exp(s - jnp.max(s, axis=-1, keepdims=True)) * mask
        probs[c] = p / jnp.maximum(jnp.sum(p, axis=-1, keepdims=True), 1e-30)
    for bb, h in chains:
        p = probs[bb, h]
        cv = ckv_ref[bb, :, h * 2 * HEAD_DIM + HEAD_DIM:(h + 1) * 2 * HEAD_DIM].astype(BF16)
        outs[bb, h] = _dot(p.astype(BF16), cv)
        psum = p[0:TS]
        for g in range(1, GROUP):
            psum = psum + p[g * TS:(g + 1) * TS]
        p_hi, p_lo = _split_hi_lo(psum)
        imp = _dot(p_hi, cov) + _dot(p_lo, cov)
        imp = jnp.where(forced, BIG, jnp.where(valid, imp, NEG))
        imps[bb, h] = jnp.where(jj < n_sel, imp, -3e38)
    for bb in range(nbatch):
        ocmp_ref[bb] = jnp.concatenate([outs[bb, h][g * TS:(g + 1) * TS, :]
                                        for h in range(KV_HEADS) for g in range(GROUP)], axis=-1)
    idx_out = [jnp.zeros((TS, LANES), jnp.int32) for _ in range(nbatch)]
    for k in range(SEL_TOP):
        for bb in range(nbatch):
            for h in range(KV_HEADS):
                imp = imps[bb, h]
                best = jnp.max(imp, axis=-1, keepdims=True)
                pick = jnp.min(jnp.where(imp == best, jjf, float(SEL_LANES)), axis=-1, keepdims=True)
                idx_out[bb] = jnp.where(lane == h * SEL_TOP + k, pick.astype(jnp.int32), idx_out[bb])
                imps[bb, h] = jnp.where(jjf == pick, -3e38, imp)
    for bb in range(nbatch):
        idx_ref[bb] = idx_out[bb]


def _nsa_sample_select(q, ckv, past_len, nbatch=8):
    b = q.shape[0]
    n_cmp = ckv.shape[1]
    n_sel = -(-(past_len + 4) // SEL_BLOCK)
    cov = np.zeros((n_cmp, SEL_LANES), np.float32)
    cov[:, :n_sel] = _cover_matrix(n_cmp, n_sel)
    kern = functools.partial(_nsa_sample_select_kernel, nbatch=nbatch, past_len=past_len, n_sel=n_sel)
    return pl.pallas_call(
        kern,
        grid=(b // nbatch,),
        in_specs=[pl.BlockSpec((nbatch, TS, NSA_WIDTH), lambda i: (i, 0, 0)),
                  pl.BlockSpec((nbatch, n_cmp, KV_WIDTH), lambda i: (i, 0, 0)),
                  pl.BlockSpec((n_cmp, SEL_LANES), lambda i: (0, 0))],
        out_specs=[pl.BlockSpec((nbatch, TS, NSA_WIDTH), lambda i: (i, 0, 0)),
                   pl.BlockSpec((nbatch, TS, LANES), lambda i: (i, 0, 0))],
        out_shape=[jax.ShapeDtypeStruct((b, TS, NSA_WIDTH), F32),
                   jax.ShapeDtypeStruct((b, TS, LANES), jnp.int32)],
        compiler_params=_cparams("parallel"),
        name="nsa_sample_select",
    )(q, ckv, jnp.asarray(cov))


def _nsa_sample_attend_kernel(idx_ref, pt_ref, q_ref, ocmp_ref, gates_ref, knew_ref, wbuf_ref, wnew_ref,
                              pool_ref, o_ref, buf_ref, sem, *, t_dec, past_len):
    b = pl.program_id(0)
    last_blk = past_len // SEL_BLOCK
    per_page = PAGE // SEL_BLOCK

    nb = pl.num_programs(0)
    slot = b % 2

    def block_of(t, h, k, bb=None):
        bb = b if bb is None else bb
        return idx_ref[((bb * t_dec + t) * KV_HEADS + h) * SEL_TOP + k]

    def page_copies(bb, sl):
        out = []
        for t in range(t_dec):
            for h in range(KV_HEADS):
                for k in range(SEL_TOP):
                    page = pt_ref[bb, jnp.minimum(block_of(t, h, k, bb), last_blk - 1) // per_page]
                    out.append(pltpu.make_async_copy(pool_ref.at[page, h], buf_ref.at[sl, t, h, k], sem.at[sl]))
        return out

    @pl.when(b == 0)
    def _():
        for cp in page_copies(b, slot):
            cp.start()

    @pl.when(b + 1 < nb)
    def _():
        for cp in page_copies(b + 1, 1 - slot):
            cp.start()

    for cp in page_copies(b, slot):
        cp.wait()

    def rows_to_t(x):
        return jnp.concatenate([x, jnp.zeros((PAGE - TS, 2 * HEAD_DIM), F32)], axis=0).T

    rowg = lax.broadcasted_iota(jnp.int32, (SUBLANES, 1), 0)
    slope_col = [jnp.zeros((SUBLANES, 1), F32) for _ in range(KV_HEADS)]
    for h in range(KV_HEADS):
        for g in range(GROUP):
            slope_col[h] = jnp.where(rowg == g, SLOPES[h * GROUP + g], slope_col[h])
    n_tok = SEL_TOP * PAGE
    lane = lax.broadcasted_iota(jnp.int32, (1, n_tok), 1)
    row = lane % PAGE
    row_half = row // SEL_BLOCK
    slot_of = lane // PAGE
    n_win = wbuf_ref.shape[-1]
    wlane = lax.broadcasted_iota(jnp.int32, (1, n_win + PAGE), 1)
    wpos = jnp.where(wlane < n_win, past_len - n_win + wlane, past_len + wlane - n_win)

    def masked_softmax(s, msk):
        p = jnp.exp(s - jnp.max(s, axis=-1, keepdims=True)) * msk
        return p / jnp.maximum(jnp.sum(p, axis=-1, keepdims=True), 1e-30)

    pairs = [(t, h) for h in range(KV_HEADS) for t in range(t_dec)]
    vw_ts, v_ts, s_slc, s_win, m_slc, m_win = {}, {}, {}, {}, {}, {}
    for h in range(KV_HEADS):
        hs = slice(h * 2 * HEAD_DIM, (h + 1) * 2 * HEAD_DIM)
        new_t = rows_to_t(knew_ref[0, :, hs])
        wnew_t = rows_to_t(wnew_ref[0, :, hs])
        kw_t = jnp.concatenate([wbuf_ref[0, h, 0], wnew_t[:HEAD_DIM]], axis=1).astype(BF16)
        vw_ts[h] = jnp.concatenate([wbuf_ref[0, h, 1], wnew_t[HEAD_DIM:]], axis=1).astype(BF16)
        for t in range(t_dec):
            qpos = past_len + t
            qh = (q_ref[0, t, h] * SCALE).astype(BF16)
            k_tiles, v_tiles = [], []
            tok = jnp.zeros((1, n_tok), jnp.int32)
            want_half = jnp.zeros((1, n_tok), jnp.int32)
            for k in range(SEL_TOP):
                blk = block_of(t, h, k)
                is_new = blk >= last_blk
                k_tiles.append(jnp.where(is_new, new_t[:HEAD_DIM], buf_ref[slot, t, h, k, 0]))
                v_tiles.append(jnp.where(is_new, new_t[HEAD_DIM:], buf_ref[slot, t, h, k, 1]))
                tok = jnp.where(slot_of == k, (blk // per_page) * PAGE + row, tok)
                want_half = jnp.where(slot_of == k, blk % per_page, want_half)
            k_t = jnp.concatenate(k_tiles, axis=1).astype(BF16)
            v_ts[t, h] = jnp.concatenate(v_tiles, axis=1).astype(BF16)
            d = qpos - tok
            m_slc[t, h] = jnp.where(d >= 0, jnp.where(row_half == want_half, 1.0, 0.0), 0.0)
            s_slc[t, h] = jnp.where(m_slc[t, h] > 0, _dot(qh, k_t) - slope_col[h] * d.astype(F32), NEG)
            dw = qpos - wpos
            m_win[t, h] = jnp.where(dw >= 0, jnp.where(dw < WINDOW, 1.0, 0.0), 0.0)
            s_win[t, h] = jnp.where(m_win[t, h] > 0, _dot(qh, kw_t) - slope_col[h] * dw.astype(F32), NEG)
    p_slc = {c: masked_softmax(s_slc[c], m_slc[c]) for c in pairs}
    p_win = {c: masked_softmax(s_win[c], m_win[c]) for c in pairs}
    for t, h in pairs:
        o_slc = _dot_nt(p_slc[t, h].astype(BF16), v_ts[t, h])
        o_win = _dot_nt(p_win[t, h].astype(BF16), vw_ts[h])
        gts = jax.nn.sigmoid(gates_ref[0, t, h])
        o_ref[0, t, h] = (gts[:, 0:1] * ocmp_ref[0, t, h] + gts[:, 1:2] * o_slc + gts[:, 2:3] * o_win)


def _nsa_sample_attend(idx, page_table, q5, ocmp5, gates5, kv_slc_new, win_buf_t, kv_win_new, pool_slc_t,
                       t_dec, past_len):
    b = q5.shape[0]
    n_win = win_buf_t.shape[-1]
    blk5 = (1, t_dec, KV_HEADS, SUBLANES, HEAD_DIM)
    grid_spec = pltpu.PrefetchScalarGridSpec(
        num_scalar_prefetch=2,
        grid=(b,),
        in_specs=[pl.BlockSpec(blk5, lambda i, *_: (i, 0, 0, 0, 0)),
                  pl.BlockSpec(blk5, lambda i, *_: (i, 0, 0, 0, 0)),
                  pl.BlockSpec((1, t_dec, KV_HEADS, SUBLANES, LANES), lambda i, *_: (i, 0, 0, 0, 0)),
                  pl.BlockSpec((1, TS, KV_WIDTH), lambda i, *_: (i, 0, 0)),
                  pl.BlockSpec((1, KV_HEADS, 2, HEAD_DIM, n_win), lambda i, *_: (i, 0, 0, 0, 0)),
                  pl.BlockSpec((1, TS, KV_WIDTH), lambda i, *_: (i, 0, 0)),
                  pl.BlockSpec(memory_space=pl.ANY)],
        out_specs=pl.BlockSpec(blk5, lambda i, *_: (i, 0, 0, 0, 0)),
        scratch_shapes=[pltpu.VMEM((2, t_dec, KV_HEADS, SEL_TOP, 2, HEAD_DIM, PAGE), F32),
                        pltpu.SemaphoreType.DMA((2,))],
    )
    kern = functools.partial(_nsa_sample_attend_kernel, t_dec=t_dec, past_len=past_len)
    return pl.pallas_call(
        kern,
        grid_spec=grid_spec,
        out_shape=jax.ShapeDtypeStruct((b, t_dec, KV_HEADS, SUBLANES, HEAD_DIM), F32),
        compiler_params=_cparams("arbitrary"),
        name="nsa_sample_attend",
    )(idx[:, :t_dec, :KV_HEADS * SEL_TOP].reshape(-1), page_table, q5, ocmp5, gates5, kv_slc_new, win_buf_t,
      kv_win_new, pool_slc_t)


NSA_SPLITS = ((0, NSA_WIDTH), (NSA_WIDTH, KV_WIDTH), (NSA_WIDTH + KV_WIDTH, KV_WIDTH),
              (NSA_WIDTH + 2 * KV_WIDTH, KV_WIDTH), (NSA_WIDTH + 3 * KV_WIDTH, MEM_WIDTH),
              (NSA_WIDTH + 3 * KV_WIDTH + MEM_WIDTH, LANES))


def _nsa_in_weight(w_in):
    o = NSA_WIDTH + 3 * KV_WIDTH
    n_gate = 3 * NSA_HEADS
    gates = jnp.pad(w_in[:, o:o + n_gate], ((0, 0), (0, LANES - n_gate)))
    return jnp.concatenate([w_in[:, :o], w_in[:, o + n_gate:], gates], axis=1).astype(BF16)


def _to_heads5(x, t_dec):
    b = x.shape[0]
    w = x.shape[-1] // NSA_HEADS
    x = x[:, :t_dec].reshape(b, t_dec, KV_HEADS, GROUP, w)
    return jnp.pad(x, ((0, 0), (0, 0), (0, 0), (0, SUBLANES - GROUP), (0, 0)))


def kernel(x_prompt, x_sample, mem_prompt, state_ssm_re, state_ssm_im, cache_cmp_kv, cache_slc_kv, cache_win_kv, cache_mem_kv, page_table, norm_mix_pre, norm_mix_post, norm_ffn_pre, norm_ffn_post, w_out, w_mem_kv, w_ffn_in, w_ffn_out, ssm_w_in, ssm_a_re, ssm_a_im, ssm_log_dt, ssm_b_re, ssm_b_im, ssm_c_re, ssm_c_im, ssm_d, ssm_w_glu, ssm_b_glu, nsa_w_in, nsa_cmp_pe, nsa_cmp_w1, nsa_cmp_b1, nsa_cmp_w2):
    bp, seq, d = x_prompt.shape
    bs, t_dec, _ = x_sample.shape
    n_mem = mem_prompt.shape[1]
    depth = w_out.shape[0]
    past_len = page_table.shape[1] * PAGE
    d_ff = w_ffn_out.shape[1]
    chunk = 16

    xp = x_prompt.reshape(bp * seq, d)
    xs = jnp.pad(x_sample, ((0, 0), (0, TS - t_dec), (0, 0))).reshape(bs * TS, d)
    mem2d = mem_prompt.reshape(bp * n_mem, d)

    outs = {k: [] for k in ('ssm_re_p', 'ssm_im_p', 'ssm_re_s', 'ssm_im_s', 'cmp_p', 'slc_p', 'win_p',
                            'cmp_s', 'slc_s', 'win_s', 'mem_p')}
    for i in range(depth):
        j = i // 2
        mem_split = ((0, 2 * MEM_WIDTH),)
        mkv_p, mkv_p_t = _norm_proj(mem2d, norm_mix_pre[i], w_mem_kv[i].astype(BF16), mem_split, do_norm=False,
                                    t_splits=mem_split, seq=n_mem)
        mkv_p = mkv_p.reshape(bp, n_mem, 2 * MEM_WIDTH)
        outs['mem_p'].append(jnp.moveaxis(mkv_p_t.reshape(bp, MEM_HEADS, 2, HEAD_DIM, n_mem), -1, 1))
        mkv_s_t = _kv_cache_t(cache_mem_kv[i])
        if i % 2 == 0:
            w_in = ssm_w_in[j].astype(BF16)
            wglu = ssm_w_glu[j].astype(BF16)
            ssm_w = (ssm_a_re[j], ssm_a_im[j], ssm_log_dt[j], ssm_b_re[j], ssm_b_im[j], ssm_c_re[j], ssm_c_im[j])
            ssm_split = ((0, SSM_WIDTH), (SSM_WIDTH, MEM_WIDTH))
            u, qm, u_half = _norm_proj(xp, norm_mix_pre[i], w_in, ssm_split, h_splits=ssm_split[:1])
            zero = jnp.zeros((bp, SSM_GROUPS, SSM_STATE), F32)
            y, hlr, hli = _s5_scan_octets(u_half.reshape(bp, seq, SSM_WIDTH), zero, zero,
                                          _s5_operators(*ssm_w, chunk), chunk)
            m1p = _s5_glu(y.reshape(bp * seq, SSM_WIDTH), u, ssm_d[j], wglu, ssm_b_glu[j])
            m2p = _mem_attn(qm.reshape(bp, seq, MEM_WIDTH), mkv_p).reshape(bp * seq, MEM_WIDTH)
            outs['ssm_re_p'].append(hlr)
            outs['ssm_im_p'].append(hli)
            u, qm = _norm_proj(xs, norm_mix_pre[i], w_in, ssm_split)
            y, hlr, hli = _s5_scan(u.reshape(bs, TS, SSM_WIDTH)[:, :t_dec], state_ssm_re[j].astype(F32),
                                   state_ssm_im[j].astype(F32), _s5_operators(*ssm_w, t_dec), t_dec)
            y = jnp.pad(y, ((0, 0), (0, TS - t_dec), (0, 0)))
            m1s = _s5_glu(y.reshape(bs * TS, SSM_WIDTH), u, ssm_d[j], wglu, ssm_b_glu[j])
            m2s = _mem_attn_t(qm.reshape(bs, TS, MEM_WIDTH), mkv_s_t).reshape(bs * TS, MEM_WIDTH)
            outs['ssm_re_s'].append(hlr)
            outs['ssm_im_s'].append(hli)
        else:
            w_in = _nsa_in_weight(nsa_w_in[j])
            wp, cbias, w2b = _cmp_weights(nsa_cmp_pe[j], nsa_cmp_w1[j], nsa_cmp_b1[j], nsa_cmp_w2[j])
            kv_shape = (KV_HEADS, 2, HEAD_DIM)
            q, kc, ks, kw, qm, gates, kc_t, ks_t, kw_t = _norm_proj(
                xp, norm_mix_pre[i], w_in, NSA_SPLITS, t_splits=NSA_SPLITS[1:4], seq=seq)
            kc3, ks3, kw3 = (a.reshape(bp, seq, KV_WIDTH) for a in (kc, ks, kw))

            def rows_major(a_t):
                return jnp.moveaxis(a_t.reshape((bp,) + kv_shape + (a_t.shape[-1],)), -1, 1)
            ident = jnp.arange(bp * (seq // PAGE), dtype=jnp.int32).reshape(bp, seq // PAGE)
            ckv = _cmp_tokens(kc.reshape(bp * seq // PAGE, PAGE, KV_WIDTH), ident, wp, cbias, w2b, False)
            m1p = _nsa_prompt_attn(q.reshape(bp, seq, NSA_WIDTH), ckv, ks3, kw3,
                                   gates.reshape(bp, seq, LANES)).reshape(bp * seq, NSA_WIDTH)
            m2p = _mem_attn(qm.reshape(bp, seq, MEM_WIDTH), mkv_p).reshape(bp * seq, MEM_WIDTH)
            outs['cmp_p'].append(rows_major(kc_t))
            outs['slc_p'].append(rows_major(ks_t))
            outs['win_p'].append(rows_major(kw_t[:, :, seq - min(WINDOW, seq):]))
            q, kc, ks, kw, qm, gates = _norm_proj(xs, norm_mix_pre[i], w_in, NSA_SPLITS)
            kc3, ks3, kw3 = (a.reshape(bs, TS, KV_WIDTH) for a in (kc, ks, kw))
            pool_cmp_t = _kv_cache_t(cache_cmp_kv[j])
            pool_cmp_t = pool_cmp_t.reshape(pool_cmp_t.shape[0], KV_HEADS, 2 * HEAD_DIM, PAGE)
            pool_slc_t = _kv_cache_t(cache_slc_kv[j])
            win_buf_t = _kv_cache_t(cache_win_kv[j])
            ckv = _cmp_tokens(pool_cmp_t, page_table, wp, cbias, w2b, True)
            q3 = q.reshape(bs, TS, NSA_WIDTH)
            ocmp, idx = _nsa_sample_select(q3, ckv, past_len)
            gates3 = gates.reshape(bs, TS, LANES)[:, :, :3 * NSA_HEADS].reshape(bs, TS, 3, NSA_HEADS)
            gates5 = _to_heads5(gates3.transpose(0, 1, 3, 2).reshape(bs, TS, NSA_HEADS * 3), t_dec)
            gates5 = jnp.pad(gates5, ((0, 0),) * 4 + ((0, LANES - 3),))
            o5 = _nsa_sample_attend(idx, page_table, _to_heads5(q3, t_dec), _to_heads5(ocmp, t_dec), gates5,
                                    ks3, win_buf_t, kw3, pool_slc_t, t_dec, past_len)
            o = o5[:, :, :, :GROUP].reshape(bs, t_dec, NSA_WIDTH)
            m1s = jnp.pad(o, ((0, 0), (0, TS - t_dec), (0, 0))).reshape(bs * TS, NSA_WIDTH)
            m2s = _mem_attn_t(qm.reshape(bs, TS, MEM_WIDTH), mkv_s_t).reshape(bs * TS, MEM_WIDTH)
            outs['cmp_s'].append(kc3[:, :t_dec].reshape((bs, t_dec) + kv_shape))
            outs['slc_s'].append(ks3[:, :t_dec].reshape((bs, t_dec) + kv_shape))
            kv_w = jnp.concatenate([cache_win_kv[j], kw3[:, :t_dec].reshape((bs, t_dec) + kv_shape)], axis=1)
            n_keep = min(WINDOW, past_len + t_dec)
            outs['win_s'].append(kv_w[:, kv_w.shape[1] - n_keep:])
        wo = w_out[i].astype(BF16)
        wg = w_ffn_in[i][:, :d_ff].astype(BF16)
        wu = w_ffn_in[i][:, d_ff:].astype(BF16)
        wd = w_ffn_out[i].astype(BF16)
        w1 = m1p.shape[1]
        args = (wo[:w1], wo[w1:], norm_mix_post[i], norm_ffn_pre[i], wg, wu, wd, norm_ffn_post[i])
        xp = _post(xp, m1p, m2p, *args)
        xs = _post(xs, m1s, m2s, *args)

    st = lambda k: jnp.stack(outs[k])
    y_sample = xs.reshape(bs, TS, d)[:, :t_dec]
    return (xp.reshape(bp, seq, d), y_sample, st('ssm_re_p'), st('ssm_im_p'), st('ssm_re_s'), st('ssm_im_s'),
            st('cmp_p'), st('slc_p'), st('win_p'), st('cmp_s'), st('slc_s'), st('win_s'), st('mem_p'))
```
